```python
import math
import jax, jax.numpy as jnp
from jax import lax
import numpy as np


D_MODEL = 2048
BATCH = 2
SEQ = 16384
DEPTH = 1

HEAD_DIM = 128
NSA_HEADS = 8
NSA_KV_GROUPS = 2
NSA_HPG = NSA_HEADS // NSA_KV_GROUPS
CMP_LEN = 32
CMP_STRIDE = 16
SEL_LEN = 64
SEL_TOPK = 16
WINDOW = 512
CMP_HIDDEN = 256
NSA_Q_BLOCK = 64
NSA_WIDTH = NSA_HEADS * HEAD_DIM
DIFF_HEADS = 4
DIFF_V_DIM = 2 * HEAD_DIM
DIFF_Q_BLOCK = 128
DIFF_WIDTH = DIFF_HEADS * DIFF_V_DIM
REL_BUCKETS = 32
REL_MAX_EXACT = 16
REL_MAX_DIST = 128
N_BIAS_HEADS = NSA_HEADS + DIFF_HEADS
D_FF = -(-8 * D_MODEL // (3 * 256)) * 256

EPS = 1e-6
NEG = -1e30
BIG = 1e30

SPLIT_SIZES = (
    NSA_HEADS * HEAD_DIM,
    NSA_KV_GROUPS * HEAD_DIM,
    NSA_KV_GROUPS * HEAD_DIM,
    NSA_KV_GROUPS * HEAD_DIM,
    NSA_KV_GROUPS * HEAD_DIM,
    NSA_KV_GROUPS * HEAD_DIM,
    NSA_KV_GROUPS * HEAD_DIM,
    NSA_HEADS * 3,
    DIFF_HEADS * 2 * HEAD_DIM,
    DIFF_HEADS * 2 * HEAD_DIM,
    DIFF_HEADS * DIFF_V_DIM,
    2 * D_MODEL,
)
D_IN_PROJ = int(sum(SPLIT_SIZES))
SPLIT_POINTS = tuple(int(v) for v in np.cumsum(SPLIT_SIZES)[:-1])

kernel_name = 'hybrid_nsa_diffattn_gated_block'


def rms_norm(x, g):
    xf = x.astype(jnp.float32)
    y = xf * lax.rsqrt(jnp.mean(xf * xf, axis=-1, keepdims=True) + EPS)
    return (y * g.astype(jnp.float32)).astype(x.dtype)


def masked_softmax(logits, mask):
    z = jnp.where(mask, logits, NEG)
    m = jnp.max(z, axis=-1, keepdims=True)
    e = jnp.where(mask, jnp.exp(z - m), 0.0)
    return e / jnp.maximum(jnp.sum(e, axis=-1, keepdims=True), 1e-30)


def t5_bucket(rel):
    n = jnp.maximum(rel, 0)
    nf = jnp.maximum(n, 1).astype(jnp.float32)
    large = REL_MAX_EXACT + (jnp.log(nf / REL_MAX_EXACT) / math.log(REL_MAX_DIST / REL_MAX_EXACT)
                             * (REL_BUCKETS - REL_MAX_EXACT)).astype(jnp.int32)
    large = jnp.minimum(large, REL_BUCKETS - 1)
    return jnp.where(n < REL_MAX_EXACT, n, large)


def compress_blocks(kv, pe, w1, w2):
    B, G, S, D = kv.shape
    halves = kv.reshape(B, G, S // CMP_STRIDE, CMP_STRIDE, D)
    nxt = jnp.pad(halves[:, :, 1:], ((0, 0), (0, 0), (0, 1), (0, 0), (0, 0)))
    blocks = jnp.concatenate([halves, nxt], axis=3) + pe
    hid = jax.nn.gelu(blocks.reshape(B, G, S // CMP_STRIDE, CMP_LEN * D) @ w1)
    return hid @ w2


def nsa_mixer(q, k_cmp, v_cmp, k_slc, v_slc, k_win, v_win, gates, table_nsa):
    B, G, P, S, D = q.shape
    scale = D ** -0.5
    n_cmp = S // CMP_STRIDE
    n_sel = S // SEL_LEN
    top_n = min(SEL_TOPK, n_sel)
    r_m = SEL_LEN // CMP_STRIDE
    r_n = CMP_LEN // CMP_STRIDE
    overlap = [sum(1 for m in range(r_m) for n in range(r_n) if m + n == s)
               for s in range(r_m + r_n - 1)]
    k_blk = k_slc.reshape(B, G, n_sel, SEL_LEN, D)
    v_blk = v_slc.reshape(B, G, n_sel, SEL_LEN, D)
    wpad = ((0, 0), (0, 0), (WINDOW, 0), (0, 0))
    k_wp = jnp.pad(k_win, wpad)
    v_wp = jnp.pad(v_win, wpad)
    c_end = jnp.arange(n_cmp, dtype=jnp.int32) * CMP_STRIDE + CMP_LEN - 1
    blk_ids = jnp.arange(n_sel, dtype=jnp.int32)
    sel_off = jnp.arange(SEL_LEN, dtype=jnp.int32)
    bi = jnp.arange(B)[:, None, None, None]
    gi = jnp.arange(G)[None, :, None, None]
    gi5 = gi[..., None]
    span = NSA_Q_BLOCK + WINDOW

    def block(i):
        t0 = i * NSA_Q_BLOCK
        tpos = t0 + jnp.arange(NSA_Q_BLOCK, dtype=jnp.int32)
        qb = lax.dynamic_slice_in_dim(q, t0, NSA_Q_BLOCK, axis=3)
        gb = lax.dynamic_slice_in_dim(gates, t0, NSA_Q_BLOCK, axis=3)
        rel_c = tpos[:, None] - c_end[None, :]
        bias_c = jnp.transpose(table_nsa[t5_bucket(rel_c)], (2, 3, 0, 1)).astype(jnp.float32)
        s_c = jnp.einsum('bgpqd,bgcd->bgpqc', qb, k_cmp).astype(jnp.float32) * scale + bias_c
        p_c = masked_softmax(s_c, rel_c >= 0)
        o_c = jnp.einsum('bgpqc,bgcd->bgpqd', p_c.astype(v_cmp.dtype), v_cmp)
        imp = jnp.pad(p_c.sum(axis=2), ((0, 0), (0, 0), (0, 0), (0, len(overlap) - 1)))
        p_slc = sum(w * imp[..., s: s + n_cmp: r_m] for s, w in enumerate(overlap))
        j_t = (tpos // SEL_LEN)[:, None]
        forced = (blk_ids == 0) | (blk_ids == j_t) | (blk_ids == j_t - 1)
        score = jnp.where(forced, BIG, jnp.where(blk_ids > j_t, NEG, p_slc))
        _, idx = lax.top_k(score, top_n)
        ks = k_blk[bi, gi, idx]
        vs = v_blk[bi, gi, idx]
        spos = idx[..., None] * SEL_LEN + sel_off
        rel_s = tpos[:, None, None] - spos
        bias_s = jnp.moveaxis(table_nsa[t5_bucket(rel_s), gi5], -1, 2).astype(jnp.float32)
        s_s = jnp.einsum('bgpqd,bgqnld->bgpqnl', qb, ks).astype(jnp.float32) * scale + bias_s
        m_sel = top_n * SEL_LEN
        p_s = masked_softmax(s_s.reshape(B, G, P, NSA_Q_BLOCK, m_sel),
                             (rel_s >= 0).reshape(B, G, 1, NSA_Q_BLOCK, m_sel))
        o_s = jnp.einsum('bgpqm,bgqmd->bgpqd', p_s.astype(vs.dtype),
                         vs.reshape(B, G, NSA_Q_BLOCK, m_sel, D))
        kwb = lax.dynamic_slice_in_dim(k_wp, t0, span, axis=2)
        vwb = lax.dynamic_slice_in_dim(v_wp, t0, span, axis=2)
        wpos = t0 - WINDOW + jnp.arange(span, dtype=jnp.int32)
        rel_w = tpos[:, None] - wpos[None, :]
        mask_w = (rel_w >= 0) & (rel_w < WINDOW) & (wpos[None, :] >= 0)
        bias_w = jnp.transpose(table_nsa[t5_bucket(rel_w)], (2, 3, 0, 1)).astype(jnp.float32)
        s_w = jnp.einsum('bgpqd,bgkd->bgpqk', qb, kwb).astype(jnp.float32) * scale + bias_w
        p_w = masked_softmax(s_w, mask_w)
        o_w = jnp.einsum('bgpqk,bgkd->bgpqd', p_w.astype(vwb.dtype), vwb)
        return gb[..., 0:1] * o_c + gb[..., 1:2] * o_s + gb[..., 2:3] * o_w

    out = lax.map(block, jnp.arange(S // NSA_Q_BLOCK, dtype=jnp.int32))
    return out.transpose(1, 0, 4, 2, 3, 5).reshape(B, S, G * P * D)


def diff_mixer(q1, q2, k1, k2, v, lam, table_diff):
    B, H, S, D = q1.shape
    scale = D ** -0.5
    kpos = jnp.arange(S, dtype=jnp.int32)

    def block(i):
        t0 = i * DIFF_Q_BLOCK
        tpos = t0 + jnp.arange(DIFF_Q_BLOCK, dtype=jnp.int32)
        rel = tpos[:, None] - kpos[None, :]
        mask = rel >= 0
        bias = jnp.transpose(table_diff[t5_bucket(rel)], (2, 0, 1)).astype(jnp.float32)
        q1b = lax.dynamic_slice_in_dim(q1, t0, DIFF_Q_BLOCK, axis=2)
        q2b = lax.dynamic_slice_in_dim(q2, t0, DIFF_Q_BLOCK, axis=2)
        a1 = masked_softmax(jnp.einsum('bhqd,bhkd->bhqk', q1b, k1).astype(jnp.float32) * scale + bias, mask)
        a2 = masked_softmax(jnp.einsum('bhqd,bhkd->bhqk', q2b, k2).astype(jnp.float32) * scale + bias, mask)
        a = a1 - lam * a2
        return jnp.einsum('bhqk,bhkd->bhqd', a.astype(v.dtype), v)

    out = lax.map(block, jnp.arange(S // DIFF_Q_BLOCK, dtype=jnp.int32))
    return out.transpose(1, 2, 0, 3, 4).reshape(B, H, S, v.shape[-1])


def setup_inputs(seed: int = 0) -> dict:
    key = jax.random.key(seed)
    ks = jax.random.split(key, 24)

    def nrm(k, shape, scale):
        return jax.random.normal(k, shape, jnp.float32) * scale

    kvin = CMP_LEN * HEAD_DIM
    return {
        'x': nrm(ks[0], (BATCH, SEQ, D_MODEL), 1.0),
        'norm_mix_g': 1.0 + nrm(ks[1], (DEPTH, D_MODEL), 0.02),
        'w_in': nrm(ks[2], (DEPTH, D_MODEL, D_IN_PROJ), D_MODEL ** -0.5),
        'cmp_pe_k': nrm(ks[3], (DEPTH, CMP_LEN, HEAD_DIM), 0.1),
        'cmp_pe_v': nrm(ks[4], (DEPTH, CMP_LEN, HEAD_DIM), 0.1),
        'cmp_w1_k': nrm(ks[5], (DEPTH, kvin, CMP_HIDDEN), kvin ** -0.5),
        'cmp_w2_k': nrm(ks[6], (DEPTH, CMP_HIDDEN, HEAD_DIM), CMP_HIDDEN ** -0.5),
        'cmp_w1_v': nrm(ks[7], (DEPTH, kvin, CMP_HIDDEN), kvin ** -0.5),
        'cmp_w2_v': nrm(ks[8], (DEPTH, CMP_HIDDEN, HEAD_DIM), CMP_HIDDEN ** -0.5),
        'diff_lq1': nrm(ks[9], (DEPTH, HEAD_DIM), 0.1),
        'diff_lk1': nrm(ks[10], (DEPTH, HEAD_DIM), 0.1),
        'diff_lq2': nrm(ks[11], (DEPTH, HEAD_DIM), 0.1),
        'diff_lk2': nrm(ks[12], (DEPTH, HEAD_DIM), 0.1),
        'diff_head_g': 1.0 + nrm(ks[13], (DEPTH, DIFF_HEADS, DIFF_V_DIM), 0.02),
        'w_up_nsa': nrm(ks[14], (DEPTH, NSA_WIDTH, D_MODEL), NSA_WIDTH ** -0.5),
        'w_up_diff': nrm(ks[15], (DEPTH, DIFF_WIDTH, D_MODEL), DIFF_WIDTH ** -0.5),
        'w_out': nrm(ks[16], (DEPTH, D_MODEL, D_MODEL), D_MODEL ** -0.5),
        'norm_ff_g': 1.0 + nrm(ks[17], (DEPTH, D_MODEL), 0.02),
        'w_ff_in': nrm(ks[18], (DEPTH, D_MODEL, 2 * D_FF), D_MODEL ** -0.5),
        'w_ff_out': nrm(ks[19], (DEPTH, D_FF, D_MODEL), D_FF ** -0.5),
        'rel_bias_table': nrm(ks[20], (REL_BUCKETS, N_BIAS_HEADS), 0.1),
        'norm_final_g': 1.0 + nrm(ks[21], (D_MODEL,), 0.02),
    }


def reference(x, norm_mix_g, w_in, cmp_pe_k, cmp_pe_v, cmp_w1_k, cmp_w2_k, cmp_w1_v, cmp_w2_v,
              diff_lq1, diff_lk1, diff_lq2, diff_lk2, diff_head_g, w_up_nsa, w_up_diff, w_out,
              norm_ff_g, w_ff_in, w_ff_out, rel_bias_table, norm_final_g):
    B, S, _ = x.shape
    G, P, HD = NSA_KV_GROUPS, NSA_HPG, HEAD_DIM
    table_nsa = rel_bias_table[:, :NSA_HEADS].reshape(REL_BUCKETS, G, P)
    table_diff = rel_bias_table[:, NSA_HEADS:]

    def groups(t, n):
        return t.reshape(B, S, n, -1).transpose(0, 2, 1, 3)

    for l in range(DEPTH):
        h = rms_norm(x, norm_mix_g[l])
        proj = h @ w_in[l]
        (q_n, kc, vc, ksl, vsl, kw, vw, g_n, q_d, k_d, v_d, g_m) = jnp.split(proj, SPLIT_POINTS, axis=-1)
        q_n = q_n.reshape(B, S, G, P, HD).transpose(0, 2, 3, 1, 4)
        k_cmp = compress_blocks(groups(kc, G), cmp_pe_k[l], cmp_w1_k[l], cmp_w2_k[l])
        v_cmp = compress_blocks(groups(vc, G), cmp_pe_v[l], cmp_w1_v[l], cmp_w2_v[l])
        nsa_gates = jax.nn.sigmoid(g_n.reshape(B, S, G, P, 3).transpose(0, 2, 3, 1, 4))
        o_nsa = nsa_mixer(q_n, k_cmp, v_cmp, groups(ksl, G), groups(vsl, G),
                          groups(kw, G), groups(vw, G), nsa_gates, table_nsa)
        q_d = q_d.reshape(B, S, DIFF_HEADS, 2, HD)
        k_d = k_d.reshape(B, S, DIFF_HEADS, 2, HD)
        q1 = q_d[:, :, :, 0].transpose(0, 2, 1, 3)
        q2 = q_d[:, :, :, 1].transpose(0, 2, 1, 3)
        k1 = k_d[:, :, :, 0].transpose(0, 2, 1, 3)
        k2 = k_d[:, :, :, 1].transpose(0, 2, 1, 3)
        lam_init = 0.8 - 0.6 * math.exp(-0.3 * l)
        lam = (jnp.exp(jnp.sum(diff_lq1[l].astype(jnp.float32) * diff_lk1[l].astype(jnp.float32)))
               - jnp.exp(jnp.sum(diff_lq2[l].astype(jnp.float32) * diff_lk2[l].astype(jnp.float32)))
               + lam_init)
        o_d = diff_mixer(q1, q2, k1, k2, groups(v_d, DIFF_HEADS), lam, table_diff)
        o_d = rms_norm(o_d, diff_head_g[l][:, None, :]) * (1.0 - lam_init)
        o_d = o_d.transpose(0, 2, 1, 3).reshape(B, S, DIFF_WIDTH)
        g_a, g_b = jnp.split(jax.nn.sigmoid(g_m), 2, axis=-1)
        mixed = (g_a * (o_nsa @ w_up_nsa[l]) + g_b * (o_d @ w_up_diff[l])) @ w_out[l]
        x = x + mixed
        h = rms_norm(x, norm_ff_g[l])
        a, b = jnp.split(h @ w_ff_in[l], 2, axis=-1)
        x = x + (jax.nn.silu(a) * b) @ w_ff_out[l]
    return rms_norm(x, norm_final_g)
```

```python
import functools
import math

import jax
import jax.numpy as jnp
from jax import lax
from jax.experimental import pallas as pl
from jax.experimental.pallas import tpu as pltpu

D_MODEL = 2048
HEAD_DIM = 128
NSA_HEADS = 8
NSA_GROUPS = 2
NSA_HPG = NSA_HEADS // NSA_GROUPS
CMP_LEN = 32
CMP_STRIDE = 16
SEL_LEN = 64
SEL_TOPK = 16
WINDOW = 512
CMP_HIDDEN = 256
DIFF_HEADS = 4
DIFF_V_DIM = 2 * HEAD_DIM
REL_BUCKETS = 32
REL_MAX_EXACT = 16
REL_MAX_DIST = 128
D_FF = -(-8 * D_MODEL // (3 * 256)) * 256
EPS = 1e-6
NEG = -1e30
BIG = 1e30

F32 = jnp.float32
BF16 = jnp.bfloat16

VMEM_LIMIT_BYTES = 56 * 1024 * 1024

COL_QN = 0
COL_KC = 1024
COL_KS = 1536
COL_VS = 1792
COL_KW = 2048
COL_VW = 2304
COL_QD = 2560
COL_KD = 3584
COL_VD = 4608
COL_GM = 5632
N_PROJ = 9728

NSA_TQ = 256
NSA_TK = 1024
CMP_NEAR = 32
CMP_PAD_FRONT = 16
CMP_PAD_BACK = 112
DIFF_TQ = 512
DIFF_TK = 512


def _dot(a, b):
    return jnp.dot(a, b, preferred_element_type=F32)


def _dot_nt(a, b):
    return lax.dot_general(a, b, (((1,), (1,)), ((), ())), preferred_element_type=F32)


def _params(*sem):
    return pltpu.CompilerParams(dimension_semantics=sem, vmem_limit_bytes=VMEM_LIMIT_BYTES)


def _norm_rows(x, g):
    ms = jnp.mean(x * x, axis=-1, keepdims=True)
    return x * lax.rsqrt(ms + EPS) * g


def _in_proj_kernel(x_ref, g_ref, w_ref, cs_ref, wg_ref, o_ref, og_ref, h_ref):
    @pl.when(pl.program_id(1) == 0)
    def _():
        hb = _norm_rows(x_ref[...], g_ref[...]).astype(BF16)
        h_ref[...] = hb
        og_ref[...] = jax.nn.sigmoid(_dot(hb, wg_ref[...]))

    o_ref[...] = (_dot(h_ref[...], w_ref[...]) * cs_ref[...]).astype(o_ref.dtype)


def _in_proj(x2, g, w, cs, wg, tm=1024, tn=512):
    T = x2.shape[0]
    n = w.shape[1]
    ng = wg.shape[1]
    return pl.pallas_call(
        _in_proj_kernel,
        grid=(T // tm, n // tn),
        in_specs=[
            pl.BlockSpec((tm, D_MODEL), lambda i, j: (i, 0)),
            pl.BlockSpec((1, D_MODEL), lambda i, j: (0, 0)),
            pl.BlockSpec((D_MODEL, tn), lambda i, j: (0, j)),
            pl.BlockSpec((1, tn), lambda i, j: (0, j)),
            pl.BlockSpec((D_MODEL, ng), lambda i, j: (0, 0)),
        ],
        out_specs=[
            pl.BlockSpec((tm, tn), lambda i, j: (i, j)),
            pl.BlockSpec((tm, ng), lambda i, j: (i, 0)),
        ],
        out_shape=[
            jax.ShapeDtypeStruct((T, n), BF16),
            jax.ShapeDtypeStruct((T, ng), F32),
        ],
        scratch_shapes=[pltpu.VMEM((tm, D_MODEL), BF16)],
        compiler_params=_params("parallel", "arbitrary"),
    )(x2, g, w, cs, wg)


def _gelu_tanh(x):
    return 0.5 * x * (1.0 + jnp.tanh(math.sqrt(2.0 / math.pi) * (x + 0.044715 * (x * x * x))))


def _compress_kernel(h_ref, pe_ref, w1_ref, w2_ref, o_ref):
    hv = h_ref[0, 0, 0]
    nc = hv.shape[0]
    half = CMP_STRIDE * HEAD_DIM
    ya = _dot(hv, w1_ref[0, :half, :])
    yb = _dot(hv, w1_ref[0, half:, :])
    yb = pltpu.roll(yb, nc - 1, 0)
    row = lax.broadcasted_iota(jnp.int32, yb.shape, 0)
    yb = jnp.where(row == nc - 1, 0.0, yb)
    pe8 = jnp.broadcast_to(pe_ref[0], (8, 2 * half)).astype(BF16)
    pec = _dot(pe8, w1_ref[0])[0:1]
    hid = _gelu_tanh(ya + yb + pec)
    o_ref[0, 0, 0] = _dot(hid.astype(BF16), w2_ref[0]).astype(o_ref.dtype)


def _compress(halves, pe, w1, w2):
    _, B, G, nc, hw = halves.shape
    return pl.pallas_call(
        _compress_kernel,
        grid=(2, B, G),
        in_specs=[
            pl.BlockSpec((1, 1, 1, nc, hw), lambda s, b, g: (s, b, g, 0, 0)),
            pl.BlockSpec((1, 1, 2 * hw), lambda s, b, g: (s, 0, 0)),
            pl.BlockSpec((1, 2 * hw, CMP_HIDDEN), lambda s, b, g: (s, 0, 0)),
            pl.BlockSpec((1, CMP_HIDDEN, HEAD_DIM), lambda s, b, g: (s, 0, 0)),
        ],
        out_specs=pl.BlockSpec((1, 1, 1, nc, HEAD_DIM), lambda s, b, g: (s, b, g, 0, 0)),
        out_shape=jax.ShapeDtypeStruct((2, B, G, nc, HEAD_DIM), BF16),
        compiler_params=_params("parallel", "parallel", "parallel"),
    )(halves, pe, w1, w2)


def _online(state, s, v):
    m, l, acc = state
    m_new = jnp.maximum(m, jnp.max(s, axis=-1, keepdims=True))
    alpha = jnp.exp(m - m_new)
    p = jnp.exp(s - m_new)
    l = alpha * l + jnp.sum(p, axis=-1, keepdims=True)
    acc = alpha * acc + _dot(p.astype(BF16), v)
    return m_new, l, acc


def _fresh(s, v):
    m = jnp.max(s, axis=-1, keepdims=True)
    p = jnp.exp(s - m)
    return m, jnp.sum(p, axis=-1, keepdims=True), _dot(p.astype(BF16), v)


def _split_dot(a, w):
    hi = a.astype(BF16)
    lo = (a - hi.astype(F32)).astype(BF16)
    return _dot(hi, w) + _dot(lo, w)


def _nsa_kernel(q_ref, gate_ref, kc_ref, vc_ref, ks_ref, vs_ref,
                kw0_ref, kw1_ref, kw2_ref, vw0_ref, vw1_ref, vw2_ref,
                dc_ref, ds_ref, dw_ref, etall_ref, wfar_ref, o_ref, *, n_sel, top_n):
    qt = pl.program_id(2)
    TQ, P = NSA_TQ, NSA_HPG
    R = P * TQ
    blocks_per_tile = NSA_TQ // SEL_LEN
    qblk = q_ref[0]
    q4 = jnp.concatenate([qblk[:, p * HEAD_DIM:(p + 1) * HEAD_DIM] for p in range(P)], axis=0)

    kc = kc_ref[0, 0, 0]
    vc = vc_ref[0, 0, 0]
    ncp = kc.shape[0]
    cmp_per_tile = NSA_TQ // CMP_STRIDE
    s_far = _dot_nt(q4, kc)
    lane = lax.broadcasted_iota(jnp.int32, (R, ncp), 1)
    far_ok = (lane >= CMP_PAD_FRONT) & (lane < cmp_per_tile * qt)
    c0 = pl.multiple_of(cmp_per_tile * qt, cmp_per_tile)
    kn = kc_ref[0, 0, 0, pl.ds(c0, CMP_NEAR), :]
    vn = vc_ref[0, 0, 0, pl.ds(c0, CMP_NEAR), :]
    dcv = dc_ref[0, 0]
    near_ok = dcv > 0.5 * NEG
    s_near = _dot_nt(q4, kn) + dcv
    s_far = jnp.where(far_ok, s_far, NEG)
    m_c = jnp.maximum(jnp.max(s_far, axis=-1, keepdims=True),
                      jnp.max(s_near, axis=-1, keepdims=True))
    e_far = jnp.where(far_ok, jnp.exp(s_far - m_c), 0.0)
    e_near = jnp.where(near_ok, jnp.exp(s_near - m_c), 0.0)
    l_c = jnp.sum(e_far, axis=-1, keepdims=True) + jnp.sum(e_near, axis=-1, keepdims=True)
    inv_c = 1.0 / jnp.maximum(l_c, 1e-30)
    p_far = e_far * inv_c
    p_near = e_near * inv_c
    o_c = _dot(p_far.astype(BF16), vc) + _dot(p_near.astype(BF16), vn)

    imp_far = jnp.sum(p_far.reshape(P, TQ, ncp), axis=0)
    imp_near = jnp.sum(p_near.reshape(P, TQ, CMP_NEAR), axis=0)
    jn = lax.broadcasted_iota(jnp.int32, (CMP_NEAR, n_sel), 0)
    jj = lax.broadcasted_iota(jnp.int32, (CMP_NEAR, n_sel), 1)
    d = (cmp_per_tile * qt - CMP_PAD_FRONT + jn) - (SEL_LEN // CMP_STRIDE) * jj
    w_near = jnp.where((d == 0) | (d == 4), 1.0,
                       jnp.where((d >= 1) & (d <= 3), 2.0, 0.0)).astype(BF16)
    p_slc = _split_dot(imp_far, wfar_ref[...]) + _split_dot(imp_near, w_near)

    ti = lax.broadcasted_iota(jnp.int32, (TQ, n_sel), 0)
    bj = lax.broadcasted_iota(jnp.int32, (TQ, n_sel), 1)
    bjf = bj.astype(F32)
    jt = blocks_per_tile * qt + ti // SEL_LEN
    forced = (bj == 0) | (bj == jt) | (bj == jt - 1)
    score = jnp.where(forced, BIG, jnp.where(bj > jt, NEG, p_slc))

    def pick(_, carry):
        sc, sel = carry
        mx = jnp.max(sc, axis=-1, keepdims=True)
        first = jnp.min(jnp.where(sc == mx, bjf, float(n_sel)), axis=-1, keepdims=True)
        hit = bjf == first
        return jnp.where(hit, -jnp.inf, sc), jnp.where(hit, 0.0, sel)

    _, selneg = lax.fori_loop(0, top_n, pick, (score, jnp.full((TQ, n_sel), NEG, F32)))

    nb0 = jnp.maximum(blocks_per_tile * (qt - 1), 0)
    ns = pl.multiple_of(nb0 * SEL_LEN, NSA_TQ)
    near_keys = 2 * NSA_TQ
    ksn = ks_ref[0, pl.ds(ns, near_keys), :]
    vsn = vs_ref[0, pl.ds(ns, near_keys), :]
    ej = lax.broadcasted_iota(jnp.int32, (n_sel, near_keys), 0)
    el = lax.broadcasted_iota(jnp.int32, (n_sel, near_keys), 1)
    e_nearsel = (ej == nb0 + el // SEL_LEN).astype(BF16)
    sb = _dot(selneg.astype(BF16), e_nearsel)
    s = _dot_nt(q4, ksn) + ds_ref[0, 0].astype(F32)
    s = (s.reshape(P, TQ, near_keys) + sb[None]).reshape(R, near_keys)
    state = _fresh(s, vsn)

    sel_far = jnp.where(bj < blocks_per_tile * (qt - 1), selneg, NEG).astype(BF16)
    e_base = n_sel - NSA_TK // SEL_LEN

    def far_tile(kt, st):
        k0 = pl.multiple_of(kt * NSA_TK, NSA_TK)
        kk = ks_ref[0, pl.ds(k0, NSA_TK), :]
        vv = vs_ref[0, pl.ds(k0, NSA_TK), :]
        e0 = pl.multiple_of(e_base - (NSA_TK // SEL_LEN) * kt, NSA_TK // SEL_LEN)
        sbt = _dot(sel_far, etall_ref[pl.ds(e0, n_sel), :])
        st_s = _dot_nt(q4, kk)
        st_s = (st_s.reshape(P, TQ, NSA_TK) + sbt[None]).reshape(R, NSA_TK)
        return _online(st, st_s, vv)

    n_far = (qt + 2) // 4
    _, l_s, acc_s = lax.fori_loop(0, n_far, far_tile, state)
    o_s = acc_s / l_s

    kw = jnp.concatenate([kw0_ref[0], kw1_ref[0], kw2_ref[0]], axis=0)
    vw = jnp.concatenate([vw0_ref[0], vw1_ref[0], vw2_ref[0]], axis=0)
    s_w = _dot_nt(q4, kw) + dw_ref[0, 0].astype(F32)
    _, l_w, acc_w = _fresh(s_w, vw)
    o_w = acc_w / l_w

    gate = gate_ref[0]
    outs = []
    for p in range(P):
        rows = slice(p * TQ, (p + 1) * TQ)
        gc = gate[:, 3 * p:3 * p + 1]
        gs = gate[:, 3 * p + 1:3 * p + 2]
        gw = gate[:, 3 * p + 2:3 * p + 3]
        outs.append(gc * o_c[rows] + gs * o_s[rows] + gw * o_w[rows])
    o_ref[0] = jnp.concatenate(outs, axis=1).astype(o_ref.dtype)


def _nsa(proj3, gates3, cmp_pad, dc, ds, dw, etall, wfar, n_sel):
    B, S, _ = proj3.shape
    G = NSA_GROUPS
    ncp = cmp_pad.shape[3]
    top_n = min(SEL_TOPK, n_sel)
    qw = NSA_HPG * HEAD_DIM
    wblk = lambda col, back: pl.BlockSpec(
        (1, NSA_TQ, HEAD_DIM),
        lambda b, g, t, col=col, back=back: (b, jnp.maximum(t - back, 0), col // HEAD_DIM + g))
    kernel = functools.partial(_nsa_kernel, n_sel=n_sel, top_n=top_n)
    return pl.pallas_call(
        kernel,
        grid=(B, G, S // NSA_TQ),
        in_specs=[
            pl.BlockSpec((1, NSA_TQ, qw), lambda b, g, t: (b, t, g)),
            pl.BlockSpec((1, NSA_TQ, 128), lambda b, g, t: (b, t, g)),
            pl.BlockSpec((1, 1, 1, ncp, HEAD_DIM), lambda b, g, t: (0, b, g, 0, 0)),
            pl.BlockSpec((1, 1, 1, ncp, HEAD_DIM), lambda b, g, t: (1, b, g, 0, 0)),
            pl.BlockSpec((1, S, HEAD_DIM), lambda b, g, t: (b, 0, COL_KS // HEAD_DIM + g)),
            pl.BlockSpec((1, S, HEAD_DIM), lambda b, g, t: (b, 0, COL_VS // HEAD_DIM + g)),
            wblk(COL_KW, 2), wblk(COL_KW, 1), wblk(COL_KW, 0),
            wblk(COL_VW, 2), wblk(COL_VW, 1), wblk(COL_VW, 0),
            pl.BlockSpec((1, 1) + dc.shape[2:], lambda b, g, t: (jnp.minimum(t, 1), g, 0, 0)),
            pl.BlockSpec((1, 1) + ds.shape[2:], lambda b, g, t: (jnp.minimum(t, 1), g, 0, 0)),
            pl.BlockSpec((1, 1) + dw.shape[2:], lambda b, g, t: (jnp.minimum(t, 2), g, 0, 0)),
            pl.BlockSpec(etall.shape, lambda b, g, t: (0, 0)),
            pl.BlockSpec(wfar.shape, lambda b, g, t: (0, 0)),
        ],
        out_specs=pl.BlockSpec((1, NSA_TQ, qw), lambda b, g, t: (b, t, g)),
        out_shape=jax.ShapeDtypeStruct((B, S, NSA_HEADS * HEAD_DIM), BF16),
        compiler_params=_params("parallel", "parallel", "arbitrary"),
    )(proj3, gates3, cmp_pad, cmp_pad, proj3, proj3,
      proj3, proj3, proj3, proj3, proj3, proj3, dc, ds, dw, etall, wfar)


def _diff_kernel(q_ref, k_ref, v_ref, dd_ref, lq1_ref, lk1_ref, lq2_ref, lk2_ref, hg_ref,
                 o_ref, *, lam_init):
    qt = pl.program_id(2)
    q = q_ref[0]
    q1 = q[:, :HEAD_DIM]
    q2 = q[:, HEAD_DIM:]
    lam = (jnp.exp(jnp.sum(lq1_ref[...] * lk1_ref[...], axis=-1, keepdims=True))
           - jnp.exp(jnp.sum(lq2_ref[...] * lk2_ref[...], axis=-1, keepdims=True)) + lam_init)

    def scores(k0):
        k = k_ref[0, pl.ds(k0, DIFF_TK), :]
        v = v_ref[0, pl.ds(k0, DIFF_TK), :]
        return _dot_nt(q1, k[:, :HEAD_DIM]), _dot_nt(q2, k[:, HEAD_DIM:]), v

    t0 = pl.multiple_of(qt * DIFF_TQ, DIFF_TQ)
    d0 = dd_ref[0, :, DIFF_TK:].astype(F32)
    s1, s2, v = scores(t0)
    st1 = _fresh(s1 + d0, v)
    st2 = _fresh(s2 + d0, v)

    tp = pl.multiple_of(jnp.maximum(qt - 1, 0) * DIFF_TK, DIFF_TK)
    d1 = jnp.where(qt >= 1, dd_ref[0, :, :DIFF_TK].astype(F32), NEG)
    s1, s2, v = scores(tp)
    st1 = _online(st1, s1 + d1, v)
    st2 = _online(st2, s2 + d1, v)

    def far_tile(kt, st):
        a1, a2, vv = scores(pl.multiple_of(kt * DIFF_TK, DIFF_TK))
        return _online(st[0], a1, vv), _online(st[1], a2, vv)

    st1, st2 = lax.fori_loop(0, jnp.maximum(qt - 1, 0), far_tile, (st1, st2))
    o = st1[2] / st1[1] - lam * (st2[2] / st2[1])
    y = _norm_rows(o, hg_ref[0]) * (1.0 - lam_init)
    o_ref[0] = y.astype(o_ref.dtype)


def _diff(proj3, dd, lq1, lk1, lq2, lk2, hg, lam_init):
    B, S, _ = proj3.shape
    H = DIFF_HEADS
    w = 2 * HEAD_DIM
    vec = pl.BlockSpec((1, HEAD_DIM), lambda b, h, t: (0, 0))
    kernel = functools.partial(_diff_kernel, lam_init=lam_init)
    return pl.pallas_call(
        kernel,
        grid=(B, H, S // DIFF_TQ),
        in_specs=[
            pl.BlockSpec((1, DIFF_TQ, w), lambda b, h, t: (b, t, COL_QD // w + h)),
            pl.BlockSpec((1, S, w), lambda b, h, t: (b, 0, COL_KD // w + h)),
            pl.BlockSpec((1, S, w), lambda b, h, t: (b, 0, COL_VD // w + h)),
            pl.BlockSpec((1,) + dd.shape[1:], lambda b, h, t: (h, 0, 0)),
            vec, vec, vec, vec,
            pl.BlockSpec((1, 1, w), lambda b, h, t: (h, 0, 0)),
        ],
        out_specs=pl.BlockSpec((1, DIFF_TQ, w), lambda b, h, t: (b, t, h)),
        out_shape=jax.ShapeDtypeStruct((B, S, H * w), BF16),
        compiler_params=_params("parallel", "parallel", "arbitrary"),
    )(proj3, proj3, proj3, dd, lq1, lk1, lq2, lk2, hg)


def _merge_kernel(on_ref, od_ref, wn_ref, wd_ref, ga_ref, gb_ref, o_ref):
    u1 = _dot(on_ref[...], wn_ref[...])
    u2 = _dot(od_ref[...], wd_ref[...])
    ga = jax.nn.sigmoid(ga_ref[...].astype(F32))
    gb = jax.nn.sigmoid(gb_ref[...].astype(F32))
    o_ref[...] = (ga * u1 + gb * u2).astype(o_ref.dtype)


def _merge(o_nsa, o_d, wn, wd, proj, tm=1024, tn=512):
    T, kn = o_nsa.shape
    kd = o_d.shape[1]
    return pl.pallas_call(
        _merge_kernel,
        grid=(T // tm, D_MODEL // tn),
        in_specs=[
            pl.BlockSpec((tm, kn), lambda i, j: (i, 0)),
            pl.BlockSpec((tm, kd), lambda i, j: (i, 0)),
            pl.BlockSpec((kn, tn), lambda i, j: (0, j)),
            pl.BlockSpec((kd, tn), lambda i, j: (0, j)),
            pl.BlockSpec((tm, tn), lambda i, j: (i, COL_GM // tn + j)),
            pl.BlockSpec((tm, tn), lambda i, j: (i, (COL_GM + D_MODEL) // tn + j)),
        ],
        out_specs=pl.BlockSpec((tm, tn), lambda i, j: (i, j)),
        out_shape=jax.ShapeDtypeStruct((T, D_MODEL), BF16),
        compiler_params=_params("parallel", "arbitrary"),
    )(o_nsa, o_d, wn, wd, proj, proj)


def _out_proj_kernel(a_ref, w_ref, x_ref, o_ref):
    o_ref[...] = x_ref[...] + _dot(a_ref[...], w_ref[...])


def _out_proj(a, w, x2, tm=1024, tn=512):
    T, k = a.shape
    return pl.pallas_call(
        _out_proj_kernel,
        grid=(T // tm, D_MODEL // tn),
        in_specs=[
            pl.BlockSpec((tm, k), lambda i, j: (i, 0)),
            pl.BlockSpec((k, tn), lambda i, j: (0, j)),
            pl.BlockSpec((tm, tn), lambda i, j: (i, j)),
        ],
        out_specs=pl.BlockSpec((tm, tn), lambda i, j: (i, j)),
        out_shape=jax.ShapeDtypeStruct((T, D_MODEL), F32),
        compiler_params=_params("parallel", "arbitrary"),
    )(a, w, x2)


def _ffn_in_kernel(x_ref, g_ref, wa_ref, wb_ref, o_ref, h_ref):
    @pl.when(pl.program_id(1) == 0)
    def _():
        h_ref[...] = _norm_rows(x_ref[...], g_ref[...]).astype(BF16)

    h = h_ref[...]
    a = _dot(h, wa_ref[...])
    b = _dot(h, wb_ref[...])
    o_ref[...] = (a * jax.nn.sigmoid(a) * b).astype(o_ref.dtype)


def _ffn_in(x2, g, w, tm=1024, tn=512):
    T = x2.shape[0]
    nb = D_FF // tn
    return pl.pallas_call(
        _ffn_in_kernel,
        grid=(T // tm, nb),
        in_specs=[
            pl.BlockSpec((tm, D_MODEL), lambda i, j: (i, 0)),
            pl.BlockSpec((1, D_MODEL), lambda i, j: (0, 0)),
            pl.BlockSpec((D_MODEL, tn), lambda i, j: (0, j)),
            pl.BlockSpec((D_MODEL, tn), lambda i, j: (0, j + nb)),
        ],
        out_specs=pl.BlockSpec((tm, tn), lambda i, j: (i, j)),
        out_shape=jax.ShapeDtypeStruct((T, D_FF), BF16),
        scratch_shapes=[pltpu.VMEM((tm, D_MODEL), BF16)],
        compiler_params=_params("parallel", "arbitrary"),
    )(x2, g, w, w)


def _ffn_out_kernel(a_ref, w_ref, x_ref, g_ref, o_ref, acc_ref, *, final_norm):
    k = pl.program_id(1)

    @pl.when(k == 0)
    def _():
        acc_ref[...] = x_ref[...]

    acc_ref[...] += _dot(a_ref[...], w_ref[...])

    @pl.when(k == pl.num_programs(1) - 1)
    def _():
        y = acc_ref[...]
        o_ref[...] = _norm_rows(y, g_ref[...]) if final_norm else y


def _ffn_out(a, w, x2, g, final_norm, tm=512, tk=512):
    T = a.shape[0]
    kernel = functools.partial(_ffn_out_kernel, final_norm=final_norm)
    return pl.pallas_call(
        kernel,
        grid=(T // tm, D_FF // tk),
        in_specs=[
            pl.BlockSpec((tm, tk), lambda i, k: (i, k)),
            pl.BlockSpec((tk, D_MODEL), lambda i, k: (k, 0)),
            pl.BlockSpec((tm, D_MODEL), lambda i, k: (i, 0)),
            pl.BlockSpec((1, D_MODEL), lambda i, k: (0, 0)),
        ],
        out_specs=pl.BlockSpec((tm, D_MODEL), lambda i, k: (i, 0)),
        out_shape=jax.ShapeDtypeStruct((T, D_MODEL), F32),
        scratch_shapes=[pltpu.VMEM((tm, D_MODEL), F32)],
        compiler_params=_params("parallel", "arbitrary"),
    )(a, w, x2, g)


def _t5_bucket(rel):
    n = jnp.maximum(rel, 0)
    nf = jnp.maximum(n, 1).astype(F32)
    large = REL_MAX_EXACT + (jnp.log(nf / REL_MAX_EXACT) / math.log(REL_MAX_DIST / REL_MAX_EXACT)
                             * (REL_BUCKETS - REL_MAX_EXACT)).astype(jnp.int32)
    large = jnp.minimum(large, REL_BUCKETS - 1)
    return jnp.where(n < REL_MAX_EXACT, n, large)


def _bias_of_rel(table, rel, valid):
    shifted = table - table[REL_BUCKETS - 1:REL_BUCKETS]
    vals = jnp.moveaxis(shifted[_t5_bucket(rel)], -1, 0)
    return jnp.where(valid[None], vals, NEG)


def _nsa_tables(table_nsa):
    TQ = NSA_TQ
    i = jnp.arange(TQ, dtype=jnp.int32)[:, None]

    def stack(t):
        return t.reshape(NSA_GROUPS, NSA_HPG * TQ, t.shape[-1])

    jc = jnp.arange(CMP_NEAR, dtype=jnp.int32)[None, :]
    rel_c = i - CMP_STRIDE * (jc - CMP_PAD_FRONT) - (CMP_LEN - 1)
    dc = jnp.stack([
        stack(_bias_of_rel(table_nsa, rel_c, (rel_c >= 0) & (jc >= CMP_PAD_FRONT))),
        stack(_bias_of_rel(table_nsa, rel_c, rel_c >= 0)),
    ])
    j = jnp.arange(2 * TQ, dtype=jnp.int32)[None, :]
    ds = jnp.stack([
        stack(_bias_of_rel(table_nsa, i - j, i - j >= 0)),
        stack(_bias_of_rel(table_nsa, i - j + TQ, i - j + TQ >= 0)),
    ]).astype(BF16)
    jw = jnp.arange(WINDOW + TQ, dtype=jnp.int32)[None, :]
    rel_w = i - jw + WINDOW
    ok_w = (rel_w >= 0) & (rel_w < WINDOW)
    dw = jnp.stack([
        stack(_bias_of_rel(table_nsa, rel_w, ok_w & (jw >= WINDOW - c * TQ))) for c in range(3)
    ]).astype(BF16)
    return dc, ds, dw


def _diff_table(table_diff):
    i = jnp.arange(DIFF_TQ, dtype=jnp.int32)[:, None]
    j = jnp.arange(2 * DIFF_TK, dtype=jnp.int32)[None, :]
    rel = i - j + DIFF_TK
    return _bias_of_rel(table_diff, rel, rel >= 0).astype(BF16)


def _selection_constants(S):
    n_sel = S // SEL_LEN
    n_cmp = S // CMP_STRIDE
    blocks = NSA_TK // SEL_LEN
    r = jnp.arange(2 * n_sel, dtype=jnp.int32)[:, None]
    l = jnp.arange(NSA_TK, dtype=jnp.int32)[None, :]
    etall = (r - (n_sel - blocks) == l // SEL_LEN).astype(BF16)
    c = jnp.arange(n_cmp + CMP_PAD_FRONT + CMP_PAD_BACK, dtype=jnp.int32)[:, None] - CMP_PAD_FRONT
    jb = jnp.arange(n_sel, dtype=jnp.int32)[None, :]
    d = c - (SEL_LEN // CMP_STRIDE) * jb
    wfar = jnp.where((d == 0) | (d == 4), 1.0, jnp.where((d >= 1) & (d <= 3), 2.0, 0.0))
    wfar = jnp.where((c >= 0) & (c < n_cmp), wfar, 0.0).astype(BF16)
    return etall, wfar


def kernel(x, norm_mix_g, w_in, cmp_pe_k, cmp_pe_v, cmp_w1_k, cmp_w2_k, cmp_w1_v, cmp_w2_v,
           diff_lq1, diff_lk1, diff_lq2, diff_lk2, diff_head_g, w_up_nsa, w_up_diff, w_out,
           norm_ff_g, w_ff_in, w_ff_out, rel_bias_table, norm_final_g):
    B, S, D = x.shape
    T = B * S
    depth = w_in.shape[0]
    n_sel = S // SEL_LEN
    n_cmp = S // CMP_STRIDE
    scale = HEAD_DIM ** -0.5

    dc, ds, dw = _nsa_tables(rel_bias_table[:, :NSA_HEADS])
    dd = _diff_table(rel_bias_table[:, NSA_HEADS:])
    etall, wfar = _selection_constants(S)

    gn0 = COL_QD
    gn1 = gn0 + NSA_HEADS * 3
    cs = jnp.ones((1, N_PROJ), F32)
    cs = cs.at[:, COL_QN:COL_KC].set(scale).at[:, COL_QD:COL_KD].set(scale)

    x2 = x.reshape(T, D)
    for l in range(depth):
        w_main = jnp.concatenate([w_in[l][:, :gn0], w_in[l][:, gn1:]], axis=1).astype(BF16)
        wg = w_in[l][:, gn0:gn1].reshape(D, NSA_GROUPS, NSA_HPG * 3)
        wg = jnp.pad(wg, ((0, 0), (0, 0), (0, 128 - NSA_HPG * 3))).reshape(D, NSA_GROUPS * 128)
        proj, gates = _in_proj(x2, norm_mix_g[l][None], w_main, cs, wg.astype(BF16))
        proj3 = proj.reshape(B, S, N_PROJ)
        gates3 = gates.reshape(B, S, NSA_GROUPS * 128)

        halves = proj3[:, :, COL_KC:COL_KS].reshape(B, n_cmp, CMP_STRIDE, 2, NSA_GROUPS, HEAD_DIM)
        halves = halves.transpose(3, 0, 4, 1, 2, 5).reshape(
            2, B, NSA_GROUPS, n_cmp, CMP_STRIDE * HEAD_DIM)
        pe = jnp.stack([cmp_pe_k[l], cmp_pe_v[l]]).reshape(2, 1, CMP_LEN * HEAD_DIM)
        w1 = jnp.stack([cmp_w1_k[l], cmp_w1_v[l]]).astype(BF16)
        w2 = jnp.stack([cmp_w2_k[l], cmp_w2_v[l]]).astype(BF16)
        cmp_kv = _compress(halves, pe, w1, w2)
        cmp_pad = jnp.pad(cmp_kv, ((0, 0), (0, 0), (0, 0), (CMP_PAD_FRONT, CMP_PAD_BACK), (0, 0)))

        o_nsa = _nsa(proj3, gates3, cmp_pad, dc, ds, dw, etall, wfar, n_sel)

        lam_init = 0.8 - 0.6 * math.exp(-0.3 * l)
        o_d = _diff(proj3, dd, diff_lq1[l][None], diff_lk1[l][None], diff_lq2[l][None],
                    diff_lk2[l][None], diff_head_g[l][:, None, :], lam_init)

        mix = _merge(o_nsa.reshape(T, -1), o_d.reshape(T, -1),
                     w_up_nsa[l].astype(BF16), w_up_diff[l].astype(BF16), proj)
        x2 = _out_proj(mix, w_out[l].astype(BF16), x2)

        act = _ffn_in(x2, norm_ff_g[l][None], w_ff_in[l].astype(BF16))
        last = l == depth - 1
        x2 = _ffn_out(act, w_ff_out[l].astype(BF16), x2,
                      norm_final_g[None] if last else norm_ff_g[l][None], final_norm=last)
    if depth == 0:
        x2 = x2
    return x2.reshape(B, S, D)
```

```python
import functools
import math

import jax
import jax.numpy as jnp
from jax import lax
from jax.experimental import pallas as pl
from jax.experimental.pallas import tpu as pltpu

D_MODEL = 2048
HEAD_DIM = 128
NSA_HEADS = 8
NSA_GROUPS = 2
NSA_HPG = NSA_HEADS // NSA_GROUPS
CMP_LEN = 32
CMP_STRIDE = 16
SEL_LEN = 64
SEL_TOPK = 16
WINDOW = 512
CMP_HIDDEN = 256
DIFF_HEADS = 4
DIFF_V_DIM = 2 * HEAD_DIM
REL_BUCKETS = 32
REL_MAX_EXACT = 16
REL_MAX_DIST = 128
D_FF = -(-8 * D_MODEL // (3 * 256)) * 256
EPS = 1e-6
NEG = -1e30
BIG = 1e30
LOG2E = math.log2(math.e)

F32 = jnp.float32
BF16 = jnp.bfloat16
LANES = 128

VMEM_LIMIT_BYTES = 56 * 1024 * 1024

COL_QN = 0
COL_KC = 1024
COL_KS = 1536
COL_VS = 1792
COL_KW = 2048
COL_VW = 2304
COL_QD = 2560
COL_KD = 3584
COL_VD = 4608
COL_GM = 5632
N_PROJ = 9728

NSA_TQ = 256
NSA_TK = 1024
NSA_ROWS = 32
CMP_NEAR = 32
CMP_PAD_FRONT = 16
CMP_PAD_BACK = 112
DIFF_TQ = 512
DIFF_TK = 512
DIFF_ROWS = 64


def _dot(a, b):
    return jnp.dot(a, b, preferred_element_type=F32)


def _dot_nt(a, b):
    return lax.dot_general(a, b, (((1,), (1,)), ((), ())), preferred_element_type=F32)


def _params(*sem):
    return pltpu.CompilerParams(dimension_semantics=sem, vmem_limit_bytes=VMEM_LIMIT_BYTES)


def _norm_rows(x, g):
    ms = jnp.mean(x * x, axis=-1, keepdims=True)
    return x * lax.rsqrt(ms + EPS) * g


def _in_proj_kernel(x_ref, g_ref, w_ref, cs_ref, wg_ref, o_ref, og_ref, h_ref):
    @pl.when(pl.program_id(1) == 0)
    def _():
        hb = _norm_rows(x_ref[...], g_ref[...]).astype(BF16)
        h_ref[...] = hb
        og_ref[...] = jax.nn.sigmoid(_dot(hb, wg_ref[...]))

    o_ref[...] = (_dot(h_ref[...], w_ref[...]) * cs_ref[...]).astype(o_ref.dtype)


def _in_proj(x2, g, w, cs, wg, tm=1024, tn=512):
    T = x2.shape[0]
    n = w.shape[1]
    ng = wg.shape[1]
    return pl.pallas_call(
        _in_proj_kernel,
        grid=(T // tm, n // tn),
        in_specs=[
            pl.BlockSpec((tm, D_MODEL), lambda i, j: (i, 0)),
            pl.BlockSpec((1, D_MODEL), lambda i, j: (0, 0)),
            pl.BlockSpec((D_MODEL, tn), lambda i, j: (0, j)),
            pl.BlockSpec((1, tn), lambda i, j: (0, j)),
            pl.BlockSpec((D_MODEL, ng), lambda i, j: (0, 0)),
        ],
        out_specs=[
            pl.BlockSpec((tm, tn), lambda i, j: (i, j)),
            pl.BlockSpec((tm, ng), lambda i, j: (i, 0)),
        ],
        out_shape=[
            jax.ShapeDtypeStruct((T, n), BF16),
            jax.ShapeDtypeStruct((T, ng), F32),
        ],
        scratch_shapes=[pltpu.VMEM((tm, D_MODEL), BF16)],
        compiler_params=_params("parallel", "arbitrary"),
    )(x2, g, w, cs, wg)


def _gelu_tanh(x):
    return 0.5 * x * (1.0 + jnp.tanh(math.sqrt(2.0 / math.pi) * (x + 0.044715 * (x * x * x))))


def _compress_kernel(h_ref, pe_ref, w1_ref, w2_ref, o_ref):
    hv = h_ref[0, 0, 0]
    nc = hv.shape[0]
    half = CMP_STRIDE * HEAD_DIM
    ya = _dot(hv, w1_ref[0, :half, :])
    yb = _dot(hv, w1_ref[0, half:, :])
    yb = pltpu.roll(yb, nc - 1, 0)
    row = lax.broadcasted_iota(jnp.int32, yb.shape, 0)
    yb = jnp.where(row == nc - 1, 0.0, yb)
    pe8 = jnp.broadcast_to(pe_ref[0], (8, 2 * half)).astype(BF16)
    pec = _dot(pe8, w1_ref[0])[0:1]
    hid = _gelu_tanh(ya + yb + pec)
    o_ref[0, 0, 0] = _dot(hid.astype(BF16), w2_ref[0]).astype(o_ref.dtype)


def _compress(halves, pe, w1, w2):
    _, B, G, nc, hw = halves.shape
    return pl.pallas_call(
        _compress_kernel,
        grid=(2, B, G),
        in_specs=[
            pl.BlockSpec((1, 1, 1, nc, hw), lambda s, b, g: (s, b, g, 0, 0)),
            pl.BlockSpec((1, 1, 2 * hw), lambda s, b, g: (s, 0, 0)),
            pl.BlockSpec((1, 2 * hw, CMP_HIDDEN), lambda s, b, g: (s, 0, 0)),
            pl.BlockSpec((1, CMP_HIDDEN, HEAD_DIM), lambda s, b, g: (s, 0, 0)),
        ],
        out_specs=pl.BlockSpec((1, 1, 1, nc, HEAD_DIM), lambda s, b, g: (s, b, g, 0, 0)),
        out_shape=jax.ShapeDtypeStruct((2, B, G, nc, HEAD_DIM), BF16),
        compiler_params=_params("parallel", "parallel", "parallel"),
    )(halves, pe, w1, w2)


def _lanes(x, width):
    return jnp.concatenate([x] * (width // x.shape[1]), axis=1)


def _init_state(m_ref, l_ref, acc_ref):
    m_ref[...] = jnp.full(m_ref.shape, NEG, F32)
    l_ref[...] = jnp.zeros(l_ref.shape, F32)
    acc_ref[...] = jnp.zeros(acc_ref.shape, F32)


def _fold_tile(slot, width, v, s_ref, p_ref, m_ref, l_ref, acc_ref, chunk):
    n_rows = p_ref.shape[0]
    for r in range(0, n_rows, chunk):
        rows = slice(r, r + chunk)
        s = s_ref[slot, rows, :width]
        m_prev = m_ref[rows, :]
        m_new = jnp.maximum(m_prev, jnp.max(s, axis=-1, keepdims=True))
        alpha = jnp.exp2(m_prev - m_new)
        p = jnp.exp2(s - _lanes(m_new, width))
        l_ref[rows, :] = alpha * l_ref[rows, :] + jnp.sum(p, axis=-1, keepdims=True)
        m_ref[rows, :] = m_new
        acc_ref[rows, :] = acc_ref[rows, :] * _lanes(alpha, acc_ref.shape[1])
        p_ref[rows, :width] = p.astype(BF16)
    acc_ref[...] += _dot(p_ref[:, :width], v)


def _finish(l_ref, acc_ref):
    return acc_ref[...] / _lanes(l_ref[...], acc_ref.shape[1])


def _split_dot(a, w):
    hi = a.astype(BF16)
    lo = (a - hi.astype(F32)).astype(BF16)
    return _dot(hi, w) + _dot(lo, w)


def _nsa_kernel(q_ref, gate_ref, kc_ref, vc_ref, ks_ref, vs_ref,
                kw0_ref, kw1_ref, kw2_ref, vw0_ref, vw1_ref, vw2_ref,
                dc_ref, ds_ref, dw_ref, perm_ref, eblk_ref, wfar_ref, o_ref,
                s_ref, p_ref, m_ref, l_ref, acc_ref, imp_ref, sn_ref, pn_ref, selt_ref,
                *, n_sel, top_n):
    qt = pl.program_id(2)
    TQ, P = NSA_TQ, NSA_HPG
    R = P * TQ
    blocks_per_tile = NSA_TQ // SEL_LEN
    blocks_per_far = NSA_TK // SEL_LEN
    state = (s_ref, p_ref, m_ref, l_ref, acc_ref)
    qblk = q_ref[0]
    q4 = jnp.concatenate([qblk[:, p * HEAD_DIM:(p + 1) * HEAD_DIM] for p in range(P)], axis=0)

    ncp = kc_ref.shape[3]
    cmp_per_tile = NSA_TQ // CMP_STRIDE
    lane = lax.broadcasted_iota(jnp.int32, (1, ncp), 1)
    far_bias = jnp.where((lane >= CMP_PAD_FRONT) & (lane < cmp_per_tile * qt), 0.0, NEG)
    c0 = pl.multiple_of(cmp_per_tile * qt, cmp_per_tile)
    kn = kc_ref[0, 0, 0, pl.ds(c0, CMP_NEAR), :]
    vn = vc_ref[0, 0, 0, pl.ds(c0, CMP_NEAR), :]
    s_ref[0, :, :ncp] = _dot_nt(q4, kc_ref[0, 0, 0])
    sn_ref[...] = _dot_nt(q4, kn) + dc_ref[0, 0]
    imp_ref[...] = jnp.zeros(imp_ref.shape, F32)

    def cmp_head(p, carry):
        for c in range(TQ // NSA_ROWS):
            rows = pl.ds(pl.multiple_of(p * TQ + c * NSA_ROWS, NSA_ROWS), NSA_ROWS)
            irows = slice(c * NSA_ROWS, (c + 1) * NSA_ROWS)
            sf = s_ref[0, rows, :ncp] + far_bias
            sn = sn_ref[rows, :]
            m = jnp.maximum(jnp.max(sf, axis=-1, keepdims=True),
                            jnp.max(sn, axis=-1, keepdims=True))
            ef = jnp.exp2(sf - m)
            en = jnp.exp2(sn - m)
            l = jnp.sum(ef, axis=-1, keepdims=True) + jnp.sum(en, axis=-1, keepdims=True)
            inv = jnp.where(m > 0.5 * NEG, 1.0 / l, 0.0)
            pf = ef * inv
            p_ref[rows, :ncp] = pf.astype(BF16)
            pn_ref[rows, :] = en * inv
            imp_ref[irows, :] += pf
        return carry

    lax.fori_loop(0, P, cmp_head, 0)
    p_near = pn_ref[...]
    o_c = _dot(p_ref[:, :ncp], vc_ref[0, 0, 0]) + _dot(p_near.astype(BF16), vn)

    imp_near = jnp.sum(p_near.reshape(P, TQ, CMP_NEAR), axis=0)
    jn = lax.broadcasted_iota(jnp.int32, (CMP_NEAR, n_sel), 0)
    jj = lax.broadcasted_iota(jnp.int32, (CMP_NEAR, n_sel), 1)
    d = (cmp_per_tile * qt - CMP_PAD_FRONT + jn) - (SEL_LEN // CMP_STRIDE) * jj
    w_near = jnp.where((d == 0) | (d == 4), 1.0,
                       jnp.where((d >= 1) & (d <= 3), 2.0, 0.0)).astype(BF16)
    p_slc = _split_dot(imp_ref[...], wfar_ref[...]) + _split_dot(imp_near, w_near)

    ti = lax.broadcasted_iota(jnp.int32, (TQ, n_sel), 0)
    bj = lax.broadcasted_iota(jnp.int32, (TQ, n_sel), 1)
    bjf = bj.astype(F32)
    jt = blocks_per_tile * qt + ti // SEL_LEN
    forced = (bj == 0) | (bj == jt) | (bj == jt - 1)
    score = jnp.where(forced, BIG, jnp.where(bj > jt, NEG, p_slc))

    def pick(_, carry):
        sc, sel = carry
        mx = jnp.max(sc, axis=-1, keepdims=True)
        first = jnp.min(jnp.where(sc == mx, bjf, float(n_sel)), axis=-1, keepdims=True)
        hit = bjf == first
        return jnp.where(hit, -jnp.inf, sc), jnp.where(hit, 0.0, sel)

    _, selneg = lax.fori_loop(0, top_n, pick, (score, jnp.full((TQ, n_sel), NEG, F32)))

    nb0 = jnp.maximum(blocks_per_tile * (qt - 1), 0)
    pj = lax.broadcasted_iota(jnp.int32, (n_sel, LANES), 0)
    pu = lax.broadcasted_iota(jnp.int32, (n_sel, LANES), 1)
    perm_near = ((pj == nb0 + pu) & (pu < 2 * blocks_per_tile)).astype(BF16)
    sel_near = _dot(selneg.astype(BF16), perm_near).astype(BF16)
    sel_far = jnp.where(bj < blocks_per_tile * (qt - 1), selneg, NEG).astype(BF16)
    sel_all = _dot(sel_far, perm_ref[...]).astype(BF16)
    n_kt = selt_ref.shape[0] - 1
    for kt in range(n_kt):
        selt_ref[kt] = sel_all[:, kt * LANES:(kt + 1) * LANES]
    col = lax.broadcasted_iota(jnp.int32, (TQ, LANES), 1)
    selt_ref[n_kt] = jnp.where(col < blocks_per_far, NEG, 0.0).astype(BF16)

    _init_state(m_ref, l_ref, acc_ref)
    near_keys = 2 * NSA_TQ
    ns = pl.multiple_of(nb0 * SEL_LEN, NSA_TQ)
    lhs = jnp.concatenate([q4, jnp.concatenate([sel_near] * P, axis=0)], axis=1)
    rhs = jnp.concatenate([ks_ref[0, pl.ds(ns, near_keys), :], eblk_ref[:near_keys, :]], axis=1)
    s_ref[0, :, :near_keys] = _dot_nt(lhs, rhs) + ds_ref[0, 0].astype(F32)

    def far_scores(kt, sel_idx, slot):
        k0 = pl.multiple_of(kt * NSA_TK, NSA_TK)
        lhs_t = jnp.concatenate([q4, jnp.concatenate([selt_ref[sel_idx]] * P, axis=0)], axis=1)
        rhs_t = jnp.concatenate([ks_ref[0, pl.ds(k0, NSA_TK), :], eblk_ref[...]], axis=1)
        s_ref[slot, :, :NSA_TK] = _dot_nt(lhs_t, rhs_t)

    def far_fold(kt, slot):
        k0 = pl.multiple_of(kt * NSA_TK, NSA_TK)
        _fold_tile(slot, NSA_TK, vs_ref[0, pl.ds(k0, NSA_TK), :], *state, NSA_ROWS)

    n_far = (qt + 2) // 4
    odd = jnp.maximum(n_far - 1, 0)
    odd_sel = jnp.where(n_far % 2 == 1, odd, selt_ref.shape[0] - 1)
    last_pair_tile = jnp.maximum(n_far - 2, 0)
    far_scores(odd, odd_sel, 1)
    _fold_tile(0, near_keys, vs_ref[0, pl.ds(ns, near_keys), :], *state, NSA_ROWS)
    far_scores(0, 0, 0)
    far_fold(odd, 1)

    def far_pair(i, carry):
        a = 2 * i
        far_scores(a + 1, a + 1, 1)
        far_fold(a, 0)
        nxt = jnp.minimum(a + 2, last_pair_tile)
        far_scores(nxt, nxt, 0)
        far_fold(a + 1, 1)
        return carry

    lax.fori_loop(0, n_far // 2, far_pair, 0)
    o_s = _finish(l_ref, acc_ref)

    _init_state(m_ref, l_ref, acc_ref)
    kw = jnp.concatenate([kw0_ref[0], kw1_ref[0], kw2_ref[0]], axis=0)
    vw = jnp.concatenate([vw0_ref[0], vw1_ref[0], vw2_ref[0]], axis=0)
    s_ref[0, :, :WINDOW + NSA_TQ] = _dot_nt(q4, kw) + dw_ref[0, 0].astype(F32)
    _fold_tile(0, WINDOW + NSA_TQ, vw, *state, NSA_ROWS)
    o_w = _finish(l_ref, acc_ref)

    gate = gate_ref[0]
    outs = []
    for p in range(P):
        rows = slice(p * TQ, (p + 1) * TQ)
        gc = gate[:, 3 * p:3 * p + 1]
        gs = gate[:, 3 * p + 1:3 * p + 2]
        gw = gate[:, 3 * p + 2:3 * p + 3]
        outs.append(gc * o_c[rows] + gs * o_s[rows] + gw * o_w[rows])
    o_ref[0] = jnp.concatenate(outs, axis=1).astype(o_ref.dtype)


def _nsa(proj3, gates3, cmp_pad, dc, ds, dw, perm, eblk, wfar, n_sel):
    B, S, _ = proj3.shape
    G = NSA_GROUPS
    ncp = cmp_pad.shape[3]
    top_n = min(SEL_TOPK, n_sel)
    qw = NSA_HPG * HEAD_DIM
    R = NSA_HPG * NSA_TQ
    wide = max(NSA_TK, ncp, WINDOW + NSA_TQ)
    wblk = lambda col, back: pl.BlockSpec(
        (1, NSA_TQ, HEAD_DIM),
        lambda b, g, t, col=col, back=back: (b, jnp.maximum(t - back, 0), col // HEAD_DIM + g))
    kernel = functools.partial(_nsa_kernel, n_sel=n_sel, top_n=top_n)
    return pl.pallas_call(
        kernel,
        grid=(B, G, S // NSA_TQ),
        in_specs=[
            pl.BlockSpec((1, NSA_TQ, qw), lambda b, g, t: (b, t, g)),
            pl.BlockSpec((1, NSA_TQ, LANES), lambda b, g, t: (b, t, g)),
            pl.BlockSpec((1, 1, 1, ncp, HEAD_DIM), lambda b, g, t: (0, b, g, 0, 0)),
            pl.BlockSpec((1, 1, 1, ncp, HEAD_DIM), lambda b, g, t: (1, b, g, 0, 0)),
            pl.BlockSpec((1, S, HEAD_DIM), lambda b, g, t: (b, 0, COL_KS // HEAD_DIM + g)),
            pl.BlockSpec((1, S, HEAD_DIM), lambda b, g, t: (b, 0, COL_VS // HEAD_DIM + g)),
            wblk(COL_KW, 2), wblk(COL_KW, 1), wblk(COL_KW, 0),
            wblk(COL_VW, 2), wblk(COL_VW, 1), wblk(COL_VW, 0),
            pl.BlockSpec((1, 1) + dc.shape[2:], lambda b, g, t: (jnp.minimum(t, 1), g, 0, 0)),
            pl.BlockSpec((1, 1) + ds.shape[2:], lambda b, g, t: (jnp.minimum(t, 1), g, 0, 0)),
            pl.BlockSpec((1, 1) + dw.shape[2:], lambda b, g, t: (jnp.minimum(t, 2), g, 0, 0)),
            pl.BlockSpec(perm.shape, lambda b, g, t: (0, 0)),
            pl.BlockSpec(eblk.shape, lambda b, g, t: (0, 0)),
            pl.BlockSpec(wfar.shape, lambda b, g, t: (0, 0)),
        ],
        out_specs=pl.BlockSpec((1, NSA_TQ, qw), lambda b, g, t: (b, t, g)),
        out_shape=jax.ShapeDtypeStruct((B, S, NSA_HEADS * HEAD_DIM), BF16),
        scratch_shapes=[
            pltpu.VMEM((2, R, wide), F32),
            pltpu.VMEM((R, wide), BF16),
            pltpu.VMEM((R, LANES), F32),
            pltpu.VMEM((R, LANES), F32),
            pltpu.VMEM((R, HEAD_DIM), F32),
            pltpu.VMEM((NSA_TQ, ncp), F32),
            pltpu.VMEM((R, CMP_NEAR), F32),
            pltpu.VMEM((R, CMP_NEAR), F32),
            pltpu.VMEM((S // NSA_TK + 1, NSA_TQ, LANES), BF16),
        ],
        compiler_params=_params("parallel", "parallel", "arbitrary"),
    )(proj3, gates3, cmp_pad, cmp_pad, proj3, proj3,
      proj3, proj3, proj3, proj3, proj3, proj3, dc, ds, dw, perm, eblk, wfar)


def _diff_kernel(q_ref, k_ref, v_ref, dd_ref, lq1_ref, lk1_ref, lq2_ref, lk2_ref, hg_ref,
                 o_ref, s_ref, p_ref, m_ref, l_ref, acc_ref, *, lam_init):
    qt = pl.program_id(2)
    TQ = DIFF_TQ
    state = (s_ref, p_ref, m_ref, l_ref, acc_ref)
    q = q_ref[0]
    zero = jnp.zeros((TQ, HEAD_DIM), BF16)
    lhs = jnp.concatenate([
        jnp.concatenate([q[:, :HEAD_DIM], zero], axis=1),
        jnp.concatenate([zero, q[:, HEAD_DIM:]], axis=1)], axis=0)
    lam = (jnp.exp(jnp.sum(lq1_ref[...] * lk1_ref[...], axis=-1, keepdims=True))
           - jnp.exp(jnp.sum(lq2_ref[...] * lk2_ref[...], axis=-1, keepdims=True)) + lam_init)

    def scores(kt, slot, bias):
        k0 = pl.multiple_of(kt * DIFF_TK, DIFF_TK)
        sc = _dot_nt(lhs, k_ref[0, pl.ds(k0, DIFF_TK), :])
        if bias is not None:
            sc = (sc.reshape(2, TQ, DIFF_TK) + bias[None]).reshape(2 * TQ, DIFF_TK)
        s_ref[slot] = sc

    def fold(kt, slot):
        k0 = pl.multiple_of(kt * DIFF_TK, DIFF_TK)
        _fold_tile(slot, DIFF_TK, v_ref[0, pl.ds(k0, DIFF_TK), :], *state, DIFF_ROWS)

    _init_state(m_ref, l_ref, acc_ref)
    n_far = jnp.maximum(qt - 1, 0)
    prev = jnp.maximum(qt - 1, 0)
    odd = jnp.maximum(n_far - 1, 0)
    last_pair_tile = jnp.maximum(n_far - 2, 0)
    scores(qt, 0, dd_ref[0, :, DIFF_TK:].astype(F32))
    fold(qt, 0)
    scores(prev, 0, jnp.where(qt >= 1, dd_ref[0, :, :DIFF_TK].astype(F32), NEG))
    fold(prev, 0)
    scores(odd, 0, jnp.where(n_far % 2 == 1, jnp.zeros((TQ, DIFF_TK), F32), NEG))
    scores(0, 1, None)
    fold(odd, 0)

    def far_pair(i, carry):
        a = 2 * i
        scores(a + 1, 0, None)
        fold(a, 1)
        scores(jnp.minimum(a + 2, last_pair_tile), 1, None)
        fold(a + 1, 0)
        return carry

    lax.fori_loop(0, n_far // 2, far_pair, 0)
    a = _finish(l_ref, acc_ref)
    o = a[:TQ] - lam * a[TQ:]
    y = _norm_rows(o, hg_ref[0]) * (1.0 - lam_init)
    o_ref[0] = y.astype(o_ref.dtype)


def _diff(proj3, dd, lq1, lk1, lq2, lk2, hg, lam_init):
    B, S, _ = proj3.shape
    H = DIFF_HEADS
    w = 2 * HEAD_DIM
    R = 2 * DIFF_TQ
    vec = pl.BlockSpec((1, HEAD_DIM), lambda b, h, t: (0, 0))
    kernel = functools.partial(_diff_kernel, lam_init=lam_init)
    return pl.pallas_call(
        kernel,
        grid=(B, H, S // DIFF_TQ),
        in_specs=[
            pl.BlockSpec((1, DIFF_TQ, w), lambda b, h, t: (b, t, COL_QD // w + h)),
            pl.BlockSpec((1, S, w), lambda b, h, t: (b, 0, COL_KD // w + h)),
            pl.BlockSpec((1, S, w), lambda b, h, t: (b, 0, COL_VD // w + h)),
            pl.BlockSpec((1,) + dd.shape[1:], lambda b, h, t: (h, 0, 0)),
            vec, vec, vec, vec,
            pl.BlockSpec((1, 1, w), lambda b, h, t: (h, 0, 0)),
        ],
        out_specs=pl.BlockSpec((1, DIFF_TQ, w), lambda b, h, t: (b, t, h)),
        out_shape=jax.ShapeDtypeStruct((B, S, H * w), BF16),
        scratch_shapes=[
            pltpu.VMEM((2, R, DIFF_TK), F32),
            pltpu.VMEM((R, DIFF_TK), BF16),
            pltpu.VMEM((R, LANES), F32),
            pltpu.VMEM((R, LANES), F32),
            pltpu.VMEM((R, DIFF_V_DIM), F32),
        ],
        compiler_params=_params("parallel", "parallel", "arbitrary"),
    )(proj3, proj3, proj3, dd, lq1, lk1, lq2, lk2, hg)


def _merge_kernel(on_ref, od_ref, wn_ref, wd_ref, ga_ref, gb_ref, o_ref):
    u1 = _dot(on_ref[...], wn_ref[...])
    u2 = _dot(od_ref[...], wd_ref[...])
    ga = jax.nn.sigmoid(ga_ref[...].astype(F32))
    gb = jax.nn.sigmoid(gb_ref[...].astype(F32))
    o_ref[...] = (ga * u1 + gb * u2).astype(o_ref.dtype)


def _merge(o_nsa, o_d, wn, wd, proj, tm=1024, tn=512):
    T, kn = o_nsa.shape
    kd = o_d.shape[1]
    return pl.pallas_call(
        _merge_kernel,
        grid=(T // tm, D_MODEL // tn),
        in_specs=[
            pl.BlockSpec((tm, kn), lambda i, j: (i, 0)),
            pl.BlockSpec((tm, kd), lambda i, j: (i, 0)),
            pl.BlockSpec((kn, tn), lambda i, j: (0, j)),
            pl.BlockSpec((kd, tn), lambda i, j: (0, j)),
            pl.BlockSpec((tm, tn), lambda i, j: (i, COL_GM // tn + j)),
            pl.BlockSpec((tm, tn), lambda i, j: (i, (COL_GM + D_MODEL) // tn + j)),
        ],
        out_specs=pl.BlockSpec((tm, tn), lambda i, j: (i, j)),
        out_shape=jax.ShapeDtypeStruct((T, D_MODEL), BF16),
        compiler_params=_params("parallel", "arbitrary"),
    )(o_nsa, o_d, wn, wd, proj, proj)


def _out_proj_kernel(a_ref, w_ref, x_ref, o_ref):
    o_ref[...] = x_ref[...] + _dot(a_ref[...], w_ref[...])


def _out_proj(a, w, x2, tm=1024, tn=512):
    T, k = a.shape
    return pl.pallas_call(
        _out_proj_kernel,
        grid=(T // tm, D_MODEL // tn),
        in_specs=[
            pl.BlockSpec((tm, k), lambda i, j: (i, 0)),
            pl.BlockSpec((k, tn), lambda i, j: (0, j)),
            pl.BlockSpec((tm, tn), lambda i, j: (i, j)),
        ],
        out_specs=pl.BlockSpec((tm, tn), lambda i, j: (i, j)),
        out_shape=jax.ShapeDtypeStruct((T, D_MODEL), F32),
        compiler_params=_params("parallel", "arbitrary"),
    )(a, w, x2)


def _ffn_in_kernel(x_ref, g_ref, wa_ref, wb_ref, o_ref, h_ref):
    @pl.when(pl.program_id(1) == 0)
    def _():
        h_ref[...] = _norm_rows(x_ref[...], g_ref[...]).astype(BF16)

    h = h_ref[...]
    a = _dot(h, wa_ref[...])
    b = _dot(h, wb_ref[...])
    o_ref[...] = (a * jax.nn.sigmoid(a) * b).astype(o_ref.dtype)


def _ffn_in(x2, g, w, tm=1024, tn=512):
    T = x2.shape[0]
    nb = D_FF // tn
    return pl.pallas_call(
        _ffn_in_kernel,
        grid=(T // tm, nb),
        in_specs=[
            pl.BlockSpec((tm, D_MODEL), lambda i, j: (i, 0)),
            pl.BlockSpec((1, D_MODEL), lambda i, j: (0, 0)),
            pl.BlockSpec((D_MODEL, tn), lambda i, j: (0, j)),
            pl.BlockSpec((D_MODEL, tn), lambda i, j: (0, j + nb)),
        ],
        out_specs=pl.BlockSpec((tm, tn), lambda i, j: (i, j)),
        out_shape=jax.ShapeDtypeStruct((T, D_FF), BF16),
        scratch_shapes=[pltpu.VMEM((tm, D_MODEL), BF16)],
        compiler_params=_params("parallel", "arbitrary"),
    )(x2, g, w, w)


def _ffn_out_kernel(a_ref, w_ref, x_ref, g_ref, o_ref, acc_ref, *, final_norm):
    k = pl.program_id(1)

    @pl.when(k == 0)
    def _():
        acc_ref[...] = x_ref[...]

    acc_ref[...] += _dot(a_ref[...], w_ref[...])

    @pl.when(k == pl.num_programs(1) - 1)
    def _():
        y = acc_ref[...]
        o_ref[...] = _norm_rows(y, g_ref[...]) if final_norm else y


def _ffn_out(a, w, x2, g, final_norm, tm=512, tk=512):
    T = a.shape[0]
    kernel = functools.partial(_ffn_out_kernel, final_norm=final_norm)
    return pl.pallas_call(
        kernel,
        grid=(T // tm, D_FF // tk),
        in_specs=[
            pl.BlockSpec((tm, tk), lambda i, k: (i, k)),
            pl.BlockSpec((tk, D_MODEL), lambda i, k: (k, 0)),
            pl.BlockSpec((tm, D_MODEL), lambda i, k: (i, 0)),
            pl.BlockSpec((1, D_MODEL), lambda i, k: (0, 0)),
        ],
        out_specs=pl.BlockSpec((tm, D_MODEL), lambda i, k: (i, 0)),
        out_shape=jax.ShapeDtypeStruct((T, D_MODEL), F32),
        scratch_shapes=[pltpu.VMEM((tm, D_MODEL), F32)],
        compiler_params=_params("parallel", "arbitrary"),
    )(a, w, x2, g)


def _t5_bucket(rel):
    n = jnp.maximum(rel, 0)
    nf = jnp.maximum(n, 1).astype(F32)
    large = REL_MAX_EXACT + (jnp.log(nf / REL_MAX_EXACT) / math.log(REL_MAX_DIST / REL_MAX_EXACT)
                             * (REL_BUCKETS - REL_MAX_EXACT)).astype(jnp.int32)
    large = jnp.minimum(large, REL_BUCKETS - 1)
    return jnp.where(n < REL_MAX_EXACT, n, large)


def _bias_of_rel(table, rel, valid):
    shifted = (table - table[REL_BUCKETS - 1:REL_BUCKETS]) * LOG2E
    vals = jnp.moveaxis(shifted[_t5_bucket(rel)], -1, 0)
    return jnp.where(valid[None], vals, NEG)


def _toeplitz_bias(table, n_i, n_j, offset, max_rel=None):
    rel = jnp.arange(n_i + n_j - 1, dtype=jnp.int32) - (n_j - 1) + offset
    valid = rel >= 0 if max_rel is None else (rel >= 0) & (rel < max_rel)
    v = _bias_of_rel(table, rel, valid)
    length = v.shape[-1]
    hankel = jnp.tile(v, (1, n_i + 1))[:, :n_i * (length + 1)].reshape(-1, n_i, length + 1)
    return hankel[:, :, :n_j][:, :, ::-1]


def _nsa_tables(table_nsa):
    TQ = NSA_TQ

    def stack(t):
        return t.reshape(NSA_GROUPS, NSA_HPG * TQ, t.shape[-1])

    i = jnp.arange(TQ, dtype=jnp.int32)[:, None]
    jc = jnp.arange(CMP_NEAR, dtype=jnp.int32)[None, :]
    rel_c = i - CMP_STRIDE * (jc - CMP_PAD_FRONT) - (CMP_LEN - 1)
    dc = jnp.stack([
        stack(_bias_of_rel(table_nsa, rel_c, (rel_c >= 0) & (jc >= CMP_PAD_FRONT))),
        stack(_bias_of_rel(table_nsa, rel_c, rel_c >= 0)),
    ])
    ds = jnp.stack([
        stack(_toeplitz_bias(table_nsa, TQ, 2 * TQ, 0)),
        stack(_toeplitz_bias(table_nsa, TQ, 2 * TQ, TQ)),
    ]).astype(BF16)
    win = stack(_toeplitz_bias(table_nsa, TQ, WINDOW + TQ, WINDOW, WINDOW))
    jw = jnp.arange(WINDOW + TQ, dtype=jnp.int32)[None, None, :]
    dw = jnp.stack([jnp.where(jw >= WINDOW - c * TQ, win, NEG) for c in range(3)]).astype(BF16)
    return dc, ds, dw


def _diff_table(table_diff):
    return _toeplitz_bias(table_diff, DIFF_TQ, 2 * DIFF_TK, DIFF_TK).astype(BF16)


def _selection_constants(S):
    n_sel = S // SEL_LEN
    n_cmp = S // CMP_STRIDE
    n_kt = S // NSA_TK
    blocks = NSA_TK // SEL_LEN
    j = jnp.arange(n_sel, dtype=jnp.int32)[:, None]
    col = jnp.arange(n_kt * LANES, dtype=jnp.int32)[None, :]
    perm = ((j == blocks * (col // LANES) + col % LANES) & (col % LANES < blocks)).astype(BF16)
    key = jnp.arange(NSA_TK, dtype=jnp.int32)[:, None]
    u = jnp.arange(LANES, dtype=jnp.int32)[None, :]
    eblk = (u == key // SEL_LEN).astype(BF16)
    c = jnp.arange(n_cmp + CMP_PAD_FRONT + CMP_PAD_BACK, dtype=jnp.int32)[:, None] - CMP_PAD_FRONT
    jb = jnp.arange(n_sel, dtype=jnp.int32)[None, :]
    d = c - (SEL_LEN // CMP_STRIDE) * jb
    wfar = jnp.where((d == 0) | (d == 4), 1.0, jnp.where((d >= 1) & (d <= 3), 2.0, 0.0))
    wfar = jnp.where((c >= 0) & (c < n_cmp), wfar, 0.0).astype(BF16)
    return perm, eblk, wfar


def kernel(x, norm_mix_g, w_in, cmp_pe_k, cmp_pe_v, cmp_w1_k, cmp_w2_k, cmp_w1_v, cmp_w2_v,
           diff_lq1, diff_lk1, diff_lq2, diff_lk2, diff_head_g, w_up_nsa, w_up_diff, w_out,
           norm_ff_g, w_ff_in, w_ff_out, rel_bias_table, norm_final_g):
    B, S, D = x.shape
    T = B * S
    depth = w_in.shape[0]
    n_sel = S // SEL_LEN
    n_cmp = S // CMP_STRIDE
    qscale = HEAD_DIM ** -0.5 * LOG2E

    dc, ds, dw = _nsa_tables(rel_bias_table[:, :NSA_HEADS])
    dd = _diff_table(rel_bias_table[:, NSA_HEADS:])
    perm, eblk, wfar = _selection_constants(S)

    gn0 = COL_QD
    gn1 = gn0 + NSA_HEADS * 3
    cs = jnp.ones((1, N_PROJ), F32)
    cs = cs.at[:, COL_QN:COL_KC].set(qscale).at[:, COL_QD:COL_KD].set(qscale)

    x2 = x.reshape(T, D)
    for l in range(depth):
        w_main = jnp.concatenate([w_in[l][:, :gn0], w_in[l][:, gn1:]], axis=1).astype(BF16)
        wg = w_in[l][:, gn0:gn1].reshape(D, NSA_GROUPS, NSA_HPG * 3)
        wg = jnp.pad(wg, ((0, 0), (0, 0), (0, LANES - NSA_HPG * 3))).reshape(D, NSA_GROUPS * LANES)
        proj, gates = _in_proj(x2, norm_mix_g[l][None], w_main, cs, wg.astype(BF16))
        proj3 = proj.reshape(B, S, N_PROJ)
        gates3 = gates.reshape(B, S, NSA_GROUPS * LANES)

        halves = proj3[:, :, COL_KC:COL_KS].reshape(B, n_cmp, CMP_STRIDE, 2, NSA_GROUPS, HEAD_DIM)
        halves = halves.transpose(3, 0, 4, 1, 2, 5).reshape(
            2, B, NSA_GROUPS, n_cmp, CMP_STRIDE * HEAD_DIM)
        pe = jnp.stack([cmp_pe_k[l], cmp_pe_v[l]]).reshape(2, 1, CMP_LEN * HEAD_DIM)
        w1 = jnp.stack([cmp_w1_k[l], cmp_w1_v[l]]).astype(BF16)
        w2 = jnp.stack([cmp_w2_k[l], cmp_w2_v[l]]).astype(BF16)
        cmp_kv = _compress(halves, pe, w1, w2)
        cmp_pad = jnp.pad(cmp_kv, ((0, 0), (0, 0), (0, 0), (CMP_PAD_FRONT, CMP_PAD_BACK), (0, 0)))

        o_nsa = _nsa(proj3, gates3, cmp_pad, dc, ds, dw, perm, eblk, wfar, n_sel)

        lam_init = 0.8 - 0.6 * math.exp(-0.3 * l)
        o_d = _diff(proj3, dd, diff_lq1[l][None], diff_lk1[l][None], diff_lq2[l][None],
                    diff_lk2[l][None], diff_head_g[l][:, None, :], lam_init)

        mix = _merge(o_nsa.reshape(T, -1), o_d.reshape(T, -1),
                     w_up_nsa[l].astype(BF16), w_up_diff[l].astype(BF16), proj)
        x2 = _out_proj(mix, w_out[l].astype(BF16), x2)

        act = _ffn_in(x2, norm_ff_g[l][None], w_ff_in[l].astype(BF16))
        last = l == depth - 1
        x2 = _ffn_out(act, w_ff_out[l].astype(BF16), x2,
                      norm_final_g[None] if last else norm_ff_g[l][None], final_norm=last)
    return x2.reshape(B, S, D)
```

```python
import functools
import math

import jax
import jax.numpy as jnp
from jax import lax
from jax.experimental import pallas as pl
from jax.experimental.pallas import tpu as pltpu

D_MODEL = 2048
HEAD_DIM = 128
NSA_HEADS = 8
NSA_GROUPS = 2
NSA_HPG = NSA_HEADS // NSA_GROUPS
CMP_LEN = 32
CMP_STRIDE = 16
SEL_LEN = 64
SEL_TOPK = 16
WINDOW = 512
CMP_HIDDEN = 256
DIFF_HEADS = 4
DIFF_V_DIM = 2 * HEAD_DIM
REL_BUCKETS = 32
REL_MAX_EXACT = 16
REL_MAX_DIST = 128
D_FF = -(-8 * D_MODEL // (3 * 256)) * 256
EPS = 1e-6
NEG = -1e30
BIG = 1e30
LOG2E = math.log2(math.e)

F32 = jnp.float32
BF16 = jnp.bfloat16
LANES = 128

VMEM_LIMIT_BYTES = 56 * 1024 * 1024

COL_QN = 0
COL_KC = 1024
COL_KS = 1536
COL_VS = 1792
COL_KW = 2048
COL_VW = 2304
COL_QD = 2560
COL_KD = 3584
COL_VD = 4608
COL_GM = 5632
N_PROJ = 9728

NSA_TQ = 256
NSA_TK = 1024
NSA_ROWS = 16
CMP_NEAR = 32
CMP_PAD_FRONT = 16
CMP_PAD_BACK = 112
DIFF_TQ = 512
DIFF_TK = 512
DIFF_ROWS = 32


def _dot(a, b):
    return jnp.dot(a, b, preferred_element_type=F32)


def _dot_nt(a, b):
    return lax.dot_general(a, b, (((1,), (1,)), ((), ())), preferred_element_type=F32)


def _params(*sem):
    return pltpu.CompilerParams(dimension_semantics=sem, vmem_limit_bytes=VMEM_LIMIT_BYTES)


def _norm_rows(x, g):
    ms = jnp.mean(x * x, axis=-1, keepdims=True)
    return x * lax.rsqrt(ms + EPS) * g


def _in_proj_kernel(x_ref, g_ref, w_ref, cs_ref, wg_ref, o_ref, og_ref, h_ref):
    @pl.when(pl.program_id(1) == 0)
    def _():
        hb = _norm_rows(x_ref[...], g_ref[...]).astype(BF16)
        h_ref[...] = hb
        og_ref[...] = jax.nn.sigmoid(_dot(hb, wg_ref[...]))

    o_ref[...] = (_dot(h_ref[...], w_ref[...]) * cs_ref[...]).astype(o_ref.dtype)


def _in_proj(x2, g, w, cs, wg, tm=1024, tn=512):
    T = x2.shape[0]
    n = w.shape[1]
    ng = wg.shape[1]
    return pl.pallas_call(
        _in_proj_kernel,
        grid=(T // tm, n // tn),
        in_specs=[
            pl.BlockSpec((tm, D_MODEL), lambda i, j: (i, 0)),
            pl.BlockSpec((1, D_MODEL), lambda i, j: (0, 0)),
            pl.BlockSpec((D_MODEL, tn), lambda i, j: (0, j)),
            pl.BlockSpec((1, tn), lambda i, j: (0, j)),
            pl.BlockSpec((D_MODEL, ng), lambda i, j: (0, 0)),
        ],
        out_specs=[
            pl.BlockSpec((tm, tn), lambda i, j: (i, j)),
            pl.BlockSpec((tm, ng), lambda i, j: (i, 0)),
        ],
        out_shape=[
            jax.ShapeDtypeStruct((T, n), BF16),
            jax.ShapeDtypeStruct((T, ng), F32),
        ],
        scratch_shapes=[pltpu.VMEM((tm, D_MODEL), BF16)],
        compiler_params=_params("parallel", "arbitrary"),
    )(x2, g, w, cs, wg)


def _gelu_tanh(x):
    return 0.5 * x * (1.0 + jnp.tanh(math.sqrt(2.0 / math.pi) * (x + 0.044715 * (x * x * x))))


def _compress_kernel(h_ref, pe_ref, w1_ref, w2_ref, o_ref):
    hv = h_ref[0, 0, 0]
    nc = hv.shape[0]
    half = CMP_STRIDE * HEAD_DIM
    ya = _dot(hv, w1_ref[0, :half, :])
    yb = _dot(hv, w1_ref[0, half:, :])
    yb = pltpu.roll(yb, nc - 1, 0)
    row = lax.broadcasted_iota(jnp.int32, yb.shape, 0)
    yb = jnp.where(row == nc - 1, 0.0, yb)
    pe8 = jnp.broadcast_to(pe_ref[0], (8, 2 * half)).astype(BF16)
    pec = _dot(pe8, w1_ref[0])[0:1]
    hid = _gelu_tanh(ya + yb + pec)
    o_ref[0, 0, 0] = _dot(hid.astype(BF16), w2_ref[0]).astype(o_ref.dtype)


def _compress(halves, pe, w1, w2):
    _, B, G, nc, hw = halves.shape
    return pl.pallas_call(
        _compress_kernel,
        grid=(2, B, G),
        in_specs=[
            pl.BlockSpec((1, 1, 1, nc, hw), lambda s, b, g: (s, b, g, 0, 0)),
            pl.BlockSpec((1, 1, 2 * hw), lambda s, b, g: (s, 0, 0)),
            pl.BlockSpec((1, 2 * hw, CMP_HIDDEN), lambda s, b, g: (s, 0, 0)),
            pl.BlockSpec((1, CMP_HIDDEN, HEAD_DIM), lambda s, b, g: (s, 0, 0)),
        ],
        out_specs=pl.BlockSpec((1, 1, 1, nc, HEAD_DIM), lambda s, b, g: (s, b, g, 0, 0)),
        out_shape=jax.ShapeDtypeStruct((2, B, G, nc, HEAD_DIM), BF16),
        compiler_params=_params("parallel", "parallel", "parallel"),
    )(halves, pe, w1, w2)


def _lanes(x, width):
    return jnp.concatenate([x] * (width // x.shape[1]), axis=1)


def _init_state(m_ref, l_ref, acc_ref):
    m_ref[...] = jnp.full(m_ref.shape, NEG, F32)
    l_ref[...] = jnp.zeros(l_ref.shape, F32)
    acc_ref[...] = jnp.zeros(acc_ref.shape, F32)


def _softmax_tile(slot, width, s_ref, p_ref, a_ref, m_ref, l_ref, chunk):
    n_rows = p_ref.shape[1]
    for r in range(0, n_rows, chunk):
        rows = slice(r, r + chunk)
        m_prev = m_ref[rows, :]
        m_new = jnp.maximum(m_prev, jnp.max(s_ref[slot, rows, :width], axis=-1, keepdims=True))
        a_ref[slot, rows, :] = jnp.exp2(m_prev - m_new)
        m_ref[rows, :] = m_new
    for r in range(0, n_rows, chunk):
        rows = slice(r, r + chunk)
        p = jnp.exp2(s_ref[slot, rows, :width] - _lanes(m_ref[rows, :], width))
        l_ref[rows, :] = (a_ref[slot, rows, :] * l_ref[rows, :]
                          + jnp.sum(p, axis=-1, keepdims=True))
        p_ref[slot, rows, :width] = p.astype(BF16)


def _value_tile(slot, width, v, p_ref, a_ref, acc_ref):
    acc_ref[...] = (acc_ref[...] * _lanes(a_ref[slot], acc_ref.shape[1])
                    + _dot(p_ref[slot, :, :width], v))


def _fold_tile(slot, width, v, s_ref, p_ref, a_ref, m_ref, l_ref, acc_ref, chunk):
    _softmax_tile(slot, width, s_ref, p_ref, a_ref, m_ref, l_ref, chunk)
    _value_tile(slot, width, v, p_ref, a_ref, acc_ref)


def _finish(l_ref, acc_ref):
    return acc_ref[...] / _lanes(l_ref[...], acc_ref.shape[1])


def _split_dot_nt(w, a):
    hi = a.astype(BF16)
    lo = (a - hi.astype(F32)).astype(BF16)
    return _dot_nt(w, hi) + _dot_nt(w, lo)


def _nsa_kernel(q_ref, gate_ref, kc_ref, vc_ref, ks_ref, vs_ref,
                kw0_ref, kw1_ref, kw2_ref, vw0_ref, vw1_ref, vw2_ref,
                dc_ref, ds_ref, dw_ref, perm_ref, eblk_ref, wfar_ref, o_ref,
                s_ref, p_ref, a_ref, m_ref, l_ref, acc_ref, imp_ref, sn_ref, pn_ref, selt_ref,
                sel_ref, q4_ref,
                *, n_sel, top_n):
    qt = pl.program_id(2)
    TQ, P = NSA_TQ, NSA_HPG
    R = P * TQ
    blocks_per_tile = NSA_TQ // SEL_LEN
    blocks_per_far = NSA_TK // SEL_LEN
    stats = (s_ref, p_ref, a_ref, m_ref, l_ref)
    state = stats + (acc_ref,)
    qblk = q_ref[0]
    q4_ref[...] = jnp.concatenate(
        [qblk[:, p * HEAD_DIM:(p + 1) * HEAD_DIM] for p in range(P)], axis=0)

    ncp = kc_ref.shape[3]
    cmp_per_tile = NSA_TQ // CMP_STRIDE
    lane = lax.broadcasted_iota(jnp.int32, (1, ncp), 1)
    far_bias = jnp.where((lane >= CMP_PAD_FRONT) & (lane < cmp_per_tile * qt), 0.0, NEG)
    c0 = pl.multiple_of(cmp_per_tile * qt, cmp_per_tile)
    kn = kc_ref[0, 0, 0, pl.ds(c0, CMP_NEAR), :]
    vn = vc_ref[0, 0, 0, pl.ds(c0, CMP_NEAR), :]
    s_ref[0, :, :ncp] = _dot_nt(q4_ref[...], kc_ref[0, 0, 0])
    sn_ref[...] = _dot_nt(q4_ref[...], kn) + dc_ref[0, 0]
    imp_ref[...] = jnp.zeros(imp_ref.shape, F32)

    def cmp_head(p, carry):
        for c in range(TQ // NSA_ROWS):
            rows = pl.ds(pl.multiple_of(p * TQ + c * NSA_ROWS, NSA_ROWS), NSA_ROWS)
            irows = slice(c * NSA_ROWS, (c + 1) * NSA_ROWS)
            sf = s_ref[0, rows, :ncp] + far_bias
            sn = sn_ref[rows, :]
            m = jnp.maximum(jnp.max(sf, axis=-1, keepdims=True),
                            jnp.max(sn, axis=-1, keepdims=True))
            ef = jnp.exp2(sf - m)
            en = jnp.exp2(sn - m)
            l = jnp.sum(ef, axis=-1, keepdims=True) + jnp.sum(en, axis=-1, keepdims=True)
            inv = jnp.where(m > 0.5 * NEG, 1.0 / l, 0.0)
            pf = ef * inv
            p_ref[0, rows, :ncp] = pf.astype(BF16)
            pn_ref[rows, :] = en * inv
            imp_ref[irows, :] += pf
        return carry

    lax.fori_loop(0, P, cmp_head, 0)
    p_near = pn_ref[...]
    o_c = _dot(p_ref[0, :, :ncp], vc_ref[0, 0, 0]) + _dot(p_near.astype(BF16), vn)

    imp_near = jnp.sum(p_near.reshape(P, TQ, CMP_NEAR), axis=0)
    jj = lax.broadcasted_iota(jnp.int32, (n_sel, CMP_NEAR), 0)
    jn = lax.broadcasted_iota(jnp.int32, (n_sel, CMP_NEAR), 1)
    d = (cmp_per_tile * qt - CMP_PAD_FRONT + jn) - (SEL_LEN // CMP_STRIDE) * jj
    w_near = jnp.where((d == 0) | (d == 4), 1.0,
                       jnp.where((d >= 1) & (d <= 3), 2.0, 0.0)).astype(BF16)
    p_slc = _split_dot_nt(wfar_ref[...], imp_ref[...]) + _split_dot_nt(w_near, imp_near)

    bj = lax.broadcasted_iota(jnp.int32, (n_sel, TQ), 0)
    ti = lax.broadcasted_iota(jnp.int32, (n_sel, TQ), 1)
    bjf = bj.astype(F32)
    jt = blocks_per_tile * qt + ti // SEL_LEN
    forced = (bj == 0) | (bj == jt) | (bj == jt - 1)
    score = jnp.where(forced, -jnp.inf, jnp.where(bj > jt, NEG, p_slc))
    sel0 = jnp.where(forced, 0.0, NEG)

    def pick(_, carry):
        sc, sel = carry
        mx = jnp.max(sc, axis=0, keepdims=True)
        first = jnp.min(jnp.where(sc == mx, bjf, float(n_sel)), axis=0, keepdims=True)
        hit = bjf == first
        return jnp.where(hit, -jnp.inf, sc), jnp.where(hit, 0.0, sel)

    _, selneg_t = lax.fori_loop(0, top_n - 3, pick, (score, sel0))
    sel_ref[...] = selneg_t.T
    selneg = sel_ref[...]
    bj = lax.broadcasted_iota(jnp.int32, (TQ, n_sel), 1)

    nb0 = jnp.maximum(blocks_per_tile * (qt - 1), 0)
    pj = lax.broadcasted_iota(jnp.int32, (n_sel, LANES), 0)
    pu = lax.broadcasted_iota(jnp.int32, (n_sel, LANES), 1)
    perm_near = ((pj == nb0 + pu) & (pu < 2 * blocks_per_tile)).astype(BF16)
    sel_near = _dot(selneg.astype(BF16), perm_near).astype(BF16)
    sel_far = jnp.where(bj < blocks_per_tile * (qt - 1), selneg, NEG).astype(BF16)
    sel_all = _dot(sel_far, perm_ref[...]).astype(BF16)
    n_kt = selt_ref.shape[0] - 1
    for kt in range(n_kt):
        selt_ref[kt] = sel_all[:, kt * LANES:(kt + 1) * LANES]
    col = lax.broadcasted_iota(jnp.int32, (TQ, LANES), 1)
    selt_ref[n_kt] = jnp.where(col < blocks_per_far, NEG, 0.0).astype(BF16)

    _init_state(m_ref, l_ref, acc_ref)
    near_keys = 2 * NSA_TQ
    ns = pl.multiple_of(nb0 * SEL_LEN, NSA_TQ)
    lhs = jnp.concatenate([q4_ref[...], jnp.concatenate([sel_near] * P, axis=0)], axis=1)
    rhs = jnp.concatenate([ks_ref[0, pl.ds(ns, near_keys), :], eblk_ref[:near_keys, :]], axis=1)
    s_ref[0, :, :near_keys] = _dot_nt(lhs, rhs) + ds_ref[0, 0].astype(F32)

    def far_scores(kt, sel_idx, slot):
        k0 = pl.multiple_of(kt * NSA_TK, NSA_TK)
        lhs_t = jnp.concatenate(
            [q4_ref[...], jnp.concatenate([selt_ref[sel_idx]] * P, axis=0)], axis=1)
        rhs_t = jnp.concatenate([ks_ref[0, pl.ds(k0, NSA_TK), :], eblk_ref[...]], axis=1)
        s_ref[slot, :, :NSA_TK] = _dot_nt(lhs_t, rhs_t)

    def far_softmax(slot):
        _softmax_tile(slot, NSA_TK, *stats, NSA_ROWS)

    def far_values(kt, slot):
        k0 = pl.multiple_of(kt * NSA_TK, NSA_TK)
        _value_tile(slot, NSA_TK, vs_ref[0, pl.ds(k0, NSA_TK), :], p_ref, a_ref, acc_ref)

    n_far = (qt + 2) // 4
    n_pairs = n_far // 2
    odd = jnp.maximum(n_far - 1, 0)
    odd_sel = jnp.where(n_far % 2 == 1, odd, selt_ref.shape[0] - 1)
    last_pair_tile = jnp.maximum(n_far - 2, 0)
    far_scores(odd, odd_sel, 1)
    _fold_tile(0, near_keys, vs_ref[0, pl.ds(ns, near_keys), :], *state, NSA_ROWS)
    far_scores(0, 0, 0)
    far_softmax(1)

    def far_pair(i, carry):
        a = 2 * i
        far_values(jnp.where(i == 0, odd, a - 1), 1)
        far_scores(a + 1, a + 1, 1)
        far_softmax(0)
        nxt = jnp.minimum(a + 2, last_pair_tile)
        far_values(a, 0)
        far_scores(nxt, nxt, 0)
        far_softmax(1)
        return carry

    lax.fori_loop(0, n_pairs, far_pair, 0)
    far_values(jnp.where(n_pairs == 0, odd, 2 * n_pairs - 1), 1)
    o_s = _finish(l_ref, acc_ref)

    _init_state(m_ref, l_ref, acc_ref)
    kw = jnp.concatenate([kw0_ref[0], kw1_ref[0], kw2_ref[0]], axis=0)
    vw = jnp.concatenate([vw0_ref[0], vw1_ref[0], vw2_ref[0]], axis=0)
    s_ref[0, :, :WINDOW + NSA_TQ] = _dot_nt(q4_ref[...], kw) + dw_ref[0, 0].astype(F32)
    _fold_tile(0, WINDOW + NSA_TQ, vw, *state, NSA_ROWS)
    o_w = _finish(l_ref, acc_ref)

    gate = gate_ref[0]
    outs = []
    for p in range(P):
        rows = slice(p * TQ, (p + 1) * TQ)
        gc = gate[:, 3 * p:3 * p + 1]
        gs = gate[:, 3 * p + 1:3 * p + 2]
        gw = gate[:, 3 * p + 2:3 * p + 3]
        outs.append(gc * o_c[rows] + gs * o_s[rows] + gw * o_w[rows])
    o_ref[0] = jnp.concatenate(outs, axis=1).astype(o_ref.dtype)


def _nsa(proj3, gates3, cmp_pad, dc, ds, dw, perm, eblk, wfar, n_sel):
    B, S, _ = proj3.shape
    G = NSA_GROUPS
    ncp = cmp_pad.shape[3]
    top_n = min(SEL_TOPK, n_sel)
    assert top_n > 3, "the selection loop assumes the three forced blocks fit in the top-n"
    qw = NSA_HPG * HEAD_DIM
    R = NSA_HPG * NSA_TQ
    wide = max(NSA_TK, ncp, WINDOW + NSA_TQ)
    wblk = lambda col, back: pl.BlockSpec(
        (1, NSA_TQ, HEAD_DIM),
        lambda b, g, t, col=col, back=back: (b, jnp.maximum(t - back, 0), col // HEAD_DIM + g))
    kernel = functools.partial(_nsa_kernel, n_sel=n_sel, top_n=top_n)
    return pl.pallas_call(
        kernel,
        grid=(B, G, S // NSA_TQ),
        in_specs=[
            pl.BlockSpec((1, NSA_TQ, qw), lambda b, g, t: (b, t, g)),
            pl.BlockSpec((1, NSA_TQ, LANES), lambda b, g, t: (b, t, g)),
            pl.BlockSpec((1, 1, 1, ncp, HEAD_DIM), lambda b, g, t: (0, b, g, 0, 0)),
            pl.BlockSpec((1, 1, 1, ncp, HEAD_DIM), lambda b, g, t: (1, b, g, 0, 0)),
            pl.BlockSpec((1, S, HEAD_DIM), lambda b, g, t: (b, 0, COL_KS // HEAD_DIM + g)),
            pl.BlockSpec((1, S, HEAD_DIM), lambda b, g, t: (b, 0, COL_VS // HEAD_DIM + g)),
            wblk(COL_KW, 2), wblk(COL_KW, 1), wblk(COL_KW, 0),
            wblk(COL_VW, 2), wblk(COL_VW, 1), wblk(COL_VW, 0),
            pl.BlockSpec((1, 1) + dc.shape[2:], lambda b, g, t: (jnp.minimum(t, 1), g, 0, 0)),
            pl.BlockSpec((1, 1) + ds.shape[2:], lambda b, g, t: (jnp.minimum(t, 1), g, 0, 0)),
            pl.BlockSpec((1, 1) + dw.shape[2:], lambda b, g, t: (jnp.minimum(t, 2), g, 0, 0)),
            pl.BlockSpec(perm.shape, lambda b, g, t: (0, 0)),
            pl.BlockSpec(eblk.shape, lambda b, g, t: (0, 0)),
            pl.BlockSpec(wfar.shape, lambda b, g, t: (0, 0)),
        ],
        out_specs=pl.BlockSpec((1, NSA_TQ, qw), lambda b, g, t: (b, t, g)),
        out_shape=jax.ShapeDtypeStruct((B, S, NSA_HEADS * HEAD_DIM), BF16),
        scratch_shapes=[
            pltpu.VMEM((2, R, wide), F32),
            pltpu.VMEM((2, R, wide), BF16),
            pltpu.VMEM((2, R, LANES), F32),
            pltpu.VMEM((R, LANES), F32),
            pltpu.VMEM((R, LANES), F32),
            pltpu.VMEM((R, HEAD_DIM), F32),
            pltpu.VMEM((NSA_TQ, ncp), F32),
            pltpu.VMEM((R, CMP_NEAR), F32),
            pltpu.VMEM((R, CMP_NEAR), F32),
            pltpu.VMEM((S // NSA_TK + 1, NSA_TQ, LANES), BF16),
            pltpu.VMEM((NSA_TQ, n_sel), F32),
            pltpu.VMEM((R, HEAD_DIM), BF16),
        ],
        compiler_params=_params("parallel", "parallel", "arbitrary"),
    )(proj3, gates3, cmp_pad, cmp_pad, proj3, proj3,
      proj3, proj3, proj3, proj3, proj3, proj3, dc, ds, dw, perm, eblk, wfar)


def _diff_kernel(q_ref, k_ref, v_ref, dd_ref, lq1_ref, lk1_ref, lq2_ref, lk2_ref, hg_ref,
                 o_ref, s_ref, p_ref, a_ref, m_ref, l_ref, acc_ref, lhs_ref, *, lam_init):
    qt = pl.program_id(2)
    TQ = DIFF_TQ
    stats = (s_ref, p_ref, a_ref, m_ref, l_ref)
    q = q_ref[0]
    zero = jnp.zeros((TQ, HEAD_DIM), BF16)
    lhs_ref[...] = jnp.concatenate([
        jnp.concatenate([q[:, :HEAD_DIM], zero], axis=1),
        jnp.concatenate([zero, q[:, HEAD_DIM:]], axis=1)], axis=0)
    lam = (jnp.exp(jnp.sum(lq1_ref[...] * lk1_ref[...], axis=-1, keepdims=True))
           - jnp.exp(jnp.sum(lq2_ref[...] * lk2_ref[...], axis=-1, keepdims=True)) + lam_init)

    def scores(kt, slot, bias):
        k0 = pl.multiple_of(kt * DIFF_TK, DIFF_TK)
        sc = _dot_nt(lhs_ref[...], k_ref[0, pl.ds(k0, DIFF_TK), :])
        if bias is not None:
            sc = (sc.reshape(2, TQ, DIFF_TK) + bias[None]).reshape(2 * TQ, DIFF_TK)
        s_ref[slot] = sc

    def softmax(slot):
        _softmax_tile(slot, DIFF_TK, *stats, DIFF_ROWS)

    def values(kt, slot):
        k0 = pl.multiple_of(kt * DIFF_TK, DIFF_TK)
        _value_tile(slot, DIFF_TK, v_ref[0, pl.ds(k0, DIFF_TK), :], p_ref, a_ref, acc_ref)

    def fold(kt, slot):
        softmax(slot)
        values(kt, slot)

    _init_state(m_ref, l_ref, acc_ref)
    n_far = jnp.maximum(qt - 1, 0)
    prev = jnp.maximum(qt - 1, 0)
    odd = jnp.maximum(n_far - 1, 0)
    last_pair_tile = jnp.maximum(n_far - 2, 0)
    scores(qt, 0, dd_ref[0, :, DIFF_TK:].astype(F32))
    fold(qt, 0)
    scores(prev, 0, jnp.where(qt >= 1, dd_ref[0, :, :DIFF_TK].astype(F32), NEG))
    fold(prev, 0)
    scores(odd, 1, jnp.where(n_far % 2 == 1, jnp.zeros((TQ, DIFF_TK), F32), NEG))
    scores(0, 0, None)
    softmax(1)

    n_pairs = n_far // 2

    def far_pair(i, carry):
        a = 2 * i
        values(jnp.where(i == 0, odd, a - 1), 1)
        scores(a + 1, 1, None)
        softmax(0)
        values(a, 0)
        scores(jnp.minimum(a + 2, last_pair_tile), 0, None)
        softmax(1)
        return carry

    lax.fori_loop(0, n_pairs, far_pair, 0)
    values(jnp.where(n_pairs == 0, odd, 2 * n_pairs - 1), 1)
    a = _finish(l_ref, acc_ref)
    o = a[:TQ] - lam * a[TQ:]
    y = _norm_rows(o, hg_ref[0]) * (1.0 - lam_init)
    o_ref[0] = y.astype(o_ref.dtype)


def _diff(proj3, dd, lq1, lk1, lq2, lk2, hg, lam_init):
    B, S, _ = proj3.shape
    H = DIFF_HEADS
    w = 2 * HEAD_DIM
    R = 2 * DIFF_TQ
    vec = pl.BlockSpec((1, HEAD_DIM), lambda b, h, t: (0, 0))
    kernel = functools.partial(_diff_kernel, lam_init=lam_init)
    return pl.pallas_call(
        kernel,
        grid=(B, H, S // DIFF_TQ),
        in_specs=[
            pl.BlockSpec((1, DIFF_TQ, w), lambda b, h, t: (b, t, COL_QD // w + h)),
            pl.BlockSpec((1, S, w), lambda b, h, t: (b, 0, COL_KD // w + h)),
            pl.BlockSpec((1, S, w), lambda b, h, t: (b, 0, COL_VD // w + h)),
            pl.BlockSpec((1,) + dd.shape[1:], lambda b, h, t: (h, 0, 0)),
            vec, vec, vec, vec,
            pl.BlockSpec((1, 1, w), lambda b, h, t: (h, 0, 0)),
        ],
        out_specs=pl.BlockSpec((1, DIFF_TQ, w), lambda b, h, t: (b, t, h)),
        out_shape=jax.ShapeDtypeStruct((B, S, H * w), BF16),
        scratch_shapes=[
            pltpu.VMEM((2, R, DIFF_TK), F32),
            pltpu.VMEM((2, R, DIFF_TK), BF16),
            pltpu.VMEM((2, R, LANES), F32),
            pltpu.VMEM((R, LANES), F32),
            pltpu.VMEM((R, LANES), F32),
            pltpu.VMEM((R, DIFF_V_DIM), F32),
            pltpu.VMEM((R, 2 * HEAD_DIM), BF16),
        ],
        compiler_params=_params("parallel", "parallel", "arbitrary"),
    )(proj3, proj3, proj3, dd, lq1, lk1, lq2, lk2, hg)


def _merge_kernel(on_ref, od_ref, wn_ref, wd_ref, ga_ref, gb_ref, o_ref):
    u1 = _dot(on_ref[...], wn_ref[...])
    u2 = _dot(od_ref[...], wd_ref[...])
    ga = jax.nn.sigmoid(ga_ref[...].astype(F32))
    gb = jax.nn.sigmoid(gb_ref[...].astype(F32))
    o_ref[...] = (ga * u1 + gb * u2).astype(o_ref.dtype)


def _merge(o_nsa, o_d, wn, wd, proj, tm=1024, tn=512):
    T, kn = o_nsa.shape
    kd = o_d.shape[1]
    return pl.pallas_call(
        _merge_kernel,
        grid=(T // tm, D_MODEL // tn),
        in_specs=[
            pl.BlockSpec((tm, kn), lambda i, j: (i, 0)),
            pl.BlockSpec((tm, kd), lambda i, j: (i, 0)),
            pl.BlockSpec((kn, tn), lambda i, j: (0, j)),
            pl.BlockSpec((kd, tn), lambda i, j: (0, j)),
            pl.BlockSpec((tm, tn), lambda i, j: (i, COL_GM // tn + j)),
            pl.BlockSpec((tm, tn), lambda i, j: (i, (COL_GM + D_MODEL) // tn + j)),
        ],
        out_specs=pl.BlockSpec((tm, tn), lambda i, j: (i, j)),
        out_shape=jax.ShapeDtypeStruct((T, D_MODEL), BF16),
        compiler_params=_params("parallel", "arbitrary"),
    )(o_nsa, o_d, wn, wd, proj, proj)


def _out_proj_kernel(a_ref, w_ref, x_ref, o_ref):
    o_ref[...] = x_ref[...] + _dot(a_ref[...], w_ref[...])


def _out_proj(a, w, x2, tm=1024, tn=512):
    T, k = a.shape
    return pl.pallas_call(
        _out_proj_kernel,
        grid=(T // tm, D_MODEL // tn),
        in_specs=[
            pl.BlockSpec((tm, k), lambda i, j: (i, 0)),
            pl.BlockSpec((k, tn), lambda i, j: (0, j)),
            pl.BlockSpec((tm, tn), lambda i, j: (i, j)),
        ],
        out_specs=pl.BlockSpec((tm, tn), lambda i, j: (i, j)),
        out_shape=jax.ShapeDtypeStruct((T, D_MODEL), F32),
        compiler_params=_params("parallel", "arbitrary"),
    )(a, w, x2)


def _ffn_in_kernel(x_ref, g_ref, wa_ref, wb_ref, o_ref, h_ref):
    @pl.when(pl.program_id(1) == 0)
    def _():
        h_ref[...] = _norm_rows(x_ref[...], g_ref[...]).astype(BF16)

    h = h_ref[...]
    a = _dot(h, wa_ref[...])
    b = _dot(h, wb_ref[...])
    o_ref[...] = (a * jax.nn.sigmoid(a) * b).astype(o_ref.dtype)


def _ffn_in(x2, g, w, tm=1024, tn=512):
    T = x2.shape[0]
    nb = D_FF // tn
    return pl.pallas_call(
        _ffn_in_kernel,
        grid=(T // tm, nb),
        in_specs=[
            pl.BlockSpec((tm, D_MODEL), lambda i, j: (i, 0)),
            pl.BlockSpec((1, D_MODEL), lambda i, j: (0, 0)),
            pl.BlockSpec((D_MODEL, tn), lambda i, j: (0, j)),
            pl.BlockSpec((D_MODEL, tn), lambda i, j: (0, j + nb)),
        ],
        out_specs=pl.BlockSpec((tm, tn), lambda i, j: (i, j)),
        out_shape=jax.ShapeDtypeStruct((T, D_FF), BF16),
        scratch_shapes=[pltpu.VMEM((tm, D_MODEL), BF16)],
        compiler_params=_params("parallel", "arbitrary"),
    )(x2, g, w, w)


def _ffn_out_kernel(a_ref, w_ref, x_ref, g_ref, o_ref, acc_ref, *, final_norm):
    k = pl.program_id(1)

    @pl.when(k == 0)
    def _():
        acc_ref[...] = x_ref[...]

    acc_ref[...] += _dot(a_ref[...], w_ref[...])

    @pl.when(k == pl.num_programs(1) - 1)
    def _():
        y = acc_ref[...]
        o_ref[...] = _norm_rows(y, g_ref[...]) if final_norm else y


def _ffn_out(a, w, x2, g, final_norm, tm=512, tk=1408):
    T = a.shape[0]
    kernel = functools.partial(_ffn_out_kernel, final_norm=final_norm)
    return pl.pallas_call(
        kernel,
        grid=(T // tm, D_FF // tk),
        in_specs=[
            pl.BlockSpec((tm, tk), lambda i, k: (i, k)),
            pl.BlockSpec((tk, D_MODEL), lambda i, k: (k, 0)),
            pl.BlockSpec((tm, D_MODEL), lambda i, k: (i, 0)),
            pl.BlockSpec((1, D_MODEL), lambda i, k: (0, 0)),
        ],
        out_specs=pl.BlockSpec((tm, D_MODEL), lambda i, k: (i, 0)),
        out_shape=jax.ShapeDtypeStruct((T, D_MODEL), F32),
        scratch_shapes=[pltpu.VMEM((tm, D_MODEL), F32)],
        compiler_params=_params("parallel", "arbitrary"),
    )(a, w, x2, g)


def _t5_bucket(rel):
    n = jnp.maximum(rel, 0)
    nf = jnp.maximum(n, 1).astype(F32)
    large = REL_MAX_EXACT + (jnp.log(nf / REL_MAX_EXACT) / math.log(REL_MAX_DIST / REL_MAX_EXACT)
                             * (REL_BUCKETS - REL_MAX_EXACT)).astype(jnp.int32)
    large = jnp.minimum(large, REL_BUCKETS - 1)
    return jnp.where(n < REL_MAX_EXACT, n, large)


def _bias_of_rel(table, rel, valid):
    shifted = (table - table[REL_BUCKETS - 1:REL_BUCKETS]) * LOG2E
    vals = jnp.moveaxis(shifted[_t5_bucket(rel)], -1, 0)
    return jnp.where(valid[None], vals, NEG)


def _toeplitz_bias(table, n_i, n_j, offset, max_rel=None):
    rel = jnp.arange(n_i + n_j - 1, dtype=jnp.int32) - (n_j - 1) + offset
    valid = rel >= 0 if max_rel is None else (rel >= 0) & (rel < max_rel)
    v = _bias_of_rel(table, rel, valid)
    length = v.shape[-1]
    hankel = jnp.tile(v, (1, n_i + 1))[:, :n_i * (length + 1)].reshape(-1, n_i, length + 1)
    return hankel[:, :, :n_j][:, :, ::-1]


def _nsa_tables(table_nsa):
    TQ = NSA_TQ

    def stack(t):
        return t.reshape(NSA_GROUPS, NSA_HPG * TQ, t.shape[-1])

    i = jnp.arange(TQ, dtype=jnp.int32)[:, None]
    jc = jnp.arange(CMP_NEAR, dtype=jnp.int32)[None, :]
    rel_c = i - CMP_STRIDE * (jc - CMP_PAD_FRONT) - (CMP_LEN - 1)
    dc = jnp.stack([
        stack(_bias_of_rel(table_nsa, rel_c, (rel_c >= 0) & (jc >= CMP_PAD_FRONT))),
        stack(_bias_of_rel(table_nsa, rel_c, rel_c >= 0)),
    ])
    ds = jnp.stack([
        stack(_toeplitz_bias(table_nsa, TQ, 2 * TQ, 0)),
        stack(_toeplitz_bias(table_nsa, TQ, 2 * TQ, TQ)),
    ]).astype(BF16)
    win = stack(_toeplitz_bias(table_nsa, TQ, WINDOW + TQ, WINDOW, WINDOW))
    jw = jnp.arange(WINDOW + TQ, dtype=jnp.int32)[None, None, :]
    dw = jnp.stack([jnp.where(jw >= WINDOW - c * TQ, win, NEG) for c in range(3)]).astype(BF16)
    return dc, ds, dw


def _diff_table(table_diff):
    return _toeplitz_bias(table_diff, DIFF_TQ, 2 * DIFF_TK, DIFF_TK).astype(BF16)


def _selection_constants(S):
    n_sel = S // SEL_LEN
    n_cmp = S // CMP_STRIDE
    n_kt = S // NSA_TK
    blocks = NSA_TK // SEL_LEN
    j = jnp.arange(n_sel, dtype=jnp.int32)[:, None]
    col = jnp.arange(n_kt * LANES, dtype=jnp.int32)[None, :]
    perm = ((j == blocks * (col // LANES) + col % LANES) & (col % LANES < blocks)).astype(BF16)
    key = jnp.arange(NSA_TK, dtype=jnp.int32)[:, None]
    u = jnp.arange(LANES, dtype=jnp.int32)[None, :]
    eblk = (u == key // SEL_LEN).astype(BF16)
    c = jnp.arange(n_cmp + CMP_PAD_FRONT + CMP_PAD_BACK, dtype=jnp.int32)[None, :] - CMP_PAD_FRONT
    jb = jnp.arange(n_sel, dtype=jnp.int32)[:, None]
    d = c - (SEL_LEN // CMP_STRIDE) * jb
    wfar = jnp.where((d == 0) | (d == 4), 1.0, jnp.where((d >= 1) & (d <= 3), 2.0, 0.0))
    wfar = jnp.where((c >= 0) & (c < n_cmp), wfar, 0.0).astype(BF16)
    return perm, eblk, wfar


def kernel(x, norm_mix_g, w_in, cmp_pe_k, cmp_pe_v, cmp_w1_k, cmp_w2_k, cmp_w1_v, cmp_w2_v,
           diff_lq1, diff_lk1, diff_lq2, diff_lk2, diff_head_g, w_up_nsa, w_up_diff, w_out,
           norm_ff_g, w_ff_in, w_ff_out, rel_bias_table, norm_final_g):
    B, S, D = x.shape
    T = B * S
    depth = w_in.shape[0]
    n_sel = S // SEL_LEN
    n_cmp = S // CMP_STRIDE
    qscale = HEAD_DIM ** -0.5 * LOG2E

    dc, ds, dw = _nsa_tables(rel_bias_table[:, :NSA_HEADS])
    dd = _diff_table(rel_bias_table[:, NSA_HEADS:])
    perm, eblk, wfar = _selection_constants(S)

    gn0 = COL_QD
    gn1 = gn0 + NSA_HEADS * 3
    cs = jnp.ones((1, N_PROJ), F32)
    cs = cs.at[:, COL_QN:COL_KC].set(qscale).at[:, COL_QD:COL_KD].set(qscale)

    x2 = x.reshape(T, D)
    for l in range(depth):
        w_main = jnp.concatenate([w_in[l][:, :gn0], w_in[l][:, gn1:]], axis=1).astype(BF16)
        wg = w_in[l][:, gn0:gn1].reshape(D, NSA_GROUPS, NSA_HPG * 3)
        wg = jnp.pad(wg, ((0, 0), (0, 0), (0, LANES - NSA_HPG * 3))).reshape(D, NSA_GROUPS * LANES)
        proj, gates = _in_proj(x2, norm_mix_g[l][None], w_main, cs, wg.astype(BF16))
        proj3 = proj.reshape(B, S, N_PROJ)
        gates3 = gates.reshape(B, S, NSA_GROUPS * LANES)

        halves = proj3[:, :, COL_KC:COL_KS].reshape(B, n_cmp, CMP_STRIDE, 2, NSA_GROUPS, HEAD_DIM)
        halves = halves.transpose(3, 0, 4, 1, 2, 5).reshape(
            2, B, NSA_GROUPS, n_cmp, CMP_STRIDE * HEAD_DIM)
        pe = jnp.stack([cmp_pe_k[l], cmp_pe_v[l]]).reshape(2, 1, CMP_LEN * HEAD_DIM)
        w1 = jnp.stack([cmp_w1_k[l], cmp_w1_v[l]]).astype(BF16)
        w2 = jnp.stack([cmp_w2_k[l], cmp_w2_v[l]]).astype(BF16)
        cmp_kv = _compress(halves, pe, w1, w2)
        cmp_pad = jnp.pad(cmp_kv, ((0, 0), (0, 0), (0, 0), (CMP_PAD_FRONT, CMP_PAD_BACK), (0, 0)))

        o_nsa = _nsa(proj3, gates3, cmp_pad, dc, ds, dw, perm, eblk, wfar, n_sel)

        lam_init = 0.8 - 0.6 * math.exp(-0.3 * l)
        o_d = _diff(proj3, dd, diff_lq1[l][None], diff_lk1[l][None], diff_lq2[l][None],
                    diff_lk2[l][None], diff_head_g[l][:, None, :], lam_init)

        mix = _merge(o_nsa.reshape(T, -1), o_d.reshape(T, -1),
                     w_up_nsa[l].astype(BF16), w_up_diff[l].astype(BF16), proj)
        x2 = _out_proj(mix, w_out[l].astype(BF16), x2)

        act = _ffn_in(x2, norm_ff_g[l][None], w_ff_in[l].astype(BF16))
        last = l == depth - 1
        x2 = _ffn_out(act, w_ff_out[l].astype(BF16), x2,
                      norm_final_g[None] if last else norm_ff_g[l][None], final_norm=last)
    return x2.reshape(B, S, D)
```

```python
import functools
import math

import jax
import jax.numpy as jnp
from jax import lax
from jax.experimental import pallas as pl
from jax.experimental.pallas import tpu as pltpu

D_MODEL = 2048
HEAD_DIM = 128
NSA_HEADS = 8
NSA_GROUPS = 2
NSA_HPG = NSA_HEADS // NSA_GROUPS
CMP_LEN = 32
CMP_STRIDE = 16
SEL_LEN = 64
SEL_TOPK = 16
WINDOW = 512
CMP_HIDDEN = 256
DIFF_HEADS = 4
DIFF_V_DIM = 2 * HEAD_DIM
REL_BUCKETS = 32
REL_MAX_EXACT = 16
REL_MAX_DIST = 128
D_FF = -(-8 * D_MODEL // (3 * 256)) * 256
EPS = 1e-6
NEG = -1e30
BIG = 1e30
LOG2E = math.log2(math.e)

F32 = jnp.float32
BF16 = jnp.bfloat16
LANES = 128

VMEM_LIMIT_BYTES = 56 * 1024 * 1024

COL_QN = 0
COL_KC = 1024
COL_KS = 1536
COL_VS = 1792
COL_KW = 2048
COL_VW = 2304
COL_QD = 2560
COL_KD = 3584
COL_VD = 4608
COL_GM = 5632
N_PROJ = 9728

NSA_TQ = 256
NSA_TK = 1024
NSA_ROWS = 16
CMP_NEAR = 32
CMP_PAD_FRONT = 16
CMP_PAD_BACK = 112
DIFF_TQ = 512
DIFF_TK = 512
DIFF_ROWS = 32


def _dot(a, b):
    return jnp.dot(a, b, preferred_element_type=F32)


def _dot_nt(a, b):
    return lax.dot_general(a, b, (((1,), (1,)), ((), ())), preferred_element_type=F32)


def _params(*sem):
    return pltpu.CompilerParams(dimension_semantics=sem, vmem_limit_bytes=VMEM_LIMIT_BYTES)


def _norm_rows(x, g):
    ms = jnp.mean(x * x, axis=-1, keepdims=True)
    return x * lax.rsqrt(ms + EPS) * g


def _in_proj_kernel(x_ref, g_ref, w_ref, cs_ref, wg_ref, o_ref, og_ref, h_ref):
    @pl.when(pl.program_id(1) == 0)
    def _():
        hb = _norm_rows(x_ref[...], g_ref[...]).astype(BF16)
        h_ref[...] = hb
        og_ref[...] = jax.nn.sigmoid(_dot(hb, wg_ref[...]))

    o_ref[...] = (_dot(h_ref[...], w_ref[...]) * cs_ref[...]).astype(o_ref.dtype)


def _in_proj(x2, g, w, cs, wg, tm=512, tn=2432):
    T = x2.shape[0]
    n = w.shape[1]
    ng = wg.shape[1]
    return pl.pallas_call(
        _in_proj_kernel,
        grid=(T // tm, n // tn),
        in_specs=[
            pl.BlockSpec((tm, D_MODEL), lambda i, j: (i, 0)),
            pl.BlockSpec((1, D_MODEL), lambda i, j: (0, 0)),
            pl.BlockSpec((D_MODEL, tn), lambda i, j: (0, j)),
            pl.BlockSpec((1, tn), lambda i, j: (0, j)),
            pl.BlockSpec((D_MODEL, ng), lambda i, j: (0, 0)),
        ],
        out_specs=[
            pl.BlockSpec((tm, tn), lambda i, j: (i, j)),
            pl.BlockSpec((tm, ng), lambda i, j: (i, 0)),
        ],
        out_shape=[
            jax.ShapeDtypeStruct((T, n), BF16),
            jax.ShapeDtypeStruct((T, ng), F32),
        ],
        scratch_shapes=[pltpu.VMEM((tm, D_MODEL), BF16)],
        compiler_params=_params("parallel", "arbitrary"),
    )(x2, g, w, cs, wg)


def _gelu_tanh(x):
    return 0.5 * x * (1.0 + jnp.tanh(math.sqrt(2.0 / math.pi) * (x + 0.044715 * (x * x * x))))


def _compress_kernel(h_ref, pe_ref, w1_ref, w2_ref, o_ref):
    hv = h_ref[0, 0, 0]
    nc = hv.shape[0]
    half = CMP_STRIDE * HEAD_DIM
    ya = _dot(hv, w1_ref[0, :half, :])
    yb = _dot(hv, w1_ref[0, half:, :])
    yb = pltpu.roll(yb, nc - 1, 0)
    row = lax.broadcasted_iota(jnp.int32, yb.shape, 0)
    yb = jnp.where(row == nc - 1, 0.0, yb)
    pe8 = jnp.broadcast_to(pe_ref[0], (8, 2 * half)).astype(BF16)
    pec = _dot(pe8, w1_ref[0])[0:1]
    hid = _gelu_tanh(ya + yb + pec)
    o_ref[0, 0, 0] = _dot(hid.astype(BF16), w2_ref[0]).astype(o_ref.dtype)


def _compress(halves, pe, w1, w2):
    _, B, G, nc, hw = halves.shape
    return pl.pallas_call(
        _compress_kernel,
        grid=(2, B, G),
        in_specs=[
            pl.BlockSpec((1, 1, 1, nc, hw), lambda s, b, g: (s, b, g, 0, 0)),
            pl.BlockSpec((1, 1, 2 * hw), lambda s, b, g: (s, 0, 0)),
            pl.BlockSpec((1, 2 * hw, CMP_HIDDEN), lambda s, b, g: (s, 0, 0)),
            pl.BlockSpec((1, CMP_HIDDEN, HEAD_DIM), lambda s, b, g: (s, 0, 0)),
        ],
        out_specs=pl.BlockSpec((1, 1, 1, nc, HEAD_DIM), lambda s, b, g: (s, b, g, 0, 0)),
        out_shape=jax.ShapeDtypeStruct((2, B, G, nc, HEAD_DIM), BF16),
        compiler_params=_params("parallel", "parallel", "parallel"),
    )(halves, pe, w1, w2)


def _lanes(x, width):
    return jnp.concatenate([x] * (width // x.shape[1]), axis=1)


def _init_state(m_ref, l_ref, acc_ref):
    m_ref[...] = jnp.full(m_ref.shape, NEG, F32)
    l_ref[...] = jnp.zeros(l_ref.shape, F32)
    acc_ref[...] = jnp.zeros(acc_ref.shape, F32)


def _softmax_tile(slot, width, s_ref, p_ref, a_ref, m_ref, l_ref, chunk):
    n_rows = p_ref.shape[1]
    for r in range(0, n_rows, chunk):
        rows = slice(r, r + chunk)
        m_prev = m_ref[rows, :]
        m_new = jnp.maximum(m_prev, jnp.max(s_ref[slot, rows, :width], axis=-1, keepdims=True))
        a_ref[slot, rows, :] = jnp.exp2(m_prev - m_new)
        m_ref[rows, :] = m_new
    for r in range(0, n_rows, chunk):
        rows = slice(r, r + chunk)
        p = jnp.exp2(s_ref[slot, rows, :width] - _lanes(m_ref[rows, :], width))
        l_ref[rows, :] = (a_ref[slot, rows, :] * l_ref[rows, :]
                          + jnp.sum(p, axis=-1, keepdims=True))
        p_ref[slot, rows, :width] = p.astype(BF16)


def _value_tile(slot, width, v, p_ref, a_ref, acc_ref):
    acc_ref[...] = (acc_ref[...] * _lanes(a_ref[slot], acc_ref.shape[1])
                    + _dot(p_ref[slot, :, :width], v))


def _fold_tile(slot, width, v, s_ref, p_ref, a_ref, m_ref, l_ref, acc_ref, chunk):
    _softmax_tile(slot, width, s_ref, p_ref, a_ref, m_ref, l_ref, chunk)
    _value_tile(slot, width, v, p_ref, a_ref, acc_ref)


def _finish(l_ref, acc_ref):
    return acc_ref[...] / _lanes(l_ref[...], acc_ref.shape[1])


def _split_dot_nt(w, a):
    hi = a.astype(BF16)
    lo = (a - hi.astype(F32)).astype(BF16)
    return _dot_nt(w, hi) + _dot_nt(w, lo)


def _nsa_kernel(q_ref, gate_ref, kc_ref, vc_ref, ks_ref, vs_ref,
                kw0_ref, kw1_ref, kw2_ref, vw0_ref, vw1_ref, vw2_ref,
                dc_ref, ds_ref, dw_ref, perm_ref, eblk_ref, wfar_ref, o_ref,
                s_ref, p_ref, a_ref, m_ref, l_ref, acc_ref, imp_ref, sn_ref, pn_ref, selt_ref,
                sel_ref, q4_ref, pick_ref,
                *, n_sel, top_n):
    qt = pl.program_id(2)
    TQ, P = NSA_TQ, NSA_HPG
    R = P * TQ
    blocks_per_tile = NSA_TQ // SEL_LEN
    stats = (s_ref, p_ref, a_ref, m_ref, l_ref)
    state = stats + (acc_ref,)
    qblk = q_ref[0]
    q4_ref[...] = jnp.concatenate(
        [qblk[:, p * HEAD_DIM:(p + 1) * HEAD_DIM] for p in range(P)], axis=0)

    ncp = kc_ref.shape[3]
    cmp_per_tile = NSA_TQ // CMP_STRIDE
    lane = lax.broadcasted_iota(jnp.int32, (1, ncp), 1)
    far_bias = jnp.where((lane >= CMP_PAD_FRONT) & (lane < cmp_per_tile * qt), 0.0, NEG)
    c0 = pl.multiple_of(cmp_per_tile * qt, cmp_per_tile)
    kn = kc_ref[0, 0, 0, pl.ds(c0, CMP_NEAR), :]
    vn = vc_ref[0, 0, 0, pl.ds(c0, CMP_NEAR), :]
    s_ref[0, :, :ncp] = _dot_nt(q4_ref[...], kc_ref[0, 0, 0])
    sn_ref[...] = _dot_nt(q4_ref[...], kn) + dc_ref[0, 0]
    imp_ref[...] = jnp.zeros(imp_ref.shape, F32)

    def cmp_head(p, carry):
        for c in range(TQ // NSA_ROWS):
            rows = pl.ds(pl.multiple_of(p * TQ + c * NSA_ROWS, NSA_ROWS), NSA_ROWS)
            irows = slice(c * NSA_ROWS, (c + 1) * NSA_ROWS)
            sf = s_ref[0, rows, :ncp] + far_bias
            sn = sn_ref[rows, :]
            m = jnp.maximum(jnp.max(sf, axis=-1, keepdims=True),
                            jnp.max(sn, axis=-1, keepdims=True))
            ef = jnp.exp2(sf - m)
            en = jnp.exp2(sn - m)
            l = jnp.sum(ef, axis=-1, keepdims=True) + jnp.sum(en, axis=-1, keepdims=True)
            inv = jnp.where(m > 0.5 * NEG, 1.0 / l, 0.0)
            pf = ef * inv
            p_ref[0, rows, :ncp] = pf.astype(BF16)
            pn_ref[rows, :] = en * inv
            imp_ref[irows, :] += pf
        return carry

    lax.fori_loop(0, P, cmp_head, 0)
    p_near = pn_ref[...]
    o_c = _dot(p_ref[0, :, :ncp], vc_ref[0, 0, 0]) + _dot(p_near.astype(BF16), vn)

    imp_near = jnp.sum(p_near.reshape(P, TQ, CMP_NEAR), axis=0)
    jj = lax.broadcasted_iota(jnp.int32, (n_sel, CMP_NEAR), 0)
    jn = lax.broadcasted_iota(jnp.int32, (n_sel, CMP_NEAR), 1)
    d = (cmp_per_tile * qt - CMP_PAD_FRONT + jn) - (SEL_LEN // CMP_STRIDE) * jj
    w_near = jnp.where((d == 0) | (d == 4), 1.0,
                       jnp.where((d >= 1) & (d <= 3), 2.0, 0.0)).astype(BF16)
    p_slc = _split_dot_nt(wfar_ref[...], imp_ref[...]) + _split_dot_nt(w_near, imp_near)

    bj = lax.broadcasted_iota(jnp.int32, (n_sel, TQ), 0)
    ti = lax.broadcasted_iota(jnp.int32, (n_sel, TQ), 1)
    jt = blocks_per_tile * qt + ti // SEL_LEN
    forced = (bj == 0) | (bj == jt) | (bj == jt - 1)
    pick_ref[...] = jnp.where(forced, -jnp.inf, jnp.where(bj > jt, NEG, p_slc))
    idx = lax.broadcasted_iota(jnp.int32, (n_sel, LANES), 0).astype(F32)

    def pick(_, sc):
        mx = jnp.max(sc, axis=0, keepdims=True)
        first = jnp.min(jnp.where(sc == mx, idx, float(n_sel)), axis=0, keepdims=True)
        return jnp.where(idx == first, -jnp.inf, sc)

    for h in range(0, TQ, LANES):
        pick_ref[:, h:h + LANES] = lax.fori_loop(0, top_n - 3, pick, pick_ref[:, h:h + LANES])
    selneg_t = jnp.where(pick_ref[...] == -jnp.inf, 0.0, NEG)
    sel_ref[...] = selneg_t.T
    selneg = sel_ref[...]
    bj = lax.broadcasted_iota(jnp.int32, (TQ, n_sel), 1)

    nb0 = jnp.maximum(blocks_per_tile * (qt - 1), 0)
    pj = lax.broadcasted_iota(jnp.int32, (n_sel, LANES), 0)
    pu = lax.broadcasted_iota(jnp.int32, (n_sel, LANES), 1)
    perm_near = ((pj == nb0 + pu) & (pu < 2 * blocks_per_tile)).astype(BF16)
    sel_near = _dot(selneg.astype(BF16), perm_near).astype(BF16)
    sel_far = jnp.where(bj < blocks_per_tile * (qt - 1), selneg, NEG).astype(BF16)
    sel_all = _dot(sel_far, perm_ref[...]).astype(BF16)
    n_kt = selt_ref.shape[0] - 1
    for kt in range(n_kt):
        selt_ref[kt] = sel_all[:, kt * LANES:(kt + 1) * LANES]
    col = lax.broadcasted_iota(jnp.int32, (TQ, LANES), 1)
    selt_ref[n_kt] = jnp.where(col < NSA_TK // SEL_LEN, NEG, 0.0).astype(BF16)

    _init_state(m_ref, l_ref, acc_ref)
    near_keys = 2 * NSA_TQ
    ns = pl.multiple_of(nb0 * SEL_LEN, NSA_TQ)
    lhs = jnp.concatenate([q4_ref[...], jnp.concatenate([sel_near] * P, axis=0)], axis=1)
    rhs = jnp.concatenate([ks_ref[0, pl.ds(ns, near_keys), :], eblk_ref[:near_keys, :]], axis=1)
    s_ref[0, :, :near_keys] = _dot_nt(lhs, rhs) + ds_ref[0, 0].astype(F32)

    def far_scores(kt, sel_idx, slot):
        k0 = pl.multiple_of(kt * NSA_TK, NSA_TK)
        lhs_t = jnp.concatenate(
            [q4_ref[...], jnp.concatenate([selt_ref[sel_idx]] * P, axis=0)], axis=1)
        rhs_t = jnp.concatenate([ks_ref[0, pl.ds(k0, NSA_TK), :], eblk_ref[...]], axis=1)
        s_ref[slot, :, :NSA_TK] = _dot_nt(lhs_t, rhs_t)

    def far_softmax(slot):
        _softmax_tile(slot, NSA_TK, *stats, NSA_ROWS)

    def far_values(kt, slot):
        k0 = pl.multiple_of(kt * NSA_TK, NSA_TK)
        _value_tile(slot, NSA_TK, vs_ref[0, pl.ds(k0, NSA_TK), :], p_ref, a_ref, acc_ref)

    n_far = (qt + 2) // 4
    n_pairs = n_far // 2
    odd = jnp.maximum(n_far - 1, 0)
    odd_sel = jnp.where(n_far % 2 == 1, odd, selt_ref.shape[0] - 1)
    last_pair_tile = jnp.maximum(n_far - 2, 0)
    far_scores(odd, odd_sel, 1)
    _fold_tile(0, near_keys, vs_ref[0, pl.ds(ns, near_keys), :], *state, NSA_ROWS)
    far_scores(0, 0, 0)
    far_softmax(1)

    def far_pair(i, carry):
        a = 2 * i
        far_values(jnp.where(i == 0, odd, a - 1), 1)
        far_scores(a + 1, a + 1, 1)
        far_softmax(0)
        nxt = jnp.minimum(a + 2, last_pair_tile)
        far_values(a, 0)
        far_scores(nxt, nxt, 0)
        far_softmax(1)
        return carry

    lax.fori_loop(0, n_pairs, far_pair, 0)
    far_values(jnp.where(n_pairs == 0, odd, 2 * n_pairs - 1), 1)
    o_s = _finish(l_ref, acc_ref)

    _init_state(m_ref, l_ref, acc_ref)
    kw = jnp.concatenate([kw0_ref[0], kw1_ref[0], kw2_ref[0]], axis=0)
    vw = jnp.concatenate([vw0_ref[0], vw1_ref[0], vw2_ref[0]], axis=0)
    s_ref[0, :, :WINDOW + NSA_TQ] = _dot_nt(q4_ref[...], kw) + dw_ref[0, 0].astype(F32)
    _fold_tile(0, WINDOW + NSA_TQ, vw, *state, NSA_ROWS)
    o_w = _finish(l_ref, acc_ref)

    gate = gate_ref[0]
    outs = []
    for p in range(P):
        rows = slice(p * TQ, (p + 1) * TQ)
        gc = gate[:, 3 * p:3 * p + 1]
        gs = gate[:, 3 * p + 1:3 * p + 2]
        gw = gate[:, 3 * p + 2:3 * p + 3]
        outs.append(gc * o_c[rows] + gs * o_s[rows] + gw * o_w[rows])
    o_ref[0] = jnp.concatenate(outs, axis=1).astype(o_ref.dtype)


def _nsa(proj3, gates3, cmp_pad, dc, ds, dw, perm, eblk, wfar, n_sel):
    B, S, _ = proj3.shape
    G = NSA_GROUPS
    ncp = cmp_pad.shape[3]
    top_n = min(SEL_TOPK, n_sel)
    assert top_n > 3, "the selection loop assumes the three forced blocks fit in the top-n"
    qw = NSA_HPG * HEAD_DIM
    R = NSA_HPG * NSA_TQ
    wide = max(NSA_TK, ncp, WINDOW + NSA_TQ)
    wblk = lambda col, back: pl.BlockSpec(
        (1, NSA_TQ, HEAD_DIM),
        lambda b, g, t, col=col, back=back: (b, jnp.maximum(t - back, 0), col // HEAD_DIM + g))
    kernel = functools.partial(_nsa_kernel, n_sel=n_sel, top_n=top_n)
    return pl.pallas_call(
        kernel,
        grid=(B, G, S // NSA_TQ),
        in_specs=[
            pl.BlockSpec((1, NSA_TQ, qw), lambda b, g, t: (b, t, g)),
            pl.BlockSpec((1, NSA_TQ, LANES), lambda b, g, t: (b, t, g)),
            pl.BlockSpec((1, 1, 1, ncp, HEAD_DIM), lambda b, g, t: (0, b, g, 0, 0)),
            pl.BlockSpec((1, 1, 1, ncp, HEAD_DIM), lambda b, g, t: (1, b, g, 0, 0)),
            pl.BlockSpec((1, S, HEAD_DIM), lambda b, g, t: (b, 0, COL_KS // HEAD_DIM + g)),
            pl.BlockSpec((1, S, HEAD_DIM), lambda b, g, t: (b, 0, COL_VS // HEAD_DIM + g)),
            wblk(COL_KW, 2), wblk(COL_KW, 1), wblk(COL_KW, 0),
            wblk(COL_VW, 2), wblk(COL_VW, 1), wblk(COL_VW, 0),
            pl.BlockSpec((1, 1) + dc.shape[2:], lambda b, g, t: (jnp.minimum(t, 1), g, 0, 0)),
            pl.BlockSpec((1, 1) + ds.shape[2:], lambda b, g, t: (jnp.minimum(t, 1), g, 0, 0)),
            pl.BlockSpec((1, 1) + dw.shape[2:], lambda b, g, t: (jnp.minimum(t, 2), g, 0, 0)),
            pl.BlockSpec(perm.shape, lambda b, g, t: (0, 0)),
            pl.BlockSpec(eblk.shape, lambda b, g, t: (0, 0)),
            pl.BlockSpec(wfar.shape, lambda b, g, t: (0, 0)),
        ],
        out_specs=pl.BlockSpec((1, NSA_TQ, qw), lambda b, g, t: (b, t, g)),
        out_shape=jax.ShapeDtypeStruct((B, S, NSA_HEADS * HEAD_DIM), BF16),
        scratch_shapes=[
            pltpu.VMEM((2, R, wide), F32),
            pltpu.VMEM((2, R, wide), BF16),
            pltpu.VMEM((2, R, LANES), F32),
            pltpu.VMEM((R, LANES), F32),
            pltpu.VMEM((R, LANES), F32),
            pltpu.VMEM((R, HEAD_DIM), F32),
            pltpu.VMEM((NSA_TQ, ncp), F32),
            pltpu.VMEM((R, CMP_NEAR), F32),
            pltpu.VMEM((R, CMP_NEAR), F32),
            pltpu.VMEM((S // NSA_TK + 1, NSA_TQ, LANES), BF16),
            pltpu.VMEM((NSA_TQ, n_sel), F32),
            pltpu.VMEM((R, HEAD_DIM), BF16),
            pltpu.VMEM((n_sel, NSA_TQ), F32),
        ],
        compiler_params=_params("parallel", "parallel", "arbitrary"),
    )(proj3, gates3, cmp_pad, cmp_pad, proj3, proj3,
      proj3, proj3, proj3, proj3, proj3, proj3, dc, ds, dw, perm, eblk, wfar)


def _diff_kernel(q_ref, k_ref, v_ref, dd_ref, lq1_ref, lk1_ref, lq2_ref, lk2_ref, hg_ref,
                 o_ref, s_ref, p_ref, a_ref, m_ref, l_ref, acc_ref, lhs_ref, *, lam_init):
    qt = pl.program_id(2)
    TQ = DIFF_TQ
    stats = (s_ref, p_ref, a_ref, m_ref, l_ref)
    q = q_ref[0]
    zero = jnp.zeros((TQ, HEAD_DIM), BF16)
    lhs_ref[...] = jnp.concatenate([
        jnp.concatenate([q[:, :HEAD_DIM], zero], axis=1),
        jnp.concatenate([zero, q[:, HEAD_DIM:]], axis=1)], axis=0)
    lam = (jnp.exp(jnp.sum(lq1_ref[...] * lk1_ref[...], axis=-1, keepdims=True))
           - jnp.exp(jnp.sum(lq2_ref[...] * lk2_ref[...], axis=-1, keepdims=True)) + lam_init)

    def scores(kt, slot, bias):
        k0 = pl.multiple_of(kt * DIFF_TK, DIFF_TK)
        sc = _dot_nt(lhs_ref[...], k_ref[0, pl.ds(k0, DIFF_TK), :])
        if bias is not None:
            sc = (sc.reshape(2, TQ, DIFF_TK) + bias[None]).reshape(2 * TQ, DIFF_TK)
        s_ref[slot] = sc

    def softmax(slot):
        _softmax_tile(slot, DIFF_TK, *stats, DIFF_ROWS)

    def values(kt, slot):
        k0 = pl.multiple_of(kt * DIFF_TK, DIFF_TK)
        _value_tile(slot, DIFF_TK, v_ref[0, pl.ds(k0, DIFF_TK), :], p_ref, a_ref, acc_ref)

    def fold(kt, slot):
        softmax(slot)
        values(kt, slot)

    _init_state(m_ref, l_ref, acc_ref)
    n_far = jnp.maximum(qt - 1, 0)
    n_pairs = n_far // 2
    prev = jnp.maximum(qt - 1, 0)
    odd = jnp.maximum(n_far - 1, 0)
    last_pair_tile = jnp.maximum(n_far - 2, 0)
    scores(qt, 0, dd_ref[0, :, DIFF_TK:].astype(F32))
    fold(qt, 0)
    scores(prev, 0, jnp.where(qt >= 1, dd_ref[0, :, :DIFF_TK].astype(F32), NEG))
    fold(prev, 0)
    scores(odd, 1, jnp.where(n_far % 2 == 1, jnp.zeros((TQ, DIFF_TK), F32), NEG))
    scores(0, 0, None)
    softmax(1)


    def far_pair(i, carry):
        a = 2 * i
        values(jnp.where(i == 0, odd, a - 1), 1)
        scores(a + 1, 1, None)
        softmax(0)
        values(a, 0)
        scores(jnp.minimum(a + 2, last_pair_tile), 0, None)
        softmax(1)
        return carry

    lax.fori_loop(0, n_pairs, far_pair, 0)
    values(jnp.where(n_pairs == 0, odd, 2 * n_pairs - 1), 1)
    a = _finish(l_ref, acc_ref)
    o = a[:TQ] - lam * a[TQ:]
    y = _norm_rows(o, hg_ref[0]) * (1.0 - lam_init)
    o_ref[0] = y.astype(o_ref.dtype)


def _diff(proj3, dd, lq1, lk1, lq2, lk2, hg, lam_init):
    B, S, _ = proj3.shape
    H = DIFF_HEADS
    w = 2 * HEAD_DIM
    R = 2 * DIFF_TQ
    vec = pl.BlockSpec((1, HEAD_DIM), lambda b, h, t: (0, 0))
    kernel = functools.partial(_diff_kernel, lam_init=lam_init)
    return pl.pallas_call(
        kernel,
        grid=(B, H, S // DIFF_TQ),
        in_specs=[
            pl.BlockSpec((1, DIFF_TQ, w), lambda b, h, t: (b, t, COL_QD // w + h)),
            pl.BlockSpec((1, S, w), lambda b, h, t: (b, 0, COL_KD // w + h)),
            pl.BlockSpec((1, S, w), lambda b, h, t: (b, 0, COL_VD // w + h)),
            pl.BlockSpec((1,) + dd.shape[1:], lambda b, h, t: (h, 0, 0)),
            vec, vec, vec, vec,
            pl.BlockSpec((1, 1, w), lambda b, h, t: (h, 0, 0)),
        ],
        out_specs=pl.BlockSpec((1, DIFF_TQ, w), lambda b, h, t: (b, t, h)),
        out_shape=jax.ShapeDtypeStruct((B, S, H * w), BF16),
        scratch_shapes=[
            pltpu.VMEM((2, R, DIFF_TK), F32),
            pltpu.VMEM((2, R, DIFF_TK), BF16),
            pltpu.VMEM((2, R, LANES), F32),
            pltpu.VMEM((R, LANES), F32),
            pltpu.VMEM((R, LANES), F32),
            pltpu.VMEM((R, DIFF_V_DIM), F32),
            pltpu.VMEM((R, 2 * HEAD_DIM), BF16),
        ],
        compiler_params=_params("parallel", "parallel", "arbitrary"),
    )(proj3, proj3, proj3, dd, lq1, lk1, lq2, lk2, hg)


def _merge_kernel(on_ref, od_ref, wn_ref, wd_ref, ga_ref, gb_ref, o_ref):
    u1 = _dot(on_ref[...], wn_ref[...])
    u2 = _dot(od_ref[...], wd_ref[...])
    ga = jax.nn.sigmoid(ga_ref[...].astype(F32))
    gb = jax.nn.sigmoid(gb_ref[...].astype(F32))
    o_ref[...] = (ga * u1 + gb * u2).astype(o_ref.dtype)


def _merge(o_nsa, o_d, wn, wd, proj, tm=1024, tn=512):
    T, kn = o_nsa.shape
    kd = o_d.shape[1]
    return pl.pallas_call(
        _merge_kernel,
        grid=(T // tm, D_MODEL // tn),
        in_specs=[
            pl.BlockSpec((tm, kn), lambda i, j: (i, 0)),
            pl.BlockSpec((tm, kd), lambda i, j: (i, 0)),
            pl.BlockSpec((kn, tn), lambda i, j: (0, j)),
            pl.BlockSpec((kd, tn), lambda i, j: (0, j)),
            pl.BlockSpec((tm, tn), lambda i, j: (i, COL_GM // tn + j)),
            pl.BlockSpec((tm, tn), lambda i, j: (i, (COL_GM + D_MODEL) // tn + j)),
        ],
        out_specs=pl.BlockSpec((tm, tn), lambda i, j: (i, j)),
        out_shape=jax.ShapeDtypeStruct((T, D_MODEL), BF16),
        compiler_params=_params("parallel", "arbitrary"),
    )(o_nsa, o_d, wn, wd, proj, proj)


def _out_proj_kernel(a_ref, w_ref, x_ref, o_ref):
    o_ref[...] = x_ref[...] + _dot(a_ref[...], w_ref[...])


def _out_proj(a, w, x2, tm=512, tn=2048):
    T, k = a.shape
    return pl.pallas_call(
        _out_proj_kernel,
        grid=(T // tm, D_MODEL // tn),
        in_specs=[
            pl.BlockSpec((tm, k), lambda i, j: (i, 0)),
            pl.BlockSpec((k, tn), lambda i, j: (0, j)),
            pl.BlockSpec((tm, tn), lambda i, j: (i, j)),
        ],
        out_specs=pl.BlockSpec((tm, tn), lambda i, j: (i, j)),
        out_shape=jax.ShapeDtypeStruct((T, D_MODEL), F32),
        compiler_params=_params("parallel", "arbitrary"),
    )(a, w, x2)


def _ffn_in_kernel(x_ref, g_ref, wa_ref, wb_ref, o_ref, h_ref):
    @pl.when(pl.program_id(1) == 0)
    def _():
        h_ref[...] = _norm_rows(x_ref[...], g_ref[...]).astype(BF16)

    h = h_ref[...]
    a = _dot(h, wa_ref[...])
    b = _dot(h, wb_ref[...])
    o_ref[...] = (a * jax.nn.sigmoid(a) * b).astype(o_ref.dtype)


def _ffn_in(x2, g, w, tm=1024, tn=512):
    T = x2.shape[0]
    nb = D_FF // tn
    return pl.pallas_call(
        _ffn_in_kernel,
        grid=(T // tm, nb),
        in_specs=[
            pl.BlockSpec((tm, D_MODEL), lambda i, j: (i, 0)),
            pl.BlockSpec((1, D_MODEL), lambda i, j: (0, 0)),
            pl.BlockSpec((D_MODEL, tn), lambda i, j: (0, j)),
            pl.BlockSpec((D_MODEL, tn), lambda i, j: (0, j + nb)),
        ],
        out_specs=pl.BlockSpec((tm, tn), lambda i, j: (i, j)),
        out_shape=jax.ShapeDtypeStruct((T, D_FF), BF16),
        scratch_shapes=[pltpu.VMEM((tm, D_MODEL), BF16)],
        compiler_params=_params("parallel", "arbitrary"),
    )(x2, g, w, w)


def _ffn_out_kernel(a_ref, w_ref, x_ref, g_ref, o_ref, acc_ref, *, final_norm):
    k = pl.program_id(1)

    @pl.when(k == 0)
    def _():
        acc_ref[...] = x_ref[...]

    acc_ref[...] += _dot(a_ref[...], w_ref[...])

    @pl.when(k == pl.num_programs(1) - 1)
    def _():
        y = acc_ref[...]
        o_ref[...] = _norm_rows(y, g_ref[...]) if final_norm else y


def _ffn_out(a, w, x2, g, final_norm, tm=512, tk=2816):
    T = a.shape[0]
    kernel = functools.partial(_ffn_out_kernel, final_norm=final_norm)
    return pl.pallas_call(
        kernel,
        grid=(T // tm, D_FF // tk),
        in_specs=[
            pl.BlockSpec((tm, tk), lambda i, k: (i, k)),
            pl.BlockSpec((tk, D_MODEL), lambda i, k: (k, 0)),
            pl.BlockSpec((tm, D_MODEL), lambda i, k: (i, 0)),
            pl.BlockSpec((1, D_MODEL), lambda i, k: (0, 0)),
        ],
        out_specs=pl.BlockSpec((tm, D_MODEL), lambda i, k: (i, 0)),
        out_shape=jax.ShapeDtypeStruct((T, D_MODEL), F32),
        scratch_shapes=[pltpu.VMEM((tm, D_MODEL), F32)],
        compiler_params=_params("parallel", "arbitrary"),
    )(a, w, x2, g)


def _t5_bucket(rel):
    n = jnp.maximum(rel, 0)
    nf = jnp.maximum(n, 1).astype(F32)
    large = REL_MAX_EXACT + (jnp.log(nf / REL_MAX_EXACT) / math.log(REL_MAX_DIST / REL_MAX_EXACT)
                             * (REL_BUCKETS - REL_MAX_EXACT)).astype(jnp.int32)
    large = jnp.minimum(large, REL_BUCKETS - 1)
    return jnp.where(n < REL_MAX_EXACT, n, large)


def _bias_of_rel(table, rel, valid):
    shifted = (table - table[REL_BUCKETS - 1:REL_BUCKETS]) * LOG2E
    vals = jnp.moveaxis(shifted[_t5_bucket(rel)], -1, 0)
    return jnp.where(valid[None], vals, NEG)


def _toeplitz_bias(table, n_i, n_j, offset, max_rel=None):
    rel = jnp.arange(n_i + n_j - 1, dtype=jnp.int32) - (n_j - 1) + offset
    valid = rel >= 0 if max_rel is None else (rel >= 0) & (rel < max_rel)
    v = _bias_of_rel(table, rel, valid)
    length = v.shape[-1]
    w = jnp.concatenate([v[:, n_j - 1::-1], v[:, :n_j - 1:-1]], axis=1)
    rows = jnp.tile(w, (1, n_i))[:, :n_i * (length - 1)].reshape(-1, n_i, length - 1)
    return rows[:, :, :n_j]


def _nsa_tables(table_nsa):
    TQ = NSA_TQ

    def stack(t):
        return t.reshape(NSA_GROUPS, NSA_HPG * TQ, t.shape[-1])

    i = jnp.arange(TQ, dtype=jnp.int32)[:, None]
    jc = jnp.arange(CMP_NEAR, dtype=jnp.int32)[None, :]
    rel_c = i - CMP_STRIDE * (jc - CMP_PAD_FRONT) - (CMP_LEN - 1)
    dc = jnp.stack([
        stack(_bias_of_rel(table_nsa, rel_c, (rel_c >= 0) & (jc >= CMP_PAD_FRONT))),
        stack(_bias_of_rel(table_nsa, rel_c, rel_c >= 0)),
    ])
    ds = jnp.stack([
        stack(_toeplitz_bias(table_nsa, TQ, 2 * TQ, 0)),
        stack(_toeplitz_bias(table_nsa, TQ, 2 * TQ, TQ)),
    ]).astype(BF16)
    win = stack(_toeplitz_bias(table_nsa, TQ, WINDOW + TQ, WINDOW, WINDOW))
    jw = jnp.arange(WINDOW + TQ, dtype=jnp.int32)[None, None, :]
    dw = jnp.stack([jnp.where(jw >= WINDOW - c * TQ, win, NEG) for c in range(3)]).astype(BF16)
    return dc, ds, dw


def _diff_table(table_diff):
    return _toeplitz_bias(table_diff, DIFF_TQ, 2 * DIFF_TK, DIFF_TK).astype(BF16)


def _selection_constants(S):
    n_sel = S // SEL_LEN
    n_cmp = S // CMP_STRIDE
    n_kt = S // NSA_TK
    blocks = NSA_TK // SEL_LEN
    j = jnp.arange(n_sel, dtype=jnp.int32)[:, None]
    col = jnp.arange(n_kt * LANES, dtype=jnp.int32)[None, :]
    perm = ((j == blocks * (col // LANES) + col % LANES) & (col % LANES < blocks)).astype(BF16)
    key = jnp.arange(NSA_TK, dtype=jnp.int32)[:, None]
    u = jnp.arange(LANES, dtype=jnp.int32)[None, :]
    eblk = (u == key // SEL_LEN).astype(BF16)
    c = jnp.arange(n_cmp + CMP_PAD_FRONT + CMP_PAD_BACK, dtype=jnp.int32)[None, :] - CMP_PAD_FRONT
    jb = jnp.arange(n_sel, dtype=jnp.int32)[:, None]
    d = c - (SEL_LEN // CMP_STRIDE) * jb
    wfar = jnp.where((d == 0) | (d == 4), 1.0, jnp.where((d >= 1) & (d <= 3), 2.0, 0.0))
    wfar = jnp.where((c >= 0) & (c < n_cmp), wfar, 0.0).astype(BF16)
    return perm, eblk, wfar


def kernel(x, norm_mix_g, w_in, cmp_pe_k, cmp_pe_v, cmp_w1_k, cmp_w2_k, cmp_w1_v, cmp_w2_v,
           diff_lq1, diff_lk1, diff_lq2, diff_lk2, diff_head_g, w_up_nsa, w_up_diff, w_out,
           norm_ff_g, w_ff_in, w_ff_out, rel_bias_table, norm_final_g):
    B, S, D = x.shape
    T = B * S
    depth = w_in.shape[0]
    n_sel = S // SEL_LEN
    n_cmp = S // CMP_STRIDE
    qscale = HEAD_DIM ** -0.5 * LOG2E

    dc, ds, dw = _nsa_tables(rel_bias_table[:, :NSA_HEADS])
    dd = _diff_table(rel_bias_table[:, NSA_HEADS:])
    perm, eblk, wfar = _selection_constants(S)

    gn0 = COL_QD
    gn1 = gn0 + NSA_HEADS * 3
    cs = jnp.ones((1, N_PROJ), F32)
    cs = cs.at[:, COL_QN:COL_KC].set(qscale).at[:, COL_QD:COL_KD].set(qscale)

    x2 = x.reshape(T, D)
    for l in range(depth):
        w_main = jnp.concatenate([w_in[l][:, :gn0], w_in[l][:, gn1:]], axis=1).astype(BF16)
        wg = w_in[l][:, gn0:gn1].reshape(D, NSA_GROUPS, NSA_HPG * 3)
        wg = jnp.pad(wg, ((0, 0), (0, 0), (0, LANES - NSA_HPG * 3))).reshape(D, NSA_GROUPS * LANES)
        proj, gates = _in_proj(x2, norm_mix_g[l][None], w_main, cs, wg.astype(BF16))
        proj3 = proj.reshape(B, S, N_PROJ)
        gates3 = gates.reshape(B, S, NSA_GROUPS * LANES)

        halves = proj3[:, :, COL_KC:COL_KS].reshape(B, n_cmp, CMP_STRIDE, 2, NSA_GROUPS, HEAD_DIM)
        halves = halves.transpose(3, 0, 4, 1, 2, 5).reshape(
            2, B, NSA_GROUPS, n_cmp, CMP_STRIDE * HEAD_DIM)
        pe = jnp.stack([cmp_pe_k[l], cmp_pe_v[l]]).reshape(2, 1, CMP_LEN * HEAD_DIM)
        w1 = jnp.stack([cmp_w1_k[l], cmp_w1_v[l]]).astype(BF16)
        w2 = jnp.stack([cmp_w2_k[l], cmp_w2_v[l]]).astype(BF16)
        cmp_kv = _compress(halves, pe, w1, w2)
        cmp_pad = jnp.pad(cmp_kv, ((0, 0), (0, 0), (0, 0), (CMP_PAD_FRONT, CMP_PAD_BACK), (0, 0)))

        o_nsa = _nsa(proj3, gates3, cmp_pad, dc, ds, dw, perm, eblk, wfar, n_sel)

        lam_init = 0.8 - 0.6 * math.exp(-0.3 * l)
        o_d = _diff(proj3, dd, diff_lq1[l][None], diff_lk1[l][None], diff_lq2[l][None],
                    diff_lk2[l][None], diff_head_g[l][:, None, :], lam_init)

        mix = _merge(o_nsa.reshape(T, -1), o_d.reshape(T, -1),
                     w_up_nsa[l].astype(BF16), w_up_diff[l].astype(BF16), proj)
        x2 = _out_proj(mix, w_out[l].astype(BF16), x2)

        act = _ffn_in(x2, norm_ff_g[l][None], w_ff_in[l].astype(BF16))
        last = l == depth - 1
        x2 = _ffn_out(act, w_ff_out[l].astype(BF16), x2,
                      norm_final_g[None] if last else norm_ff_g[l][None], final_norm=last)
    return x2.reshape(B, S, D)
```

```python
import functools
import math

import jax
import jax.numpy as jnp
from jax import lax
from jax.experimental import pallas as pl
from jax.experimental.pallas import tpu as pltpu

D_MODEL = 2048
HEAD_DIM = 128
NSA_HEADS = 8
NSA_GROUPS = 2
NSA_HPG = NSA_HEADS // NSA_GROUPS
CMP_LEN = 32
CMP_STRIDE = 16
SEL_LEN = 64
SEL_TOPK = 16
WINDOW = 512
CMP_HIDDEN = 256
DIFF_HEADS = 4
DIFF_V_DIM = 2 * HEAD_DIM
REL_BUCKETS = 32
REL_MAX_EXACT = 16
REL_MAX_DIST = 128
D_FF = -(-8 * D_MODEL // (3 * 256)) * 256
EPS = 1e-6
NEG = -1e30
BIG = 1e30
LOG2E = math.log2(math.e)

F32 = jnp.float32
BF16 = jnp.bfloat16
LANES = 128

VMEM_LIMIT_BYTES = 56 * 1024 * 1024

COL_QN = 0
COL_KC = 1024
COL_KS = 1536
COL_VS = 1792
COL_KW = 2048
COL_VW = 2304
COL_QD = 2560
COL_KD = 3584
COL_VD = 4608
COL_GM = 5632
N_PROJ = 9728

NSA_TQ = 256
NSA_TK = 1024
NSA_ROWS = 32
CMP_NEAR = 32
CMP_PAD_FRONT = 16
CMP_PAD_BACK = 112
DIFF_TQ = 512
DIFF_TK = 512
DIFF_ROWS = 64


def _dot(a, b):
    return jnp.dot(a, b, preferred_element_type=F32)


def _dot_nt(a, b):
    return lax.dot_general(a, b, (((1,), (1,)), ((), ())), preferred_element_type=F32)


def _params(*sem):
    return pltpu.CompilerParams(dimension_semantics=sem, vmem_limit_bytes=VMEM_LIMIT_BYTES)


def _norm_rows(x, g):
    ms = jnp.mean(x * x, axis=-1, keepdims=True)
    return x * lax.rsqrt(ms + EPS) * g


def _in_proj_kernel(x_ref, g_ref, w_ref, cs_ref, wg_ref, o_ref, og_ref, h_ref):
    @pl.when(pl.program_id(1) == 0)
    def _():
        hb = _norm_rows(x_ref[...], g_ref[...]).astype(BF16)
        h_ref[...] = hb
        og_ref[...] = jax.nn.sigmoid(_dot(hb, wg_ref[...]))

    o_ref[...] = (_dot(h_ref[...], w_ref[...]) * cs_ref[...]).astype(o_ref.dtype)


def _in_proj(x2, g, w, cs, wg, tm=512, tn=2432):
    T = x2.shape[0]
    n = w.shape[1]
    ng = wg.shape[1]
    return pl.pallas_call(
        _in_proj_kernel,
        grid=(T // tm, n // tn),
        in_specs=[
            pl.BlockSpec((tm, D_MODEL), lambda i, j: (i, 0)),
            pl.BlockSpec((1, D_MODEL), lambda i, j: (0, 0)),
            pl.BlockSpec((D_MODEL, tn), lambda i, j: (0, j)),
            pl.BlockSpec((1, tn), lambda i, j: (0, j)),
            pl.BlockSpec((D_MODEL, ng), lambda i, j: (0, 0)),
        ],
        out_specs=[
            pl.BlockSpec((tm, tn), lambda i, j: (i, j)),
            pl.BlockSpec((tm, ng), lambda i, j: (i, 0)),
        ],
        out_shape=[
            jax.ShapeDtypeStruct((T, n), BF16),
            jax.ShapeDtypeStruct((T, ng), F32),
        ],
        scratch_shapes=[pltpu.VMEM((tm, D_MODEL), BF16)],
        compiler_params=_params("parallel", "arbitrary"),
    )(x2, g, w, cs, wg)


def _gelu_tanh(x):
    return 0.5 * x * (1.0 + jnp.tanh(math.sqrt(2.0 / math.pi) * (x + 0.044715 * (x * x * x))))


def _compress_kernel(h_ref, pe_ref, w1_ref, w2_ref, o_ref):
    hv = h_ref[0, 0, 0]
    nc = hv.shape[0]
    half = CMP_STRIDE * HEAD_DIM
    ya = _dot(hv, w1_ref[0, :half, :])
    yb = _dot(hv, w1_ref[0, half:, :])
    yb = pltpu.roll(yb, nc - 1, 0)
    row = lax.broadcasted_iota(jnp.int32, yb.shape, 0)
    yb = jnp.where(row == nc - 1, 0.0, yb)
    pe8 = jnp.broadcast_to(pe_ref[0], (8, 2 * half)).astype(BF16)
    pec = _dot(pe8, w1_ref[0])[0:1]
    hid = _gelu_tanh(ya + yb + pec)
    o_ref[0, 0, 0] = _dot(hid.astype(BF16), w2_ref[0]).astype(o_ref.dtype)


def _compress(halves, pe, w1, w2):
    _, B, G, nc, hw = halves.shape
    return pl.pallas_call(
        _compress_kernel,
        grid=(2, B, G),
        in_specs=[
            pl.BlockSpec((1, 1, 1, nc, hw), lambda s, b, g: (s, b, g, 0, 0)),
            pl.BlockSpec((1, 1, 2 * hw), lambda s, b, g: (s, 0, 0)),
            pl.BlockSpec((1, 2 * hw, CMP_HIDDEN), lambda s, b, g: (s, 0, 0)),
            pl.BlockSpec((1, CMP_HIDDEN, HEAD_DIM), lambda s, b, g: (s, 0, 0)),
        ],
        out_specs=pl.BlockSpec((1, 1, 1, nc, HEAD_DIM), lambda s, b, g: (s, b, g, 0, 0)),
        out_shape=jax.ShapeDtypeStruct((2, B, G, nc, HEAD_DIM), BF16),
        compiler_params=_params("parallel", "parallel", "parallel"),
    )(halves, pe, w1, w2)


def _lanes(x, width):
    return jnp.concatenate([x] * (width // x.shape[1]), axis=1)


def _init_state(m_ref, l_ref, acc_ref):
    m_ref[...] = jnp.full(m_ref.shape, NEG, F32)
    l_ref[...] = jnp.zeros(l_ref.shape, F32)
    acc_ref[...] = jnp.zeros(acc_ref.shape, F32)


def _softmax_tile(slot, width, s_ref, p_ref, a_ref, m_ref, l_ref, chunk):
    n_rows = p_ref.shape[1]
    for r in range(0, n_rows, chunk):
        rows = slice(r, r + chunk)
        m_prev = m_ref[rows, :]
        m_new = jnp.maximum(m_prev, jnp.max(s_ref[slot, rows, :width], axis=-1, keepdims=True))
        a_ref[slot, rows, :] = jnp.exp2(m_prev - m_new)
        m_ref[rows, :] = m_new
    for r in range(0, n_rows, chunk):
        rows = slice(r, r + chunk)
        p = jnp.exp2(s_ref[slot, rows, :width] - _lanes(m_ref[rows, :], width))
        l_ref[rows, :] = (a_ref[slot, rows, :] * l_ref[rows, :]
                          + jnp.sum(p, axis=-1, keepdims=True))
        p_ref[slot, rows, :width] = p.astype(BF16)


def _value_tile(slot, width, v, p_ref, a_ref, acc_ref):
    acc_ref[...] = (acc_ref[...] * _lanes(a_ref[slot], acc_ref.shape[1])
                    + _dot(p_ref[slot, :, :width], v))


def _fold_tile(slot, width, v, s_ref, p_ref, a_ref, m_ref, l_ref, acc_ref, chunk):
    _softmax_tile(slot, width, s_ref, p_ref, a_ref, m_ref, l_ref, chunk)
    _value_tile(slot, width, v, p_ref, a_ref, acc_ref)


def _fold_tile_inplace(slot, width, v, s_ref, p_ref, m_ref, l_ref, acc_ref, chunk):
    n_rows = p_ref.shape[1]
    for r in range(0, n_rows, chunk):
        rows = slice(r, r + chunk)
        s = s_ref[slot, rows, :width]
        m_prev = m_ref[rows, :]
        m_new = jnp.maximum(m_prev, jnp.max(s, axis=-1, keepdims=True))
        alpha = jnp.exp2(m_prev - m_new)
        p = jnp.exp2(s - _lanes(m_new, width))
        l_ref[rows, :] = alpha * l_ref[rows, :] + jnp.sum(p, axis=-1, keepdims=True)
        m_ref[rows, :] = m_new
        acc_ref[rows, :] = acc_ref[rows, :] * _lanes(alpha, acc_ref.shape[1])
        p_ref[0, rows, :width] = p.astype(BF16)
    acc_ref[...] += _dot(p_ref[0, :, :width], v)


def _finish(l_ref, acc_ref):
    return acc_ref[...] / _lanes(l_ref[...], acc_ref.shape[1])


def _split_dot_nt(w, a):
    hi = a.astype(BF16)
    lo = (a - hi.astype(F32)).astype(BF16)
    return _dot_nt(w, hi) + _dot_nt(w, lo)


def _nsa_kernel(q_ref, gate_ref, kc_ref, vc_ref, ks_ref, vs_ref,
                kw0_ref, kw1_ref, kw2_ref, vw0_ref, vw1_ref, vw2_ref,
                dc_ref, ds_ref, dw_ref, perm_ref, eblk_ref, wfar_ref, o_ref,
                s_ref, p_ref, a_ref, m_ref, l_ref, acc_ref, imp_ref, sn_ref, pn_ref, selt_ref,
                sel_ref, q4_ref, pick_ref,
                *, n_sel, top_n):
    qt = pl.program_id(2)
    TQ, P = NSA_TQ, NSA_HPG
    R = P * TQ
    blocks_per_tile = NSA_TQ // SEL_LEN
    stats = (s_ref, p_ref, a_ref, m_ref, l_ref)
    state = stats + (acc_ref,)
    qblk = q_ref[0]
    q4_ref[...] = jnp.concatenate(
        [qblk[:, p * HEAD_DIM:(p + 1) * HEAD_DIM] for p in range(P)], axis=0)

    ncp = kc_ref.shape[3]
    cmp_per_tile = NSA_TQ // CMP_STRIDE
    lane = lax.broadcasted_iota(jnp.int32, (1, ncp), 1)
    far_bias = jnp.where((lane >= CMP_PAD_FRONT) & (lane < cmp_per_tile * qt), 0.0, NEG)
    c0 = pl.multiple_of(cmp_per_tile * qt, cmp_per_tile)
    kn = kc_ref[0, 0, 0, pl.ds(c0, CMP_NEAR), :]
    vn = vc_ref[0, 0, 0, pl.ds(c0, CMP_NEAR), :]
    s_ref[0, :, :ncp] = _dot_nt(q4_ref[...], kc_ref[0, 0, 0])
    sn_ref[...] = _dot_nt(q4_ref[...], kn) + dc_ref[0, 0]
    imp_ref[...] = jnp.zeros(imp_ref.shape, F32)

    def cmp_head(p, carry):
        for c in range(TQ // NSA_ROWS):
            rows = pl.ds(pl.multiple_of(p * TQ + c * NSA_ROWS, NSA_ROWS), NSA_ROWS)
            irows = slice(c * NSA_ROWS, (c + 1) * NSA_ROWS)
            sf = s_ref[0, rows, :ncp] + far_bias
            sn = sn_ref[rows, :]
            m = jnp.maximum(jnp.max(sf, axis=-1, keepdims=True),
                            jnp.max(sn, axis=-1, keepdims=True))
            ef = jnp.exp2(sf - m)
            en = jnp.exp2(sn - m)
            l = jnp.sum(ef, axis=-1, keepdims=True) + jnp.sum(en, axis=-1, keepdims=True)
            inv = jnp.where(m > 0.5 * NEG, 1.0 / l, 0.0)
            pf = ef * inv
            p_ref[0, rows, :ncp] = pf.astype(BF16)
            pn_ref[rows, :] = en * inv
            imp_ref[irows, :] += pf
        return carry

    lax.fori_loop(0, P, cmp_head, 0)
    p_near = pn_ref[...]
    o_c = _dot(p_ref[0, :, :ncp], vc_ref[0, 0, 0]) + _dot(p_near.astype(BF16), vn)

    imp_near = jnp.sum(p_near.reshape(P, TQ, CMP_NEAR), axis=0)
    jj = lax.broadcasted_iota(jnp.int32, (n_sel, CMP_NEAR), 0)
    jn = lax.broadcasted_iota(jnp.int32, (n_sel, CMP_NEAR), 1)
    d = (cmp_per_tile * qt - CMP_PAD_FRONT + jn) - (SEL_LEN // CMP_STRIDE) * jj
    w_near = jnp.where((d == 0) | (d == 4), 1.0,
                       jnp.where((d >= 1) & (d <= 3), 2.0, 0.0)).astype(BF16)
    p_slc = _split_dot_nt(wfar_ref[...], imp_ref[...]) + _split_dot_nt(w_near, imp_near)

    bj = lax.broadcasted_iota(jnp.int32, (n_sel, TQ), 0)
    ti = lax.broadcasted_iota(jnp.int32, (n_sel, TQ), 1)
    jt = blocks_per_tile * qt + ti // SEL_LEN
    forced = (bj == 0) | (bj == jt) | (bj == jt - 1)
    pick_ref[...] = jnp.where(forced, -jnp.inf, jnp.where(bj > jt, NEG, p_slc))
    idx = lax.broadcasted_iota(jnp.int32, (n_sel, LANES), 0).astype(F32)

    def pick(_, sc):
        mx = jnp.max(sc, axis=0, keepdims=True)
        first = jnp.min(jnp.where(sc == mx, idx, float(n_sel)), axis=0, keepdims=True)
        return jnp.where(idx == first, -jnp.inf, sc)

    for h in range(0, TQ, LANES):
        pick_ref[:, h:h + LANES] = lax.fori_loop(0, top_n - 3, pick, pick_ref[:, h:h + LANES])
    selneg_t = jnp.where(pick_ref[...] == -jnp.inf, 0.0, NEG)
    sel_ref[...] = selneg_t.T
    selneg = sel_ref[...]
    bj = lax.broadcasted_iota(jnp.int32, (TQ, n_sel), 1)

    nb0 = jnp.maximum(blocks_per_tile * (qt - 1), 0)
    pj = lax.broadcasted_iota(jnp.int32, (n_sel, LANES), 0)
    pu = lax.broadcasted_iota(jnp.int32, (n_sel, LANES), 1)
    perm_near = ((pj == nb0 + pu) & (pu < 2 * blocks_per_tile)).astype(BF16)
    sel_near = _dot(selneg.astype(BF16), perm_near).astype(BF16)
    sel_far = jnp.where(bj < blocks_per_tile * (qt - 1), selneg, NEG).astype(BF16)
    sel_all = _dot(sel_far, perm_ref[...]).astype(BF16)
    n_kt = selt_ref.shape[0] - 1
    for kt in range(n_kt):
        selt_ref[kt] = sel_all[:, kt * LANES:(kt + 1) * LANES]
    col = lax.broadcasted_iota(jnp.int32, (TQ, LANES), 1)
    selt_ref[n_kt] = jnp.where(col < NSA_TK // SEL_LEN, NEG, 0.0).astype(BF16)

    _init_state(m_ref, l_ref, acc_ref)
    near_keys = 2 * NSA_TQ
    ns = pl.multiple_of(nb0 * SEL_LEN, NSA_TQ)
    lhs = jnp.concatenate([q4_ref[...], jnp.concatenate([sel_near] * P, axis=0)], axis=1)
    rhs = jnp.concatenate([ks_ref[0, pl.ds(ns, near_keys), :], eblk_ref[:near_keys, :]], axis=1)
    s_ref[0, :, :near_keys] = _dot_nt(lhs, rhs) + ds_ref[0, 0].astype(F32)

    def far_scores(kt, sel_idx, slot):
        k0 = pl.multiple_of(kt * NSA_TK, NSA_TK)
        lhs_t = jnp.concatenate(
            [q4_ref[...], jnp.concatenate([selt_ref[sel_idx]] * P, axis=0)], axis=1)
        rhs_t = jnp.concatenate([ks_ref[0, pl.ds(k0, NSA_TK), :], eblk_ref[...]], axis=1)
        s_ref[slot, :, :NSA_TK] = _dot_nt(lhs_t, rhs_t)

    def far_softmax(slot):
        _softmax_tile(slot, NSA_TK, *stats, NSA_ROWS)

    def far_values(kt, slot):
        k0 = pl.multiple_of(kt * NSA_TK, NSA_TK)
        _value_tile(slot, NSA_TK, vs_ref[0, pl.ds(k0, NSA_TK), :], p_ref, a_ref, acc_ref)

    n_far = (qt + 2) // 4
    n_pairs = n_far // 2
    odd = jnp.maximum(n_far - 1, 0)
    odd_sel = jnp.where(n_far % 2 == 1, odd, selt_ref.shape[0] - 1)
    last_pair_tile = jnp.maximum(n_far - 2, 0)
    far_scores(odd, odd_sel, 1)
    _fold_tile(0, near_keys, vs_ref[0, pl.ds(ns, near_keys), :], *state, NSA_ROWS)
    far_scores(0, 0, 0)
    far_softmax(1)

    def far_pair(i, carry):
        a = 2 * i
        far_values(jnp.where(i == 0, odd, a - 1), 1)
        far_scores(a + 1, a + 1, 1)
        far_softmax(0)
        nxt = jnp.minimum(a + 2, last_pair_tile)
        far_values(a, 0)
        far_scores(nxt, nxt, 0)
        far_softmax(1)
        return carry

    lax.fori_loop(0, n_pairs, far_pair, 0)
    far_values(jnp.where(n_pairs == 0, odd, 2 * n_pairs - 1), 1)
    o_s = _finish(l_ref, acc_ref)

    _init_state(m_ref, l_ref, acc_ref)
    kw = jnp.concatenate([kw0_ref[0], kw1_ref[0], kw2_ref[0]], axis=0)
    vw = jnp.concatenate([vw0_ref[0], vw1_ref[0], vw2_ref[0]], axis=0)
    s_ref[0, :, :WINDOW + NSA_TQ] = _dot_nt(q4_ref[...], kw) + dw_ref[0, 0].astype(F32)
    _fold_tile(0, WINDOW + NSA_TQ, vw, *state, NSA_ROWS)
    o_w = _finish(l_ref, acc_ref)

    gate = gate_ref[0]
    outs = []
    for p in range(P):
        rows = slice(p * TQ, (p + 1) * TQ)
        gc = gate[:, 3 * p:3 * p + 1]
        gs = gate[:, 3 * p + 1:3 * p + 2]
        gw = gate[:, 3 * p + 2:3 * p + 3]
        outs.append(gc * o_c[rows] + gs * o_s[rows] + gw * o_w[rows])
    o_ref[0] = jnp.concatenate(outs, axis=1).astype(o_ref.dtype)


def _nsa(proj3, gates3, cmp_pad, dc, ds, dw, perm, eblk, wfar, n_sel):
    B, S, _ = proj3.shape
    G = NSA_GROUPS
    ncp = cmp_pad.shape[3]
    top_n = min(SEL_TOPK, n_sel)
    assert top_n > 3, "the selection loop assumes the three forced blocks fit in the top-n"
    qw = NSA_HPG * HEAD_DIM
    R = NSA_HPG * NSA_TQ
    wide = max(NSA_TK, ncp, WINDOW + NSA_TQ)
    wblk = lambda col, back: pl.BlockSpec(
        (1, NSA_TQ, HEAD_DIM),
        lambda b, g, t, col=col, back=back: (b, jnp.maximum(t - back, 0), col // HEAD_DIM + g))
    kernel = functools.partial(_nsa_kernel, n_sel=n_sel, top_n=top_n)
    return pl.pallas_call(
        kernel,
        grid=(B, G, S // NSA_TQ),
        in_specs=[
            pl.BlockSpec((1, NSA_TQ, qw), lambda b, g, t: (b, t, g)),
            pl.BlockSpec((1, NSA_TQ, LANES), lambda b, g, t: (b, t, g)),
            pl.BlockSpec((1, 1, 1, ncp, HEAD_DIM), lambda b, g, t: (0, b, g, 0, 0)),
            pl.BlockSpec((1, 1, 1, ncp, HEAD_DIM), lambda b, g, t: (1, b, g, 0, 0)),
            pl.BlockSpec((1, S, HEAD_DIM), lambda b, g, t: (b, 0, COL_KS // HEAD_DIM + g)),
            pl.BlockSpec((1, S, HEAD_DIM), lambda b, g, t: (b, 0, COL_VS // HEAD_DIM + g)),
            wblk(COL_KW, 2), wblk(COL_KW, 1), wblk(COL_KW, 0),
            wblk(COL_VW, 2), wblk(COL_VW, 1), wblk(COL_VW, 0),
            pl.BlockSpec((1, 1) + dc.shape[2:], lambda b, g, t: (jnp.minimum(t, 1), g, 0, 0)),
            pl.BlockSpec((1, 1) + ds.shape[2:], lambda b, g, t: (jnp.minimum(t, 1), g, 0, 0)),
            pl.BlockSpec((1, 1) + dw.shape[2:], lambda b, g, t: (jnp.minimum(t, 2), g, 0, 0)),
            pl.BlockSpec(perm.shape, lambda b, g, t: (0, 0)),
            pl.BlockSpec(eblk.shape, lambda b, g, t: (0, 0)),
            pl.BlockSpec(wfar.shape, lambda b, g, t: (0, 0)),
        ],
        out_specs=pl.BlockSpec((1, NSA_TQ, qw), lambda b, g, t: (b, t, g)),
        out_shape=jax.ShapeDtypeStruct((B, S, NSA_HEADS * HEAD_DIM), BF16),
        scratch_shapes=[
            pltpu.VMEM((2, R, wide), F32),
            pltpu.VMEM((2, R, wide), BF16),
            pltpu.VMEM((2, R, LANES), F32),
            pltpu.VMEM((R, LANES), F32),
            pltpu.VMEM((R, LANES), F32),
            pltpu.VMEM((R, HEAD_DIM), F32),
            pltpu.VMEM((NSA_TQ, ncp), F32),
            pltpu.VMEM((R, CMP_NEAR), F32),
            pltpu.VMEM((R, CMP_NEAR), F32),
            pltpu.VMEM((S // NSA_TK + 1, NSA_TQ, LANES), BF16),
            pltpu.VMEM((NSA_TQ, n_sel), F32),
            pltpu.VMEM((R, HEAD_DIM), BF16),
            pltpu.VMEM((n_sel, NSA_TQ), F32),
        ],
        compiler_params=_params("parallel", "parallel", "arbitrary"),
    )(proj3, gates3, cmp_pad, cmp_pad, proj3, proj3,
      proj3, proj3, proj3, proj3, proj3, proj3, dc, ds, dw, perm, eblk, wfar)


def _diff_kernel(q_ref, k_ref, v_ref, dd_ref, lq1_ref, lk1_ref, lq2_ref, lk2_ref, hg_ref,
                 o_ref, s_ref, p_ref, m_ref, l_ref, acc_ref, *, lam_init):
    qt = pl.program_id(2)
    TQ = DIFF_TQ
    q = q_ref[0]
    zero = jnp.zeros((TQ, HEAD_DIM), BF16)
    lhs = jnp.concatenate([
        jnp.concatenate([q[:, :HEAD_DIM], zero], axis=1),
        jnp.concatenate([zero, q[:, HEAD_DIM:]], axis=1)], axis=0)
    lam = (jnp.exp(jnp.sum(lq1_ref[...] * lk1_ref[...], axis=-1, keepdims=True))
           - jnp.exp(jnp.sum(lq2_ref[...] * lk2_ref[...], axis=-1, keepdims=True)) + lam_init)

    def scores(kt, slot, bias):
        k0 = pl.multiple_of(kt * DIFF_TK, DIFF_TK)
        sc = _dot_nt(lhs, k_ref[0, pl.ds(k0, DIFF_TK), :])
        if bias is not None:
            sc = (sc.reshape(2, TQ, DIFF_TK) + bias[None]).reshape(2 * TQ, DIFF_TK)
        s_ref[slot] = sc

    def fold(kt, slot):
        k0 = pl.multiple_of(kt * DIFF_TK, DIFF_TK)
        _fold_tile_inplace(slot, DIFF_TK, v_ref[0, pl.ds(k0, DIFF_TK), :],
                           s_ref, p_ref, m_ref, l_ref, acc_ref, DIFF_ROWS)

    _init_state(m_ref, l_ref, acc_ref)
    n_far = jnp.maximum(qt - 1, 0)
    prev = jnp.maximum(qt - 1, 0)
    odd = jnp.maximum(n_far - 1, 0)
    last_pair_tile = jnp.maximum(n_far - 2, 0)
    scores(qt, 0, dd_ref[0, :, DIFF_TK:].astype(F32))
    fold(qt, 0)
    scores(prev, 0, jnp.where(qt >= 1, dd_ref[0, :, :DIFF_TK].astype(F32), NEG))
    fold(prev, 0)
    scores(odd, 0, jnp.where(n_far % 2 == 1, jnp.zeros((TQ, DIFF_TK), F32), NEG))
    scores(0, 1, None)
    fold(odd, 0)

    def far_pair(i, carry):
        a = 2 * i
        scores(a + 1, 0, None)
        fold(a, 1)
        scores(jnp.minimum(a + 2, last_pair_tile), 1, None)
        fold(a + 1, 0)
        return carry

    lax.fori_loop(0, n_far // 2, far_pair, 0)
    a = _finish(l_ref, acc_ref)
    o = a[:TQ] - lam * a[TQ:]
    y = _norm_rows(o, hg_ref[0]) * (1.0 - lam_init)
    o_ref[0] = y.astype(o_ref.dtype)


def _diff(proj3, dd, lq1, lk1, lq2, lk2, hg, lam_init):
    B, S, _ = proj3.shape
    H = DIFF_HEADS
    w = 2 * HEAD_DIM
    R = 2 * DIFF_TQ
    vec = pl.BlockSpec((1, HEAD_DIM), lambda b, h, t: (0, 0))
    kernel = functools.partial(_diff_kernel, lam_init=lam_init)
    return pl.pallas_call(
        kernel,
        grid=(B, H, S // DIFF_TQ),
        in_specs=[
            pl.BlockSpec((1, DIFF_TQ, w), lambda b, h, t: (b, t, COL_QD // w + h)),
            pl.BlockSpec((1, S, w), lambda b, h, t: (b, 0, COL_KD // w + h)),
            pl.BlockSpec((1, S, w), lambda b, h, t: (b, 0, COL_VD // w + h)),
            pl.BlockSpec((1,) + dd.shape[1:], lambda b, h, t: (h, 0, 0)),
            vec, vec, vec, vec,
            pl.BlockSpec((1, 1, w), lambda b, h, t: (h, 0, 0)),
        ],
        out_specs=pl.BlockSpec((1, DIFF_TQ, w), lambda b, h, t: (b, t, h)),
        out_shape=jax.ShapeDtypeStruct((B, S, H * w), BF16),
        scratch_shapes=[
            pltpu.VMEM((2, R, DIFF_TK), F32),
            pltpu.VMEM((1, R, DIFF_TK), BF16),
            pltpu.VMEM((R, LANES), F32),
            pltpu.VMEM((R, LANES), F32),
            pltpu.VMEM((R, DIFF_V_DIM), F32),
        ],
        compiler_params=_params("parallel", "parallel", "arbitrary"),
    )(proj3, proj3, proj3, dd, lq1, lk1, lq2, lk2, hg)


def _merge_kernel(on_ref, od_ref, wn_ref, wd_ref, ga_ref, gb_ref, o_ref):
    u1 = _dot(on_ref[...], wn_ref[...])
    u2 = _dot(od_ref[...], wd_ref[...])
    ga = jax.nn.sigmoid(ga_ref[...].astype(F32))
    gb = jax.nn.sigmoid(gb_ref[...].astype(F32))
    o_ref[...] = (ga * u1 + gb * u2).astype(o_ref.dtype)


def _merge(o_nsa, o_d, wn, wd, proj, tm=1024, tn=512):
    T, kn = o_nsa.shape
    kd = o_d.shape[1]
    return pl.pallas_call(
        _merge_kernel,
        grid=(T // tm, D_MODEL // tn),
        in_specs=[
            pl.BlockSpec((tm, kn), lambda i, j: (i, 0)),
            pl.BlockSpec((tm, kd), lambda i, j: (i, 0)),
            pl.BlockSpec((kn, tn), lambda i, j: (0, j)),
            pl.BlockSpec((kd, tn), lambda i, j: (0, j)),
            pl.BlockSpec((tm, tn), lambda i, j: (i, COL_GM // tn + j)),
            pl.BlockSpec((tm, tn), lambda i, j: (i, (COL_GM + D_MODEL) // tn + j)),
        ],
        out_specs=pl.BlockSpec((tm, tn), lambda i, j: (i, j)),
        out_shape=jax.ShapeDtypeStruct((T, D_MODEL), BF16),
        compiler_params=_params("parallel", "arbitrary"),
    )(o_nsa, o_d, wn, wd, proj, proj)


def _out_proj_kernel(a_ref, w_ref, x_ref, o_ref):
    o_ref[...] = x_ref[...] + _dot(a_ref[...], w_ref[...])


def _out_proj(a, w, x2, tm=512, tn=2048):
    T, k = a.shape
    return pl.pallas_call(
        _out_proj_kernel,
        grid=(T // tm, D_MODEL // tn),
        in_specs=[
            pl.BlockSpec((tm, k), lambda i, j: (i, 0)),
            pl.BlockSpec((k, tn), lambda i, j: (0, j)),
            pl.BlockSpec((tm, tn), lambda i, j: (i, j)),
        ],
        out_specs=pl.BlockSpec((tm, tn), lambda i, j: (i, j)),
        out_shape=jax.ShapeDtypeStruct((T, D_MODEL), F32),
        compiler_params=_params("parallel", "arbitrary"),
    )(a, w, x2)


def _ffn_in_kernel(x_ref, g_ref, wa_ref, wb_ref, o_ref, h_ref):
    @pl.when(pl.program_id(1) == 0)
    def _():
        h_ref[...] = _norm_rows(x_ref[...], g_ref[...]).astype(BF16)

    h = h_ref[...]
    a = _dot(h, wa_ref[...])
    b = _dot(h, wb_ref[...])
    o_ref[...] = (a * jax.nn.sigmoid(a) * b).astype(o_ref.dtype)


def _ffn_in(x2, g, w, tm=1024, tn=512):
    T = x2.shape[0]
    nb = D_FF // tn
    return pl.pallas_call(
        _ffn_in_kernel,
        grid=(T // tm, nb),
        in_specs=[
            pl.BlockSpec((tm, D_MODEL), lambda i, j: (i, 0)),
            pl.BlockSpec((1, D_MODEL), lambda i, j: (0, 0)),
            pl.BlockSpec((D_MODEL, tn), lambda i, j: (0, j)),
            pl.BlockSpec((D_MODEL, tn), lambda i, j: (0, j + nb)),
        ],
        out_specs=pl.BlockSpec((tm, tn), lambda i, j: (i, j)),
        out_shape=jax.ShapeDtypeStruct((T, D_FF), BF16),
        scratch_shapes=[pltpu.VMEM((tm, D_MODEL), BF16)],
        compiler_params=_params("parallel", "arbitrary"),
    )(x2, g, w, w)


def _ffn_out_kernel(a_ref, w_ref, x_ref, g_ref, o_ref, acc_ref, *, final_norm):
    k = pl.program_id(1)

    @pl.when(k == 0)
    def _():
        acc_ref[...] = x_ref[...]

    acc_ref[...] += _dot(a_ref[...], w_ref[...])

    @pl.when(k == pl.num_programs(1) - 1)
    def _():
        y = acc_ref[...]
        o_ref[...] = _norm_rows(y, g_ref[...]) if final_norm else y


def _ffn_out(a, w, x2, g, final_norm, tm=512, tk=2816):
    T = a.shape[0]
    kernel = functools.partial(_ffn_out_kernel, final_norm=final_norm)
    return pl.pallas_call(
        kernel,
        grid=(T // tm, D_FF // tk),
        in_specs=[
            pl.BlockSpec((tm, tk), lambda i, k: (i, k)),
            pl.BlockSpec((tk, D_MODEL), lambda i, k: (k, 0)),
            pl.BlockSpec((tm, D_MODEL), lambda i, k: (i, 0)),
            pl.BlockSpec((1, D_MODEL), lambda i, k: (0, 0)),
        ],
        out_specs=pl.BlockSpec((tm, D_MODEL), lambda i, k: (i, 0)),
        out_shape=jax.ShapeDtypeStruct((T, D_MODEL), F32),
        scratch_shapes=[pltpu.VMEM((tm, D_MODEL), F32)],
        compiler_params=_params("parallel", "arbitrary"),
    )(a, w, x2, g)


def _t5_bucket(rel):
    n = jnp.maximum(rel, 0)
    nf = jnp.maximum(n, 1).astype(F32)
    large = REL_MAX_EXACT + (jnp.log(nf / REL_MAX_EXACT) / math.log(REL_MAX_DIST / REL_MAX_EXACT)
                             * (REL_BUCKETS - REL_MAX_EXACT)).astype(jnp.int32)
    large = jnp.minimum(large, REL_BUCKETS - 1)
    return jnp.where(n < REL_MAX_EXACT, n, large)


def _bias_of_rel(table, rel, valid):
    shifted = (table - table[REL_BUCKETS - 1:REL_BUCKETS]) * LOG2E
    vals = jnp.moveaxis(shifted[_t5_bucket(rel)], -1, 0)
    return jnp.where(valid[None], vals, NEG)


def _toeplitz_bias(table, n_i, n_j, offset, max_rel=None):
    rel = jnp.arange(n_i + n_j - 1, dtype=jnp.int32) - (n_j - 1) + offset
    valid = rel >= 0 if max_rel is None else (rel >= 0) & (rel < max_rel)
    v = _bias_of_rel(table, rel, valid)
    length = v.shape[-1]
    w = jnp.concatenate([v[:, n_j - 1::-1], v[:, :n_j - 1:-1]], axis=1)
    rows = jnp.tile(w, (1, n_i))[:, :n_i * (length - 1)].reshape(-1, n_i, length - 1)
    return rows[:, :, :n_j]


def _nsa_tables(table_nsa):
    TQ = NSA_TQ

    def stack(t):
        return t.reshape(NSA_GROUPS, NSA_HPG * TQ, t.shape[-1])

    i = jnp.arange(TQ, dtype=jnp.int32)[:, None]
    jc = jnp.arange(CMP_NEAR, dtype=jnp.int32)[None, :]
    rel_c = i - CMP_STRIDE * (jc - CMP_PAD_FRONT) - (CMP_LEN - 1)
    dc = jnp.stack([
        stack(_bias_of_rel(table_nsa, rel_c, (rel_c >= 0) & (jc >= CMP_PAD_FRONT))),
        stack(_bias_of_rel(table_nsa, rel_c, rel_c >= 0)),
    ])
    ds = jnp.stack([
        stack(_toeplitz_bias(table_nsa, TQ, 2 * TQ, 0)),
        stack(_toeplitz_bias(table_nsa, TQ, 2 * TQ, TQ)),
    ]).astype(BF16)
    win = stack(_toeplitz_bias(table_nsa, TQ, WINDOW + TQ, WINDOW, WINDOW))
    jw = jnp.arange(WINDOW + TQ, dtype=jnp.int32)[None, None, :]
    dw = jnp.stack([jnp.where(jw >= WINDOW - c * TQ, win, NEG) for c in range(3)]).astype(BF16)
    return dc, ds, dw


def _diff_table(table_diff):
    return _toeplitz_bias(table_diff, DIFF_TQ, 2 * DIFF_TK, DIFF_TK).astype(BF16)


def _selection_constants(S):
    n_sel = S // SEL_LEN
    n_cmp = S // CMP_STRIDE
    n_kt = S // NSA_TK
    blocks = NSA_TK // SEL_LEN
    j = jnp.arange(n_sel, dtype=jnp.int32)[:, None]
    col = jnp.arange(n_kt * LANES, dtype=jnp.int32)[None, :]
    perm = ((j == blocks * (col // LANES) + col % LANES) & (col % LANES < blocks)).astype(BF16)
    key = jnp.arange(NSA_TK, dtype=jnp.int32)[:, None]
    u = jnp.arange(LANES, dtype=jnp.int32)[None, :]
    eblk = (u == key // SEL_LEN).astype(BF16)
    c = jnp.arange(n_cmp + CMP_PAD_FRONT + CMP_PAD_BACK, dtype=jnp.int32)[None, :] - CMP_PAD_FRONT
    jb = jnp.arange(n_sel, dtype=jnp.int32)[:, None]
    d = c - (SEL_LEN // CMP_STRIDE) * jb
    wfar = jnp.where((d == 0) | (d == 4), 1.0, jnp.where((d >= 1) & (d <= 3), 2.0, 0.0))
    wfar = jnp.where((c >= 0) & (c < n_cmp), wfar, 0.0).astype(BF16)
    return perm, eblk, wfar


def kernel(x, norm_mix_g, w_in, cmp_pe_k, cmp_pe_v, cmp_w1_k, cmp_w2_k, cmp_w1_v, cmp_w2_v,
           diff_lq1, diff_lk1, diff_lq2, diff_lk2, diff_head_g, w_up_nsa, w_up_diff, w_out,
           norm_ff_g, w_ff_in, w_ff_out, rel_bias_table, norm_final_g):
    B, S, D = x.shape
    T = B * S
    depth = w_in.shape[0]
    n_sel = S // SEL_LEN
    n_cmp = S // CMP_STRIDE
    qscale = HEAD_DIM ** -0.5 * LOG2E

    dc, ds, dw = _nsa_tables(rel_bias_table[:, :NSA_HEADS])
    dd = _diff_table(rel_bias_table[:, NSA_HEADS:])
    perm, eblk, wfar = _selection_constants(S)

    gn0 = COL_QD
    gn1 = gn0 + NSA_HEADS * 3
    cs = jnp.ones((1, N_PROJ), F32)
    cs = cs.at[:, COL_QN:COL_KC].set(qscale).at[:, COL_QD:COL_KD].set(qscale)

    x2 = x.reshape(T, D)
    for l in range(depth):
        w_main = jnp.concatenate([w_in[l][:, :gn0], w_in[l][:, gn1:]], axis=1).astype(BF16)
        wg = w_in[l][:, gn0:gn1].reshape(D, NSA_GROUPS, NSA_HPG * 3)
        wg = jnp.pad(wg, ((0, 0), (0, 0), (0, LANES - NSA_HPG * 3))).reshape(D, NSA_GROUPS * LANES)
        proj, gates = _in_proj(x2, norm_mix_g[l][None], w_main, cs, wg.astype(BF16))
        proj3 = proj.reshape(B, S, N_PROJ)
        gates3 = gates.reshape(B, S, NSA_GROUPS * LANES)

        halves = proj3[:, :, COL_KC:COL_KS].reshape(B, n_cmp, CMP_STRIDE, 2, NSA_GROUPS, HEAD_DIM)
        halves = halves.transpose(3, 0, 4, 1, 2, 5).reshape(
            2, B, NSA_GROUPS, n_cmp, CMP_STRIDE * HEAD_DIM)
        pe = jnp.stack([cmp_pe_k[l], cmp_pe_v[l]]).reshape(2, 1, CMP_LEN * HEAD_DIM)
        w1 = jnp.stack([cmp_w1_k[l], cmp_w1_v[l]]).astype(BF16)
        w2 = jnp.stack([cmp_w2_k[l], cmp_w2_v[l]]).astype(BF16)
        cmp_kv = _compress(halves, pe, w1, w2)
        cmp_pad = jnp.pad(cmp_kv, ((0, 0), (0, 0), (0, 0), (CMP_PAD_FRONT, CMP_PAD_BACK), (0, 0)))

        o_nsa = _nsa(proj3, gates3, cmp_pad, dc, ds, dw, perm, eblk, wfar, n_sel)

        lam_init = 0.8 - 0.6 * math.exp(-0.3 * l)
        o_d = _diff(proj3, dd, diff_lq1[l][None], diff_lk1[l][None], diff_lq2[l][None],
                    diff_lk2[l][None], diff_head_g[l][:, None, :], lam_init)

        mix = _merge(o_nsa.reshape(T, -1), o_d.reshape(T, -1),
                     w_up_nsa[l].astype(BF16), w_up_diff[l].astype(BF16), proj)
        x2 = _out_proj(mix, w_out[l].astype(BF16), x2)

        act = _ffn_in(x2, norm_ff_g[l][None], w_ff_in[l].astype(BF16))
        last = l == depth - 1
        x2 = _ffn_out(act, w_ff_out[l].astype(BF16), x2,
                      norm_final_g[None] if last else norm_ff_g[l][None], final_norm=last)
    return x2.reshape(B, S, D)
```

```python
import functools
import math

import jax
import jax.numpy as jnp
from jax import lax
from jax.experimental import pallas as pl
from jax.experimental.pallas import tpu as pltpu

D_MODEL = 2048
HEAD_DIM = 128
NSA_HEADS = 8
NSA_GROUPS = 2
NSA_HPG = NSA_HEADS // NSA_GROUPS
CMP_LEN = 32
CMP_STRIDE = 16
SEL_LEN = 64
SEL_TOPK = 16
WINDOW = 512
CMP_HIDDEN = 256
DIFF_HEADS = 4
DIFF_V_DIM = 2 * HEAD_DIM
REL_BUCKETS = 32
REL_MAX_EXACT = 16
REL_MAX_DIST = 128
D_FF = -(-8 * D_MODEL // (3 * 256)) * 256
EPS = 1e-6
NEG = -1e30
BIG = 1e30
LOG2E = math.log2(math.e)

F32 = jnp.float32
BF16 = jnp.bfloat16
LANES = 128

VMEM_LIMIT_BYTES = 56 * 1024 * 1024

COL_QN = 0
COL_KC = 1024
COL_KS = 1536
COL_VS = 1792
COL_KW = 2048
COL_VW = 2304
COL_QD = 2560
COL_KD = 3584
COL_VD = 4608
COL_GM = 5632
N_PROJ = 9728

NSA_TQ = 256
NSA_TK = 1024
NSA_ROWS = 32
CMP_NEAR = 32
CMP_WIDTHS = (256, 512, 768, 1024)
CMP_PAD_FRONT = 16
CMP_PAD_BACK = 112
DIFF_TQ = 512
DIFF_TK = 512
DIFF_ROWS = 64


def _dot(a, b):
    return jnp.dot(a, b, preferred_element_type=F32)


def _dot_nt(a, b):
    return lax.dot_general(a, b, (((1,), (1,)), ((), ())), preferred_element_type=F32)


def _params(*sem):
    return pltpu.CompilerParams(dimension_semantics=sem, vmem_limit_bytes=VMEM_LIMIT_BYTES)


def _norm_rows(x, g):
    ms = jnp.mean(x * x, axis=-1, keepdims=True)
    return x * lax.rsqrt(ms + EPS) * g


def _in_proj_kernel(x_ref, g_ref, w_ref, cs_ref, wg_ref, o_ref, og_ref, h_ref):
    @pl.when(pl.program_id(1) == 0)
    def _():
        hb = _norm_rows(x_ref[...], g_ref[...]).astype(BF16)
        h_ref[...] = hb
        og_ref[...] = jax.nn.sigmoid(_dot(hb, wg_ref[...]))

    o_ref[...] = (_dot(h_ref[...], w_ref[...]) * cs_ref[...]).astype(o_ref.dtype)


def _in_proj(x2, g, w, cs, wg, tm=512, tn=2432):
    T = x2.shape[0]
    n = w.shape[1]
    ng = wg.shape[1]
    return pl.pallas_call(
        _in_proj_kernel,
        grid=(T // tm, n // tn),
        in_specs=[
            pl.BlockSpec((tm, D_MODEL), lambda i, j: (i, 0)),
            pl.BlockSpec((1, D_MODEL), lambda i, j: (0, 0)),
            pl.BlockSpec((D_MODEL, tn), lambda i, j: (0, j)),
            pl.BlockSpec((1, tn), lambda i, j: (0, j)),
            pl.BlockSpec((D_MODEL, ng), lambda i, j: (0, 0)),
        ],
        out_specs=[
            pl.BlockSpec((tm, tn), lambda i, j: (i, j)),
            pl.BlockSpec((tm, ng), lambda i, j: (i, 0)),
        ],
        out_shape=[
            jax.ShapeDtypeStruct((T, n), BF16),
            jax.ShapeDtypeStruct((T, ng), F32),
        ],
        scratch_shapes=[pltpu.VMEM((tm, D_MODEL), BF16)],
        compiler_params=_params("parallel", "arbitrary"),
    )(x2, g, w, cs, wg)


def _gelu_tanh(x):
    return 0.5 * x * (1.0 + jnp.tanh(math.sqrt(2.0 / math.pi) * (x + 0.044715 * (x * x * x))))


def _compress_kernel(h_ref, pe_ref, w1_ref, w2_ref, o_ref):
    hv = h_ref[0, 0, 0]
    nc = hv.shape[0]
    half = CMP_STRIDE * HEAD_DIM
    ya = _dot(hv, w1_ref[0, :half, :])
    yb = _dot(hv, w1_ref[0, half:, :])
    yb = pltpu.roll(yb, nc - 1, 0)
    row = lax.broadcasted_iota(jnp.int32, yb.shape, 0)
    yb = jnp.where(row == nc - 1, 0.0, yb)
    pe8 = jnp.broadcast_to(pe_ref[0], (8, 2 * half)).astype(BF16)
    pec = _dot(pe8, w1_ref[0])[0:1]
    hid = _gelu_tanh(ya + yb + pec)
    o_ref[0, 0, 0] = _dot(hid.astype(BF16), w2_ref[0]).astype(o_ref.dtype)


def _compress(halves, pe, w1, w2):
    _, B, G, nc, hw = halves.shape
    return pl.pallas_call(
        _compress_kernel,
        grid=(2, B, G),
        in_specs=[
            pl.BlockSpec((1, 1, 1, nc, hw), lambda s, b, g: (s, b, g, 0, 0)),
            pl.BlockSpec((1, 1, 2 * hw), lambda s, b, g: (s, 0, 0)),
            pl.BlockSpec((1, 2 * hw, CMP_HIDDEN), lambda s, b, g: (s, 0, 0)),
            pl.BlockSpec((1, CMP_HIDDEN, HEAD_DIM), lambda s, b, g: (s, 0, 0)),
        ],
        out_specs=pl.BlockSpec((1, 1, 1, nc, HEAD_DIM), lambda s, b, g: (s, b, g, 0, 0)),
        out_shape=jax.ShapeDtypeStruct((2, B, G, nc, HEAD_DIM), BF16),
        compiler_params=_params("parallel", "parallel", "parallel"),
    )(halves, pe, w1, w2)


def _lanes(x, width):
    return jnp.concatenate([x] * (width // x.shape[1]), axis=1)


def _init_state(m_ref, l_ref, acc_ref):
    m_ref[...] = jnp.full(m_ref.shape, NEG, F32)
    l_ref[...] = jnp.zeros(l_ref.shape, F32)
    acc_ref[...] = jnp.zeros(acc_ref.shape, F32)


def _softmax_tile(slot, width, s_ref, p_ref, a_ref, m_ref, l_ref, chunk):
    n_rows = p_ref.shape[1]
    for r in range(0, n_rows, chunk):
        rows = slice(r, r + chunk)
        m_prev = m_ref[rows, :]
        m_new = jnp.maximum(m_prev, jnp.max(s_ref[slot, rows, :width], axis=-1, keepdims=True))
        a_ref[slot, rows, :] = jnp.exp2(m_prev - m_new)
        m_ref[rows, :] = m_new
    for r in range(0, n_rows, chunk):
        rows = slice(r, r + chunk)
        p = jnp.exp2(s_ref[slot, rows, :width] - _lanes(m_ref[rows, :], width))
        l_ref[rows, :] = (a_ref[slot, rows, :] * l_ref[rows, :]
                          + jnp.sum(p, axis=-1, keepdims=True))
        p_ref[slot, rows, :width] = p.astype(BF16)


def _value_tile(slot, width, v, p_ref, a_ref, acc_ref):
    acc_ref[...] = (acc_ref[...] * _lanes(a_ref[slot], acc_ref.shape[1])
                    + _dot(p_ref[slot, :, :width], v))


def _fold_tile(slot, width, v, s_ref, p_ref, a_ref, m_ref, l_ref, acc_ref, chunk):
    _softmax_tile(slot, width, s_ref, p_ref, a_ref, m_ref, l_ref, chunk)
    _value_tile(slot, width, v, p_ref, a_ref, acc_ref)


def _fold_tile_inplace(slot, width, v, s_ref, p_ref, m_ref, l_ref, acc_ref, chunk):
    n_rows = p_ref.shape[1]
    for r in range(0, n_rows, chunk):
        rows = slice(r, r + chunk)
        s = s_ref[slot, rows, :width]
        m_prev = m_ref[rows, :]
        m_new = jnp.maximum(m_prev, jnp.max(s, axis=-1, keepdims=True))
        alpha = jnp.exp2(m_prev - m_new)
        p = jnp.exp2(s - _lanes(m_new, width))
        l_ref[rows, :] = alpha * l_ref[rows, :] + jnp.sum(p, axis=-1, keepdims=True)
        m_ref[rows, :] = m_new
        acc_ref[rows, :] = acc_ref[rows, :] * _lanes(alpha, acc_ref.shape[1])
        p_ref[0, rows, :width] = p.astype(BF16)
    acc_ref[...] += _dot(p_ref[0, :, :width], v)


def _finish(l_ref, acc_ref):
    return acc_ref[...] / _lanes(l_ref[...], acc_ref.shape[1])


def _split_dot_nt(w, a):
    hi = a.astype(BF16)
    lo = (a - hi.astype(F32)).astype(BF16)
    return _dot_nt(w, hi) + _dot_nt(w, lo)


def _nsa_kernel(q_ref, gate_ref, kc_ref, vc_ref, ks_ref, vs_ref,
                kw0_ref, kw1_ref, kw2_ref, vw0_ref, vw1_ref, vw2_ref,
                dc_ref, ds_ref, dw_ref, perm_ref, eblk_ref, wfar_ref, o_ref,
                s_ref, p_ref, a_ref, m_ref, l_ref, acc_ref, imp_ref, sn_ref, pn_ref, selt_ref,
                sel_ref, q4_ref, pick_ref, oc_ref,
                *, n_sel, top_n):
    qt = pl.program_id(2)
    TQ, P = NSA_TQ, NSA_HPG
    R = P * TQ
    blocks_per_tile = NSA_TQ // SEL_LEN
    stats = (s_ref, p_ref, a_ref, m_ref, l_ref)
    state = stats + (acc_ref,)
    qblk = q_ref[0]
    q4_ref[...] = jnp.concatenate(
        [qblk[:, p * HEAD_DIM:(p + 1) * HEAD_DIM] for p in range(P)], axis=0)

    ncp = kc_ref.shape[3]
    cmp_per_tile = NSA_TQ // CMP_STRIDE
    far_end = cmp_per_tile * qt
    c0 = pl.multiple_of(far_end, cmp_per_tile)

    def compressed(w):
        lane = lax.broadcasted_iota(jnp.int32, (1, w), 1)
        far_bias = jnp.where((lane >= CMP_PAD_FRONT) & (lane < far_end), 0.0, NEG)
        kn = kc_ref[0, 0, 0, pl.ds(c0, CMP_NEAR), :]
        vn = vc_ref[0, 0, 0, pl.ds(c0, CMP_NEAR), :]
        s_ref[0, :, :w] = _dot_nt(q4_ref[...], kc_ref[0, 0, 0, :w, :])
        sn_ref[...] = _dot_nt(q4_ref[...], kn) + dc_ref[0, 0]
        imp_ref[:, :w] = jnp.zeros((TQ, w), F32)

        def cmp_head(p, carry):
            for c in range(TQ // NSA_ROWS):
                rows = pl.ds(pl.multiple_of(p * TQ + c * NSA_ROWS, NSA_ROWS), NSA_ROWS)
                irows = slice(c * NSA_ROWS, (c + 1) * NSA_ROWS)
                sf = s_ref[0, rows, :w] + far_bias
                sn = sn_ref[rows, :]
                m = jnp.maximum(jnp.max(sf, axis=-1, keepdims=True),
                                jnp.max(sn, axis=-1, keepdims=True))
                ef = jnp.exp2(sf - m)
                en = jnp.exp2(sn - m)
                l = jnp.sum(ef, axis=-1, keepdims=True) + jnp.sum(en, axis=-1, keepdims=True)
                inv = jnp.where(m > 0.5 * NEG, 1.0 / l, 0.0)
                pf = ef * inv
                p_ref[0, rows, :w] = pf.astype(BF16)
                pn_ref[rows, :] = en * inv
                imp_ref[irows, :w] += pf
            return carry

        lax.fori_loop(0, P, cmp_head, 0)
        p_near = pn_ref[...]
        oc_ref[...] = (_dot(p_ref[0, :, :w], vc_ref[0, 0, 0, :w, :])
                       + _dot(p_near.astype(BF16), vn))

        imp_near = jnp.sum(p_near.reshape(P, TQ, CMP_NEAR), axis=0)
        jj = lax.broadcasted_iota(jnp.int32, (n_sel, CMP_NEAR), 0)
        jn = lax.broadcasted_iota(jnp.int32, (n_sel, CMP_NEAR), 1)
        d = (far_end - CMP_PAD_FRONT + jn) - (SEL_LEN // CMP_STRIDE) * jj
        w_near = jnp.where((d == 0) | (d == 4), 1.0,
                           jnp.where((d >= 1) & (d <= 3), 2.0, 0.0)).astype(BF16)
        pick_ref[...] = (_split_dot_nt(wfar_ref[:, :w], imp_ref[:, :w])
                         + _split_dot_nt(w_near, imp_near))

    max_far_end = (ks_ref.shape[1] // TQ - 1) * cmp_per_tile
    candidates = sorted(set(CMP_WIDTHS) | {ncp})
    widths = [w for w in candidates if w < max_far_end]
    widths.append(min(w for w in candidates if w >= max_far_end))
    lo = -1
    for w in widths:
        pl.when((far_end > lo) & (far_end <= w))(functools.partial(compressed, w))
        lo = w
    o_c = oc_ref[...]
    p_slc = pick_ref[...]

    bj = lax.broadcasted_iota(jnp.int32, (n_sel, TQ), 0)
    ti = lax.broadcasted_iota(jnp.int32, (n_sel, TQ), 1)
    jt = blocks_per_tile * qt + ti // SEL_LEN
    forced = (bj == 0) | (bj == jt) | (bj == jt - 1)
    pick_ref[...] = jnp.where(forced, -jnp.inf, jnp.where(bj > jt, NEG, p_slc))
    idx = lax.broadcasted_iota(jnp.int32, (n_sel, LANES), 0).astype(F32)

    def pick(_, sc):
        mx = jnp.max(sc, axis=0, keepdims=True)
        first = jnp.min(jnp.where(sc == mx, idx, float(n_sel)), axis=0, keepdims=True)
        return jnp.where(idx == first, -jnp.inf, sc)

    for h in range(0, TQ, LANES):
        pick_ref[:, h:h + LANES] = lax.fori_loop(0, top_n - 3, pick, pick_ref[:, h:h + LANES])
    selneg_t = jnp.where(pick_ref[...] == -jnp.inf, 0.0, NEG)
    sel_ref[...] = selneg_t.T
    selneg = sel_ref[...]
    bj = lax.broadcasted_iota(jnp.int32, (TQ, n_sel), 1)

    nb0 = jnp.maximum(blocks_per_tile * (qt - 1), 0)
    pj = lax.broadcasted_iota(jnp.int32, (n_sel, LANES), 0)
    pu = lax.broadcasted_iota(jnp.int32, (n_sel, LANES), 1)
    perm_near = ((pj == nb0 + pu) & (pu < 2 * blocks_per_tile)).astype(BF16)
    sel_near = _dot(selneg.astype(BF16), perm_near).astype(BF16)
    sel_far = jnp.where(bj < blocks_per_tile * (qt - 1), selneg, NEG).astype(BF16)
    sel_all = _dot(sel_far, perm_ref[...]).astype(BF16)
    n_kt = selt_ref.shape[0] - 1
    for kt in range(n_kt):
        selt_ref[kt] = sel_all[:, kt * LANES:(kt + 1) * LANES]
    col = lax.broadcasted_iota(jnp.int32, (TQ, LANES), 1)
    selt_ref[n_kt] = jnp.where(col < NSA_TK // SEL_LEN, NEG, 0.0).astype(BF16)

    _init_state(m_ref, l_ref, acc_ref)
    near_keys = 2 * NSA_TQ
    ns = pl.multiple_of(nb0 * SEL_LEN, NSA_TQ)
    lhs = jnp.concatenate([q4_ref[...], jnp.concatenate([sel_near] * P, axis=0)], axis=1)
    rhs = jnp.concatenate([ks_ref[0, pl.ds(ns, near_keys), :], eblk_ref[:near_keys, :]], axis=1)
    s_ref[0, :, :near_keys] = _dot_nt(lhs, rhs) + ds_ref[0, 0].astype(F32)

    def far_scores(kt, sel_idx, slot):
        k0 = pl.multiple_of(kt * NSA_TK, NSA_TK)
        lhs_t = jnp.concatenate(
            [q4_ref[...], jnp.concatenate([selt_ref[sel_idx]] * P, axis=0)], axis=1)
        rhs_t = jnp.concatenate([ks_ref[0, pl.ds(k0, NSA_TK), :], eblk_ref[...]], axis=1)
        s_ref[slot, :, :NSA_TK] = _dot_nt(lhs_t, rhs_t)

    def far_softmax(slot):
        _softmax_tile(slot, NSA_TK, *stats, NSA_ROWS)

    def far_values(kt, slot):
        k0 = pl.multiple_of(kt * NSA_TK, NSA_TK)
        _value_tile(slot, NSA_TK, vs_ref[0, pl.ds(k0, NSA_TK), :], p_ref, a_ref, acc_ref)

    n_far = (qt + 2) // 4
    n_pairs = n_far // 2
    odd = jnp.maximum(n_far - 1, 0)
    odd_sel = jnp.where(n_far % 2 == 1, odd, selt_ref.shape[0] - 1)
    last_pair_tile = jnp.maximum(n_far - 2, 0)
    far_scores(odd, odd_sel, 1)
    _fold_tile(0, near_keys, vs_ref[0, pl.ds(ns, near_keys), :], *state, NSA_ROWS)
    far_scores(0, 0, 0)
    far_softmax(1)

    def far_pair(i, carry):
        a = 2 * i
        far_values(jnp.where(i == 0, odd, a - 1), 1)
        far_scores(a + 1, a + 1, 1)
        far_softmax(0)
        nxt = jnp.minimum(a + 2, last_pair_tile)
        far_values(a, 0)
        far_scores(nxt, nxt, 0)
        far_softmax(1)
        return carry

    lax.fori_loop(0, n_pairs, far_pair, 0)
    far_values(jnp.where(n_pairs == 0, odd, 2 * n_pairs - 1), 1)
    o_s = _finish(l_ref, acc_ref)

    _init_state(m_ref, l_ref, acc_ref)
    kw = jnp.concatenate([kw0_ref[0], kw1_ref[0], kw2_ref[0]], axis=0)
    vw = jnp.concatenate([vw0_ref[0], vw1_ref[0], vw2_ref[0]], axis=0)
    s_ref[0, :, :WINDOW + NSA_TQ] = _dot_nt(q4_ref[...], kw) + dw_ref[0, 0].astype(F32)
    _fold_tile(0, WINDOW + NSA_TQ, vw, *state, NSA_ROWS)
    o_w = _finish(l_ref, acc_ref)

    gate = gate_ref[0]
    outs = []
    for p in range(P):
        rows = slice(p * TQ, (p + 1) * TQ)
        gc = gate[:, 3 * p:3 * p + 1]
        gs = gate[:, 3 * p + 1:3 * p + 2]
        gw = gate[:, 3 * p + 2:3 * p + 3]
        outs.append(gc * o_c[rows] + gs * o_s[rows] + gw * o_w[rows])
    o_ref[0] = jnp.concatenate(outs, axis=1).astype(o_ref.dtype)


def _nsa(proj3, gates3, cmp_pad, dc, ds, dw, perm, eblk, wfar, n_sel):
    B, S, _ = proj3.shape
    G = NSA_GROUPS
    ncp = cmp_pad.shape[3]
    top_n = min(SEL_TOPK, n_sel)
    assert top_n > 3, "the selection loop assumes the three forced blocks fit in the top-n"
    qw = NSA_HPG * HEAD_DIM
    R = NSA_HPG * NSA_TQ
    wide = max(NSA_TK, ncp, WINDOW + NSA_TQ)
    wblk = lambda col, back: pl.BlockSpec(
        (1, NSA_TQ, HEAD_DIM),
        lambda b, g, t, col=col, back=back: (b, jnp.maximum(t - back, 0), col // HEAD_DIM + g))
    kernel = functools.partial(_nsa_kernel, n_sel=n_sel, top_n=top_n)
    return pl.pallas_call(
        kernel,
        grid=(B, G, S // NSA_TQ),
        in_specs=[
            pl.BlockSpec((1, NSA_TQ, qw), lambda b, g, t: (b, t, g)),
            pl.BlockSpec((1, NSA_TQ, LANES), lambda b, g, t: (b, t, g)),
            pl.BlockSpec((1, 1, 1, ncp, HEAD_DIM), lambda b, g, t: (0, b, g, 0, 0)),
            pl.BlockSpec((1, 1, 1, ncp, HEAD_DIM), lambda b, g, t: (1, b, g, 0, 0)),
            pl.BlockSpec((1, S, HEAD_DIM), lambda b, g, t: (b, 0, COL_KS // HEAD_DIM + g)),
            pl.BlockSpec((1, S, HEAD_DIM), lambda b, g, t: (b, 0, COL_VS // HEAD_DIM + g)),
            wblk(COL_KW, 2), wblk(COL_KW, 1), wblk(COL_KW, 0),
            wblk(COL_VW, 2), wblk(COL_VW, 1), wblk(COL_VW, 0),
            pl.BlockSpec((1, 1) + dc.shape[2:], lambda b, g, t: (jnp.minimum(t, 1), g, 0, 0)),
            pl.BlockSpec((1, 1) + ds.shape[2:], lambda b, g, t: (jnp.minimum(t, 1), g, 0, 0)),
            pl.BlockSpec((1, 1) + dw.shape[2:], lambda b, g, t: (jnp.minimum(t, 2), g, 0, 0)),
            pl.BlockSpec(perm.shape, lambda b, g, t: (0, 0)),
            pl.BlockSpec(eblk.shape, lambda b, g, t: (0, 0)),
            pl.BlockSpec(wfar.shape, lambda b, g, t: (0, 0)),
        ],
        out_specs=pl.BlockSpec((1, NSA_TQ, qw), lambda b, g, t: (b, t, g)),
        out_shape=jax.ShapeDtypeStruct((B, S, NSA_HEADS * HEAD_DIM), BF16),
        scratch_shapes=[
            pltpu.VMEM((2, R, wide), F32),
            pltpu.VMEM((2, R, wide), BF16),
            pltpu.VMEM((2, R, LANES), F32),
            pltpu.VMEM((R, LANES), F32),
            pltpu.VMEM((R, LANES), F32),
            pltpu.VMEM((R, HEAD_DIM), F32),
            pltpu.VMEM((NSA_TQ, ncp), F32),
            pltpu.VMEM((R, CMP_NEAR), F32),
            pltpu.VMEM((R, CMP_NEAR), F32),
            pltpu.VMEM((S // NSA_TK + 1, NSA_TQ, LANES), BF16),
            pltpu.VMEM((NSA_TQ, n_sel), F32),
            pltpu.VMEM((R, HEAD_DIM), BF16),
            pltpu.VMEM((n_sel, NSA_TQ), F32),
            pltpu.VMEM((R, HEAD_DIM), F32),
        ],
        compiler_params=_params("parallel", "parallel", "arbitrary"),
    )(proj3, gates3, cmp_pad, cmp_pad, proj3, proj3,
      proj3, proj3, proj3, proj3, proj3, proj3, dc, ds, dw, perm, eblk, wfar)


def _diff_kernel(q_ref, k_ref, v_ref, dd_ref, lq1_ref, lk1_ref, lq2_ref, lk2_ref, hg_ref,
                 o_ref, s_ref, p_ref, m_ref, l_ref, acc_ref, *, lam_init):
    qt = pl.program_id(2)
    TQ = DIFF_TQ
    q = q_ref[0]
    zero = jnp.zeros((TQ, HEAD_DIM), BF16)
    lhs = jnp.concatenate([
        jnp.concatenate([q[:, :HEAD_DIM], zero], axis=1),
        jnp.concatenate([zero, q[:, HEAD_DIM:]], axis=1)], axis=0)
    lam = (jnp.exp(jnp.sum(lq1_ref[...] * lk1_ref[...], axis=-1, keepdims=True))
           - jnp.exp(jnp.sum(lq2_ref[...] * lk2_ref[...], axis=-1, keepdims=True)) + lam_init)

    def scores(kt, slot, bias):
        k0 = pl.multiple_of(kt * DIFF_TK, DIFF_TK)
        sc = _dot_nt(lhs, k_ref[0, pl.ds(k0, DIFF_TK), :])
        if bias is not None:
            sc = (sc.reshape(2, TQ, DIFF_TK) + bias[None]).reshape(2 * TQ, DIFF_TK)
        s_ref[slot] = sc

    def fold(kt, slot):
        k0 = pl.multiple_of(kt * DIFF_TK, DIFF_TK)
        _fold_tile_inplace(slot, DIFF_TK, v_ref[0, pl.ds(k0, DIFF_TK), :],
                           s_ref, p_ref, m_ref, l_ref, acc_ref, DIFF_ROWS)

    _init_state(m_ref, l_ref, acc_ref)
    n_far = jnp.maximum(qt - 1, 0)
    prev = jnp.maximum(qt - 1, 0)
    odd = jnp.maximum(n_far - 1, 0)
    last_pair_tile = jnp.maximum(n_far - 2, 0)
    scores(qt, 0, dd_ref[0, :, DIFF_TK:].astype(F32))
    fold(qt, 0)
    scores(prev, 0, jnp.where(qt >= 1, dd_ref[0, :, :DIFF_TK].astype(F32), NEG))
    fold(prev, 0)
    scores(odd, 0, jnp.where(n_far % 2 == 1, jnp.zeros((TQ, DIFF_TK), F32), NEG))
    scores(0, 1, None)
    fold(odd, 0)

    def far_pair(i, carry):
        a = 2 * i
        scores(a + 1, 0, None)
        fold(a, 1)
        scores(jnp.minimum(a + 2, last_pair_tile), 1, None)
        fold(a + 1, 0)
        return carry

    lax.fori_loop(0, n_far // 2, far_pair, 0)
    a = _finish(l_ref, acc_ref)
    o = a[:TQ] - lam * a[TQ:]
    y = _norm_rows(o, hg_ref[0]) * (1.0 - lam_init)
    o_ref[0] = y.astype(o_ref.dtype)


def _diff(proj3, dd, lq1, lk1, lq2, lk2, hg, lam_init):
    B, S, _ = proj3.shape
    H = DIFF_HEADS
    w = 2 * HEAD_DIM
    R = 2 * DIFF_TQ
    vec = pl.BlockSpec((1, HEAD_DIM), lambda b, h, t: (0, 0))
    kernel = functools.partial(_diff_kernel, lam_init=lam_init)
    return pl.pallas_call(
        kernel,
        grid=(B, H, S // DIFF_TQ),
        in_specs=[
            pl.BlockSpec((1, DIFF_TQ, w), lambda b, h, t: (b, t, COL_QD // w + h)),
            pl.BlockSpec((1, S, w), lambda b, h, t: (b, 0, COL_KD // w + h)),
            pl.BlockSpec((1, S, w), lambda b, h, t: (b, 0, COL_VD // w + h)),
            pl.BlockSpec((1,) + dd.shape[1:], lambda b, h, t: (h, 0, 0)),
            vec, vec, vec, vec,
            pl.BlockSpec((1, 1, w), lambda b, h, t: (h, 0, 0)),
        ],
        out_specs=pl.BlockSpec((1, DIFF_TQ, w), lambda b, h, t: (b, t, h)),
        out_shape=jax.ShapeDtypeStruct((B, S, H * w), BF16),
        scratch_shapes=[
            pltpu.VMEM((2, R, DIFF_TK), F32),
            pltpu.VMEM((1, R, DIFF_TK), BF16),
            pltpu.VMEM((R, LANES), F32),
            pltpu.VMEM((R, LANES), F32),
            pltpu.VMEM((R, DIFF_V_DIM), F32),
        ],
        compiler_params=_params("parallel", "parallel", "arbitrary"),
    )(proj3, proj3, proj3, dd, lq1, lk1, lq2, lk2, hg)


def _merge_kernel(on_ref, od_ref, wn_ref, wd_ref, ga_ref, gb_ref, o_ref):
    u1 = _dot(on_ref[...], wn_ref[...])
    u2 = _dot(od_ref[...], wd_ref[...])
    ga = jax.nn.sigmoid(ga_ref[...].astype(F32))
    gb = jax.nn.sigmoid(gb_ref[...].astype(F32))
    o_ref[...] = (ga * u1 + gb * u2).astype(o_ref.dtype)


def _merge(o_nsa, o_d, wn, wd, proj, tm=1024, tn=512):
    T, kn = o_nsa.shape
    kd = o_d.shape[1]
    return pl.pallas_call(
        _merge_kernel,
        grid=(T // tm, D_MODEL // tn),
        in_specs=[
            pl.BlockSpec((tm, kn), lambda i, j: (i, 0)),
            pl.BlockSpec((tm, kd), lambda i, j: (i, 0)),
            pl.BlockSpec((kn, tn), lambda i, j: (0, j)),
            pl.BlockSpec((kd, tn), lambda i, j: (0, j)),
            pl.BlockSpec((tm, tn), lambda i, j: (i, COL_GM // tn + j)),
            pl.BlockSpec((tm, tn), lambda i, j: (i, (COL_GM + D_MODEL) // tn + j)),
        ],
        out_specs=pl.BlockSpec((tm, tn), lambda i, j: (i, j)),
        out_shape=jax.ShapeDtypeStruct((T, D_MODEL), BF16),
        compiler_params=_params("parallel", "arbitrary"),
    )(o_nsa, o_d, wn, wd, proj, proj)


def _out_proj_kernel(a_ref, w_ref, x_ref, o_ref):
    o_ref[...] = x_ref[...] + _dot(a_ref[...], w_ref[...])


def _out_proj(a, w, x2, tm=512, tn=2048):
    T, k = a.shape
    return pl.pallas_call(
        _out_proj_kernel,
        grid=(T // tm, D_MODEL // tn),
        in_specs=[
            pl.BlockSpec((tm, k), lambda i, j: (i, 0)),
            pl.BlockSpec((k, tn), lambda i, j: (0, j)),
            pl.BlockSpec((tm, tn), lambda i, j: (i, j)),
        ],
        out_specs=pl.BlockSpec((tm, tn), lambda i, j: (i, j)),
        out_shape=jax.ShapeDtypeStruct((T, D_MODEL), F32),
        compiler_params=_params("parallel", "arbitrary"),
    )(a, w, x2)


def _ffn_in_kernel(x_ref, g_ref, wa_ref, wb_ref, o_ref, h_ref):
    @pl.when(pl.program_id(1) == 0)
    def _():
        h_ref[...] = _norm_rows(x_ref[...], g_ref[...]).astype(BF16)

    h = h_ref[...]
    a = _dot(h, wa_ref[...])
    b = _dot(h, wb_ref[...])
    o_ref[...] = (a * jax.nn.sigmoid(a) * b).astype(o_ref.dtype)


def _ffn_in(x2, g, w, tm=1024, tn=512):
    T = x2.shape[0]
    nb = D_FF // tn
    return pl.pallas_call(
        _ffn_in_kernel,
        grid=(T // tm, nb),
        in_specs=[
            pl.BlockSpec((tm, D_MODEL), lambda i, j: (i, 0)),
            pl.BlockSpec((1, D_MODEL), lambda i, j: (0, 0)),
            pl.BlockSpec((D_MODEL, tn), lambda i, j: (0, j)),
            pl.BlockSpec((D_MODEL, tn), lambda i, j: (0, j + nb)),
        ],
        out_specs=pl.BlockSpec((tm, tn), lambda i, j: (i, j)),
        out_shape=jax.ShapeDtypeStruct((T, D_FF), BF16),
        scratch_shapes=[pltpu.VMEM((tm, D_MODEL), BF16)],
        compiler_params=_params("parallel", "arbitrary"),
    )(x2, g, w, w)


def _ffn_out_kernel(a_ref, w_ref, x_ref, g_ref, o_ref, acc_ref, *, final_norm):
    k = pl.program_id(1)

    @pl.when(k == 0)
    def _():
        acc_ref[...] = x_ref[...]

    acc_ref[...] += _dot(a_ref[...], w_ref[...])

    @pl.when(k == pl.num_programs(1) - 1)
    def _():
        y = acc_ref[...]
        o_ref[...] = _norm_rows(y, g_ref[...]) if final_norm else y


def _ffn_out(a, w, x2, g, final_norm, tm=512, tk=2816):
    T = a.shape[0]
    kernel = functools.partial(_ffn_out_kernel, final_norm=final_norm)
    return pl.pallas_call(
        kernel,
        grid=(T // tm, D_FF // tk),
        in_specs=[
            pl.BlockSpec((tm, tk), lambda i, k: (i, k)),
            pl.BlockSpec((tk, D_MODEL), lambda i, k: (k, 0)),
            pl.BlockSpec((tm, D_MODEL), lambda i, k: (i, 0)),
            pl.BlockSpec((1, D_MODEL), lambda i, k: (0, 0)),
        ],
        out_specs=pl.BlockSpec((tm, D_MODEL), lambda i, k: (i, 0)),
        out_shape=jax.ShapeDtypeStruct((T, D_MODEL), F32),
        scratch_shapes=[pltpu.VMEM((tm, D_MODEL), F32)],
        compiler_params=_params("parallel", "arbitrary"),
    )(a, w, x2, g)


def _t5_bucket(rel):
    n = jnp.maximum(rel, 0)
    nf = jnp.maximum(n, 1).astype(F32)
    large = REL_MAX_EXACT + (jnp.log(nf / REL_MAX_EXACT) / math.log(REL_MAX_DIST / REL_MAX_EXACT)
                             * (REL_BUCKETS - REL_MAX_EXACT)).astype(jnp.int32)
    large = jnp.minimum(large, REL_BUCKETS - 1)
    return jnp.where(n < REL_MAX_EXACT, n, large)


def _bias_of_rel(table, rel, valid):
    shifted = (table - table[REL_BUCKETS - 1:REL_BUCKETS]) * LOG2E
    vals = jnp.moveaxis(shifted[_t5_bucket(rel)], -1, 0)
    return jnp.where(valid[None], vals, NEG)


def _toeplitz_bias(table, n_i, n_j, offset, max_rel=None):
    rel = jnp.arange(n_i + n_j - 1, dtype=jnp.int32) - (n_j - 1) + offset
    valid = rel >= 0 if max_rel is None else (rel >= 0) & (rel < max_rel)
    v = _bias_of_rel(table, rel, valid)
    length = v.shape[-1]
    w = jnp.concatenate([v[:, n_j - 1::-1], v[:, :n_j - 1:-1]], axis=1)
    rows = jnp.tile(w, (1, n_i))[:, :n_i * (length - 1)].reshape(-1, n_i, length - 1)
    return rows[:, :, :n_j]


def _nsa_tables(table_nsa):
    TQ = NSA_TQ

    def stack(t):
        return t.reshape(NSA_GROUPS, NSA_HPG * TQ, t.shape[-1])

    i = jnp.arange(TQ, dtype=jnp.int32)[:, None]
    jc = jnp.arange(CMP_NEAR, dtype=jnp.int32)[None, :]
    rel_c = i - CMP_STRIDE * (jc - CMP_PAD_FRONT) - (CMP_LEN - 1)
    dc = jnp.stack([
        stack(_bias_of_rel(table_nsa, rel_c, (rel_c >= 0) & (jc >= CMP_PAD_FRONT))),
        stack(_bias_of_rel(table_nsa, rel_c, rel_c >= 0)),
    ])
    ds = jnp.stack([
        stack(_toeplitz_bias(table_nsa, TQ, 2 * TQ, 0)),
        stack(_toeplitz_bias(table_nsa, TQ, 2 * TQ, TQ)),
    ]).astype(BF16)
    win = stack(_toeplitz_bias(table_nsa, TQ, WINDOW + TQ, WINDOW, WINDOW))
    jw = jnp.arange(WINDOW + TQ, dtype=jnp.int32)[None, None, :]
    dw = jnp.stack([jnp.where(jw >= WINDOW - c * TQ, win, NEG) for c in range(3)]).astype(BF16)
    return dc, ds, dw


def _diff_table(table_diff):
    return _toeplitz_bias(table_diff, DIFF_TQ, 2 * DIFF_TK, DIFF_TK).astype(BF16)


def _selection_constants(S):
    n_sel = S // SEL_LEN
    n_cmp = S // CMP_STRIDE
    n_kt = S // NSA_TK
    blocks = NSA_TK // SEL_LEN
    j = jnp.arange(n_sel, dtype=jnp.int32)[:, None]
    col = jnp.arange(n_kt * LANES, dtype=jnp.int32)[None, :]
    perm = ((j == blocks * (col // LANES) + col % LANES) & (col % LANES < blocks)).astype(BF16)
    key = jnp.arange(NSA_TK, dtype=jnp.int32)[:, None]
    u = jnp.arange(LANES, dtype=jnp.int32)[None, :]
    eblk = (u == key // SEL_LEN).astype(BF16)
    c = jnp.arange(n_cmp + CMP_PAD_FRONT + CMP_PAD_BACK, dtype=jnp.int32)[None, :] - CMP_PAD_FRONT
    jb = jnp.arange(n_sel, dtype=jnp.int32)[:, None]
    d = c - (SEL_LEN // CMP_STRIDE) * jb
    wfar = jnp.where((d == 0) | (d == 4), 1.0, jnp.where((d >= 1) & (d <= 3), 2.0, 0.0))
    wfar = jnp.where((c >= 0) & (c < n_cmp), wfar, 0.0).astype(BF16)
    return perm, eblk, wfar


def kernel(x, norm_mix_g, w_in, cmp_pe_k, cmp_pe_v, cmp_w1_k, cmp_w2_k, cmp_w1_v, cmp_w2_v,
           diff_lq1, diff_lk1, diff_lq2, diff_lk2, diff_head_g, w_up_nsa, w_up_diff, w_out,
           norm_ff_g, w_ff_in, w_ff_out, rel_bias_table, norm_final_g):
    B, S, D = x.shape
    T = B * S
    depth = w_in.shape[0]
    n_sel = S // SEL_LEN
    n_cmp = S // CMP_STRIDE
    qscale = HEAD_DIM ** -0.5 * LOG2E

    dc, ds, dw = _nsa_tables(rel_bias_table[:, :NSA_HEADS])
    dd = _diff_table(rel_bias_table[:, NSA_HEADS:])
    perm, eblk, wfar = _selection_constants(S)

    gn0 = COL_QD
    gn1 = gn0 + NSA_HEADS * 3
    cs = jnp.ones((1, N_PROJ), F32)
    cs = cs.at[:, COL_QN:COL_KC].set(qscale).at[:, COL_QD:COL_KD].set(qscale)

    x2 = x.reshape(T, D)
    for l in range(depth):
        w_main = jnp.concatenate([w_in[l][:, :gn0], w_in[l][:, gn1:]], axis=1).astype(BF16)
        wg = w_in[l][:, gn0:gn1].reshape(D, NSA_GROUPS, NSA_HPG * 3)
        wg = jnp.pad(wg, ((0, 0), (0, 0), (0, LANES - NSA_HPG * 3))).reshape(D, NSA_GROUPS * LANES)
        proj, gates = _in_proj(x2, norm_mix_g[l][None], w_main, cs, wg.astype(BF16))
        proj3 = proj.reshape(B, S, N_PROJ)
        gates3 = gates.reshape(B, S, NSA_GROUPS * LANES)

        halves = proj3[:, :, COL_KC:COL_KS].reshape(B, n_cmp, CMP_STRIDE, 2, NSA_GROUPS, HEAD_DIM)
        halves = halves.transpose(3, 0, 4, 1, 2, 5).reshape(
            2, B, NSA_GROUPS, n_cmp, CMP_STRIDE * HEAD_DIM)
        pe = jnp.stack([cmp_pe_k[l], cmp_pe_v[l]]).reshape(2, 1, CMP_LEN * HEAD_DIM)
        w1 = jnp.stack([cmp_w1_k[l], cmp_w1_v[l]]).astype(BF16)
        w2 = jnp.stack([cmp_w2_k[l], cmp_w2_v[l]]).astype(BF16)
        cmp_kv = _compress(halves, pe, w1, w2)
        cmp_pad = jnp.pad(cmp_kv, ((0, 0), (0, 0), (0, 0), (CMP_PAD_FRONT, CMP_PAD_BACK), (0, 0)))

        o_nsa = _nsa(proj3, gates3, cmp_pad, dc, ds, dw, perm, eblk, wfar, n_sel)

        lam_init = 0.8 - 0.6 * math.exp(-0.3 * l)
        o_d = _diff(proj3, dd, diff_lq1[l][None], diff_lk1[l][None], diff_lq2[l][None],
                    diff_lk2[l][None], diff_head_g[l][:, None, :], lam_init)

        mix = _merge(o_nsa.reshape(T, -1), o_d.reshape(T, -1),
                     w_up_nsa[l].astype(BF16), w_up_diff[l].astype(BF16), proj)
        x2 = _out_proj(mix, w_out[l].astype(BF16), x2)

        act = _ffn_in(x2, norm_ff_g[l][None], w_ff_in[l].astype(BF16))
        last = l == depth - 1
        x2 = _ffn_out(act, w_ff_out[l].astype(BF16), x2,
                      norm_final_g[None] if last else norm_ff_g[l][None], final_norm=last)
    return x2.reshape(B, S, D)
```

```python
import functools
import math

import jax
import jax.numpy as jnp
from jax import lax
from jax.experimental import pallas as pl
from jax.experimental.pallas import tpu as pltpu

D_MODEL = 2048
HEAD_DIM = 128
NSA_HEADS = 8
NSA_GROUPS = 2
NSA_HPG = NSA_HEADS // NSA_GROUPS
CMP_LEN = 32
CMP_STRIDE = 16
SEL_LEN = 64
SEL_TOPK = 16
WINDOW = 512
CMP_HIDDEN = 256
DIFF_HEADS = 4
DIFF_V_DIM = 2 * HEAD_DIM
REL_BUCKETS = 32
REL_MAX_EXACT = 16
REL_MAX_DIST = 128
D_FF = -(-8 * D_MODEL // (3 * 256)) * 256
EPS = 1e-6
NEG = -1e30
BIG = 1e30
LOG2E = math.log2(math.e)

F32 = jnp.float32
BF16 = jnp.bfloat16
LANES = 128

VMEM_LIMIT_BYTES = 56 * 1024 * 1024

COL_QN = 0
COL_KC = 1024
COL_KS = 1536
COL_VS = 1792
COL_KW = 2048
COL_VW = 2304
COL_QD = 2560
COL_KD = 3584
COL_VD = 4608
COL_GM = 5632
N_PROJ = 9728

NSA_TQ = 256
NSA_TK = 1024
NSA_ROWS = 32
CMP_NEAR = 32
CMP_WIDTHS = (256, 512, 768, 1024)
CMP_PAD_FRONT = 16
CMP_PAD_BACK = 112
DIFF_TQ = 512
DIFF_TK = 512
DIFF_ROWS = 64


def _dot(a, b):
    return jnp.dot(a, b, preferred_element_type=F32)


def _dot_nt(a, b):
    return lax.dot_general(a, b, (((1,), (1,)), ((), ())), preferred_element_type=F32)


def _params(*sem):
    return pltpu.CompilerParams(dimension_semantics=sem, vmem_limit_bytes=VMEM_LIMIT_BYTES)


def _norm_rows(x, g):
    ms = jnp.mean(x * x, axis=-1, keepdims=True)
    return x * lax.rsqrt(ms + EPS) * g


def _in_proj_kernel(x_ref, g_ref, w_ref, cs_ref, wg_ref, o_ref, og_ref, h_ref):
    @pl.when(pl.program_id(1) == 0)
    def _():
        hb = _norm_rows(x_ref[...], g_ref[...]).astype(BF16)
        h_ref[...] = hb
        og_ref[...] = jax.nn.sigmoid(_dot(hb, wg_ref[...]))

    o_ref[...] = (_dot(h_ref[...], w_ref[...]) * cs_ref[...]).astype(o_ref.dtype)


def _in_proj(x2, g, w, cs, wg, tm=512, tn=2432):
    T = x2.shape[0]
    n = w.shape[1]
    ng = wg.shape[1]
    return pl.pallas_call(
        _in_proj_kernel,
        grid=(T // tm, n // tn),
        in_specs=[
            pl.BlockSpec((tm, D_MODEL), lambda i, j: (i, 0)),
            pl.BlockSpec((1, D_MODEL), lambda i, j: (0, 0)),
            pl.BlockSpec((D_MODEL, tn), lambda i, j: (0, j)),
            pl.BlockSpec((1, tn), lambda i, j: (0, j)),
            pl.BlockSpec((D_MODEL, ng), lambda i, j: (0, 0)),
        ],
        out_specs=[
            pl.BlockSpec((tm, tn), lambda i, j: (i, j)),
            pl.BlockSpec((tm, ng), lambda i, j: (i, 0)),
        ],
        out_shape=[
            jax.ShapeDtypeStruct((T, n), BF16),
            jax.ShapeDtypeStruct((T, ng), F32),
        ],
        scratch_shapes=[pltpu.VMEM((tm, D_MODEL), BF16)],
        compiler_params=_params("parallel", "arbitrary"),
    )(x2, g, w, cs, wg)


def _gelu_tanh(x):
    return 0.5 * x * (1.0 + jnp.tanh(math.sqrt(2.0 / math.pi) * (x + 0.044715 * (x * x * x))))


def _compress_kernel(h_ref, pe_ref, w1_ref, w2_ref, o_ref):
    hv = h_ref[0, 0, 0]
    nc = hv.shape[0]
    half = CMP_STRIDE * HEAD_DIM
    ya = _dot(hv, w1_ref[0, :half, :])
    yb = _dot(hv, w1_ref[0, half:, :])
    yb = pltpu.roll(yb, nc - 1, 0)
    row = lax.broadcasted_iota(jnp.int32, yb.shape, 0)
    yb = jnp.where(row == nc - 1, 0.0, yb)
    pe8 = jnp.broadcast_to(pe_ref[0], (8, 2 * half)).astype(BF16)
    pec = _dot(pe8, w1_ref[0])[0:1]
    hid = _gelu_tanh(ya + yb + pec)
    o_ref[0, 0, 0, :CMP_PAD_FRONT] = jnp.zeros((CMP_PAD_FRONT, HEAD_DIM), o_ref.dtype)
    o_ref[0, 0, 0, CMP_PAD_FRONT:CMP_PAD_FRONT + nc] = (
        _dot(hid.astype(BF16), w2_ref[0]).astype(o_ref.dtype))
    o_ref[0, 0, 0, CMP_PAD_FRONT + nc:] = jnp.zeros((CMP_PAD_BACK, HEAD_DIM), o_ref.dtype)


def _compress(halves, pe, w1, w2):
    _, B, G, nc, hw = halves.shape
    ncp = CMP_PAD_FRONT + nc + CMP_PAD_BACK
    return pl.pallas_call(
        _compress_kernel,
        grid=(2, B, G),
        in_specs=[
            pl.BlockSpec((1, 1, 1, nc, hw), lambda s, b, g: (s, b, g, 0, 0)),
            pl.BlockSpec((1, 1, 2 * hw), lambda s, b, g: (s, 0, 0)),
            pl.BlockSpec((1, 2 * hw, CMP_HIDDEN), lambda s, b, g: (s, 0, 0)),
            pl.BlockSpec((1, CMP_HIDDEN, HEAD_DIM), lambda s, b, g: (s, 0, 0)),
        ],
        out_specs=pl.BlockSpec((1, 1, 1, ncp, HEAD_DIM), lambda s, b, g: (s, b, g, 0, 0)),
        out_shape=jax.ShapeDtypeStruct((2, B, G, ncp, HEAD_DIM), BF16),
        compiler_params=_params("parallel", "parallel", "parallel"),
    )(halves, pe, w1, w2)


def _lanes(x, width):
    return jnp.concatenate([x] * (width // x.shape[1]), axis=1)


def _init_state(m_ref, l_ref, acc_ref):
    m_ref[...] = jnp.full(m_ref.shape, NEG, F32)
    l_ref[...] = jnp.zeros(l_ref.shape, F32)
    acc_ref[...] = jnp.zeros(acc_ref.shape, F32)


def _softmax_tile(slot, width, s_ref, p_ref, a_ref, m_ref, l_ref, chunk):
    n_rows = p_ref.shape[1]
    for r in range(0, n_rows, chunk):
        rows = slice(r, r + chunk)
        m_prev = m_ref[rows, :]
        m_new = jnp.maximum(m_prev, jnp.max(s_ref[slot, rows, :width], axis=-1, keepdims=True))
        a_ref[slot, rows, :] = jnp.exp2(m_prev - m_new)
        m_ref[rows, :] = m_new
    for r in range(0, n_rows, chunk):
        rows = slice(r, r + chunk)
        p = jnp.exp2(s_ref[slot, rows, :width] - _lanes(m_ref[rows, :], width))
        l_ref[rows, :] = (a_ref[slot, rows, :] * l_ref[rows, :]
                          + jnp.sum(p, axis=-1, keepdims=True))
        p_ref[slot, rows, :width] = p.astype(BF16)


def _value_tile(slot, width, v, p_ref, a_ref, acc_ref):
    acc_ref[...] = (acc_ref[...] * _lanes(a_ref[slot], acc_ref.shape[1])
                    + _dot(p_ref[slot, :, :width], v))


def _fold_tile(slot, width, v, s_ref, p_ref, a_ref, m_ref, l_ref, acc_ref, chunk):
    _softmax_tile(slot, width, s_ref, p_ref, a_ref, m_ref, l_ref, chunk)
    _value_tile(slot, width, v, p_ref, a_ref, acc_ref)


def _fold_tile_inplace(slot, width, v, s_ref, p_ref, m_ref, l_ref, acc_ref, chunk):
    n_rows = p_ref.shape[1]
    for r in range(0, n_rows, chunk):
        rows = slice(r, r + chunk)
        s = s_ref[slot, rows, :width]
        m_prev = m_ref[rows, :]
        m_new = jnp.maximum(m_prev, jnp.max(s, axis=-1, keepdims=True))
        alpha = jnp.exp2(m_prev - m_new)
        p = jnp.exp2(s - _lanes(m_new, width))
        l_ref[rows, :] = alpha * l_ref[rows, :] + jnp.sum(p, axis=-1, keepdims=True)
        m_ref[rows, :] = m_new
        acc_ref[rows, :] = acc_ref[rows, :] * _lanes(alpha, acc_ref.shape[1])
        p_ref[0, rows, :width] = p.astype(BF16)
    acc_ref[...] += _dot(p_ref[0, :, :width], v)


def _finish(l_ref, acc_ref):
    return acc_ref[...] / _lanes(l_ref[...], acc_ref.shape[1])


def _split_dot_nt(w, a):
    hi = a.astype(BF16)
    lo = (a - hi.astype(F32)).astype(BF16)
    return _dot_nt(w, hi) + _dot_nt(w, lo)


def _nsa_kernel(q_ref, gate_ref, kc_ref, vc_ref, ks_ref, vs_ref,
                kw0_ref, kw1_ref, kw2_ref, vw0_ref, vw1_ref, vw2_ref,
                dc_ref, ds_ref, dw_ref, perm_ref, eblk_ref, wfar_ref, o_ref,
                s_ref, p_ref, a_ref, m_ref, l_ref, acc_ref, imp_ref, sn_ref, pn_ref, selt_ref,
                sel_ref, q4_ref, pick_ref, oc_ref,
                *, n_sel, top_n):
    qt = pl.program_id(2)
    TQ, P = NSA_TQ, NSA_HPG
    R = P * TQ
    blocks_per_tile = NSA_TQ // SEL_LEN
    stats = (s_ref, p_ref, a_ref, m_ref, l_ref)
    state = stats + (acc_ref,)
    qblk = q_ref[0]
    q4_ref[...] = jnp.concatenate(
        [qblk[:, p * HEAD_DIM:(p + 1) * HEAD_DIM] for p in range(P)], axis=0)

    ncp = kc_ref.shape[3]
    cmp_per_tile = NSA_TQ // CMP_STRIDE
    far_end = cmp_per_tile * qt
    c0 = pl.multiple_of(far_end, cmp_per_tile)

    def compressed(w):
        lane = lax.broadcasted_iota(jnp.int32, (1, w), 1)
        far_bias = jnp.where((lane >= CMP_PAD_FRONT) & (lane < far_end), 0.0, NEG)
        kn = kc_ref[0, 0, 0, pl.ds(c0, CMP_NEAR), :]
        vn = vc_ref[0, 0, 0, pl.ds(c0, CMP_NEAR), :]
        s_ref[0, :, :w] = _dot_nt(q4_ref[...], kc_ref[0, 0, 0, :w, :])
        sn_ref[...] = _dot_nt(q4_ref[...], kn) + dc_ref[0, 0]
        imp_ref[:, :w] = jnp.zeros((TQ, w), F32)

        def cmp_head(p, carry):
            for c in range(TQ // NSA_ROWS):
                rows = pl.ds(pl.multiple_of(p * TQ + c * NSA_ROWS, NSA_ROWS), NSA_ROWS)
                irows = slice(c * NSA_ROWS, (c + 1) * NSA_ROWS)
                sf = s_ref[0, rows, :w] + far_bias
                sn = sn_ref[rows, :]
                m = jnp.maximum(jnp.max(sf, axis=-1, keepdims=True),
                                jnp.max(sn, axis=-1, keepdims=True))
                ef = jnp.exp2(sf - m)
                en = jnp.exp2(sn - m)
                l = jnp.sum(ef, axis=-1, keepdims=True) + jnp.sum(en, axis=-1, keepdims=True)
                inv = jnp.where(m > 0.5 * NEG, 1.0 / l, 0.0)
                pf = ef * inv
                p_ref[0, rows, :w] = pf.astype(BF16)
                pn_ref[rows, :] = en * inv
                imp_ref[irows, :w] += pf
            return carry

        lax.fori_loop(0, P, cmp_head, 0)
        p_near = pn_ref[...]
        oc_ref[...] = (_dot(p_ref[0, :, :w], vc_ref[0, 0, 0, :w, :])
                       + _dot(p_near.astype(BF16), vn))

        imp_near = jnp.sum(p_near.reshape(P, TQ, CMP_NEAR), axis=0)
        jj = lax.broadcasted_iota(jnp.int32, (n_sel, CMP_NEAR), 0)
        jn = lax.broadcasted_iota(jnp.int32, (n_sel, CMP_NEAR), 1)
        d = (far_end - CMP_PAD_FRONT + jn) - (SEL_LEN // CMP_STRIDE) * jj
        w_near = jnp.where((d == 0) | (d == 4), 1.0,
                           jnp.where((d >= 1) & (d <= 3), 2.0, 0.0)).astype(BF16)
        pick_ref[...] = (_split_dot_nt(wfar_ref[:, :w], imp_ref[:, :w])
                         + _split_dot_nt(w_near, imp_near))

    max_far_end = (ks_ref.shape[1] // TQ - 1) * cmp_per_tile
    candidates = sorted(set(CMP_WIDTHS) | {ncp})
    widths = [w for w in candidates if w < max_far_end]
    widths.append(min(w for w in candidates if w >= max_far_end))
    lo = -1
    for w in widths:
        pl.when((far_end > lo) & (far_end <= w))(functools.partial(compressed, w))
        lo = w
    o_c = oc_ref[...]
    p_slc = pick_ref[...]

    bj = lax.broadcasted_iota(jnp.int32, (n_sel, TQ), 0)
    ti = lax.broadcasted_iota(jnp.int32, (n_sel, TQ), 1)
    jt = blocks_per_tile * qt + ti // SEL_LEN
    forced = (bj == 0) | (bj == jt) | (bj == jt - 1)
    pick_ref[...] = jnp.where(forced, -jnp.inf, jnp.where(bj > jt, NEG, p_slc))
    idx = lax.broadcasted_iota(jnp.int32, (n_sel, LANES), 0).astype(F32)

    def pick(_, sc):
        mx = jnp.max(sc, axis=0, keepdims=True)
        first = jnp.min(jnp.where(sc == mx, idx, float(n_sel)), axis=0, keepdims=True)
        return jnp.where(idx == first, -jnp.inf, sc)

    for h in range(0, TQ, LANES):
        pick_ref[:, h:h + LANES] = lax.fori_loop(0, top_n - 3, pick, pick_ref[:, h:h + LANES])
    selneg_t = jnp.where(pick_ref[...] == -jnp.inf, 0.0, NEG)
    sel_ref[...] = selneg_t.T
    selneg = sel_ref[...]
    bj = lax.broadcasted_iota(jnp.int32, (TQ, n_sel), 1)

    nb0 = jnp.maximum(blocks_per_tile * (qt - 1), 0)
    pj = lax.broadcasted_iota(jnp.int32, (n_sel, LANES), 0)
    pu = lax.broadcasted_iota(jnp.int32, (n_sel, LANES), 1)
    perm_near = ((pj == nb0 + pu) & (pu < 2 * blocks_per_tile)).astype(BF16)
    sel_near = _dot(selneg.astype(BF16), perm_near).astype(BF16)
    sel_far = jnp.where(bj < blocks_per_tile * (qt - 1), selneg, NEG).astype(BF16)
    sel_all = _dot(sel_far, perm_ref[...]).astype(BF16)
    n_kt = selt_ref.shape[0] - 1
    for kt in range(n_kt):
        selt_ref[kt] = sel_all[:, kt * LANES:(kt + 1) * LANES]
    col = lax.broadcasted_iota(jnp.int32, (TQ, LANES), 1)
    selt_ref[n_kt] = jnp.where(col < NSA_TK // SEL_LEN, NEG, 0.0).astype(BF16)

    _init_state(m_ref, l_ref, acc_ref)
    near_keys = 2 * NSA_TQ
    ns = pl.multiple_of(nb0 * SEL_LEN, NSA_TQ)
    lhs = jnp.concatenate([q4_ref[...], jnp.concatenate([sel_near] * P, axis=0)], axis=1)
    rhs = jnp.concatenate([ks_ref[0, pl.ds(ns, near_keys), :], eblk_ref[:near_keys, :]], axis=1)
    s_ref[0, :, :near_keys] = _dot_nt(lhs, rhs) + ds_ref[0, 0].astype(F32)

    def far_scores(kt, sel_idx, slot):
        k0 = pl.multiple_of(kt * NSA_TK, NSA_TK)
        lhs_t = jnp.concatenate(
            [q4_ref[...], jnp.concatenate([selt_ref[sel_idx]] * P, axis=0)], axis=1)
        rhs_t = jnp.concatenate([ks_ref[0, pl.ds(k0, NSA_TK), :], eblk_ref[...]], axis=1)
        s_ref[slot, :, :NSA_TK] = _dot_nt(lhs_t, rhs_t)

    def far_softmax(slot):
        _softmax_tile(slot, NSA_TK, *stats, NSA_ROWS)

    def far_values(kt, slot):
        k0 = pl.multiple_of(kt * NSA_TK, NSA_TK)
        _value_tile(slot, NSA_TK, vs_ref[0, pl.ds(k0, NSA_TK), :], p_ref, a_ref, acc_ref)

    n_far = (qt + 2) // 4
    n_pairs = n_far // 2
    odd = jnp.maximum(n_far - 1, 0)
    odd_sel = jnp.where(n_far % 2 == 1, odd, selt_ref.shape[0] - 1)
    last_pair_tile = jnp.maximum(n_far - 2, 0)
    far_scores(odd, odd_sel, 1)
    _fold_tile(0, near_keys, vs_ref[0, pl.ds(ns, near_keys), :], *state, NSA_ROWS)
    far_scores(0, 0, 0)
    far_softmax(1)

    def far_pair(i, carry):
        a = 2 * i
        far_values(jnp.where(i == 0, odd, a - 1), 1)
        far_scores(a + 1, a + 1, 1)
        far_softmax(0)
        nxt = jnp.minimum(a + 2, last_pair_tile)
        far_values(a, 0)
        far_scores(nxt, nxt, 0)
        far_softmax(1)
        return carry

    lax.fori_loop(0, n_pairs, far_pair, 0)
    far_values(jnp.where(n_pairs == 0, odd, 2 * n_pairs - 1), 1)
    o_s = _finish(l_ref, acc_ref)

    _init_state(m_ref, l_ref, acc_ref)
    kw = jnp.concatenate([kw0_ref[0], kw1_ref[0], kw2_ref[0]], axis=0)
    vw = jnp.concatenate([vw0_ref[0], vw1_ref[0], vw2_ref[0]], axis=0)
    s_ref[0, :, :WINDOW + NSA_TQ] = _dot_nt(q4_ref[...], kw) + dw_ref[0, 0].astype(F32)
    _fold_tile(0, WINDOW + NSA_TQ, vw, *state, NSA_ROWS)
    o_w = _finish(l_ref, acc_ref)

    gate = gate_ref[0]
    outs = []
    for p in range(P):
        rows = slice(p * TQ, (p + 1) * TQ)
        gc = gate[:, 3 * p:3 * p + 1]
        gs = gate[:, 3 * p + 1:3 * p + 2]
        gw = gate[:, 3 * p + 2:3 * p + 3]
        outs.append(gc * o_c[rows] + gs * o_s[rows] + gw * o_w[rows])
    o_ref[0] = jnp.concatenate(outs, axis=1).astype(o_ref.dtype)


def _nsa(proj3, gates3, cmp_pad, dc, ds, dw, perm, eblk, wfar, n_sel):
    B, S, _ = proj3.shape
    G = NSA_GROUPS
    ncp = cmp_pad.shape[3]
    top_n = min(SEL_TOPK, n_sel)
    assert top_n > 3, "the selection loop assumes the three forced blocks fit in the top-n"
    qw = NSA_HPG * HEAD_DIM
    R = NSA_HPG * NSA_TQ
    wide = max(NSA_TK, ncp, WINDOW + NSA_TQ)
    wblk = lambda col, back: pl.BlockSpec(
        (1, NSA_TQ, HEAD_DIM),
        lambda b, g, t, col=col, back=back: (b, jnp.maximum(t - back, 0), col // HEAD_DIM + g))
    kernel = functools.partial(_nsa_kernel, n_sel=n_sel, top_n=top_n)
    return pl.pallas_call(
        kernel,
        grid=(B, G, S // NSA_TQ),
        in_specs=[
            pl.BlockSpec((1, NSA_TQ, qw), lambda b, g, t: (b, t, g)),
            pl.BlockSpec((1, NSA_TQ, LANES), lambda b, g, t: (b, t, g)),
            pl.BlockSpec((1, 1, 1, ncp, HEAD_DIM), lambda b, g, t: (0, b, g, 0, 0)),
            pl.BlockSpec((1, 1, 1, ncp, HEAD_DIM), lambda b, g, t: (1, b, g, 0, 0)),
            pl.BlockSpec((1, S, HEAD_DIM), lambda b, g, t: (b, 0, COL_KS // HEAD_DIM + g)),
            pl.BlockSpec((1, S, HEAD_DIM), lambda b, g, t: (b, 0, COL_VS // HEAD_DIM + g)),
            wblk(COL_KW, 2), wblk(COL_KW, 1), wblk(COL_KW, 0),
            wblk(COL_VW, 2), wblk(COL_VW, 1), wblk(COL_VW, 0),
            pl.BlockSpec((1, 1) + dc.shape[2:], lambda b, g, t: (jnp.minimum(t, 1), g, 0, 0)),
            pl.BlockSpec((1, 1) + ds.shape[2:], lambda b, g, t: (jnp.minimum(t, 1), g, 0, 0)),
            pl.BlockSpec((1, 1) + dw.shape[2:], lambda b, g, t: (jnp.minimum(t, 2), g, 0, 0)),
            pl.BlockSpec(perm.shape, lambda b, g, t: (0, 0)),
            pl.BlockSpec(eblk.shape, lambda b, g, t: (0, 0)),
            pl.BlockSpec(wfar.shape, lambda b, g, t: (0, 0)),
        ],
        out_specs=pl.BlockSpec((1, NSA_TQ, qw), lambda b, g, t: (b, t, g)),
        out_shape=jax.ShapeDtypeStruct((B, S, NSA_HEADS * HEAD_DIM), BF16),
        scratch_shapes=[
            pltpu.VMEM((2, R, wide), F32),
            pltpu.VMEM((2, R, wide), BF16),
            pltpu.VMEM((2, R, LANES), F32),
            pltpu.VMEM((R, LANES), F32),
            pltpu.VMEM((R, LANES), F32),
            pltpu.VMEM((R, HEAD_DIM), F32),
            pltpu.VMEM((NSA_TQ, ncp), F32),
            pltpu.VMEM((R, CMP_NEAR), F32),
            pltpu.VMEM((R, CMP_NEAR), F32),
            pltpu.VMEM((S // NSA_TK + 1, NSA_TQ, LANES), BF16),
            pltpu.VMEM((NSA_TQ, n_sel), F32),
            pltpu.VMEM((R, HEAD_DIM), BF16),
            pltpu.VMEM((n_sel, NSA_TQ), F32),
            pltpu.VMEM((R, HEAD_DIM), F32),
        ],
        compiler_params=_params("parallel", "parallel", "arbitrary"),
    )(proj3, gates3, cmp_pad, cmp_pad, proj3, proj3,
      proj3, proj3, proj3, proj3, proj3, proj3, dc, ds, dw, perm, eblk, wfar)


def _diff_kernel(q_ref, k_ref, v_ref, dd_ref, lq1_ref, lk1_ref, lq2_ref, lk2_ref, hg_ref,
                 o_ref, s_ref, p_ref, m_ref, l_ref, acc_ref, *, lam_init):
    qt = pl.program_id(2)
    TQ = DIFF_TQ
    q = q_ref[0]
    zero = jnp.zeros((TQ, HEAD_DIM), BF16)
    lhs = jnp.concatenate([
        jnp.concatenate([q[:, :HEAD_DIM], zero], axis=1),
        jnp.concatenate([zero, q[:, HEAD_DIM:]], axis=1)], axis=0)
    lam = (jnp.exp(jnp.sum(lq1_ref[...] * lk1_ref[...], axis=-1, keepdims=True))
           - jnp.exp(jnp.sum(lq2_ref[...] * lk2_ref[...], axis=-1, keepdims=True)) + lam_init)

    def scores(kt, slot, bias):
        k0 = pl.multiple_of(kt * DIFF_TK, DIFF_TK)
        sc = _dot_nt(lhs, k_ref[0, pl.ds(k0, DIFF_TK), :])
        if bias is not None:
            sc = (sc.reshape(2, TQ, DIFF_TK) + bias[None]).reshape(2 * TQ, DIFF_TK)
        s_ref[slot] = sc

    def fold(kt, slot):
        k0 = pl.multiple_of(kt * DIFF_TK, DIFF_TK)
        _fold_tile_inplace(slot, DIFF_TK, v_ref[0, pl.ds(k0, DIFF_TK), :],
                           s_ref, p_ref, m_ref, l_ref, acc_ref, DIFF_ROWS)

    _init_state(m_ref, l_ref, acc_ref)
    n_far = jnp.maximum(qt - 1, 0)
    prev = jnp.maximum(qt - 1, 0)
    odd = jnp.maximum(n_far - 1, 0)
    last_pair_tile = jnp.maximum(n_far - 2, 0)
    scores(qt, 0, dd_ref[0, :, DIFF_TK:].astype(F32))
    fold(qt, 0)
    scores(prev, 0, jnp.where(qt >= 1, dd_ref[0, :, :DIFF_TK].astype(F32), NEG))
    fold(prev, 0)
    scores(odd, 0, jnp.where(n_far % 2 == 1, jnp.zeros((TQ, DIFF_TK), F32), NEG))
    scores(0, 1, None)
    fold(odd, 0)

    def far_pair(i, carry):
        a = 2 * i
        scores(a + 1, 0, None)
        fold(a, 1)
        scores(jnp.minimum(a + 2, last_pair_tile), 1, None)
        fold(a + 1, 0)
        return carry

    lax.fori_loop(0, n_far // 2, far_pair, 0)
    a = _finish(l_ref, acc_ref)
    o = a[:TQ] - lam * a[TQ:]
    y = _norm_rows(o, hg_ref[0]) * (1.0 - lam_init)
    o_ref[0] = y.astype(o_ref.dtype)


def _diff(proj3, dd, lq1, lk1, lq2, lk2, hg, lam_init):
    B, S, _ = proj3.shape
    H = DIFF_HEADS
    w = 2 * HEAD_DIM
    R = 2 * DIFF_TQ
    vec = pl.BlockSpec((1, HEAD_DIM), lambda b, h, t: (0, 0))
    kernel = functools.partial(_diff_kernel, lam_init=lam_init)
    return pl.pallas_call(
        kernel,
        grid=(B, H, S // DIFF_TQ),
        in_specs=[
            pl.BlockSpec((1, DIFF_TQ, w), lambda b, h, t: (b, t, COL_QD // w + h)),
            pl.BlockSpec((1, S, w), lambda b, h, t: (b, 0, COL_KD // w + h)),
            pl.BlockSpec((1, S, w), lambda b, h, t: (b, 0, COL_VD // w + h)),
            pl.BlockSpec((1,) + dd.shape[1:], lambda b, h, t: (h, 0, 0)),
            vec, vec, vec, vec,
            pl.BlockSpec((1, 1, w), lambda b, h, t: (h, 0, 0)),
        ],
        out_specs=pl.BlockSpec((1, DIFF_TQ, w), lambda b, h, t: (b, t, h)),
        out_shape=jax.ShapeDtypeStruct((B, S, H * w), BF16),
        scratch_shapes=[
            pltpu.VMEM((2, R, DIFF_TK), F32),
            pltpu.VMEM((1, R, DIFF_TK), BF16),
            pltpu.VMEM((R, LANES), F32),
            pltpu.VMEM((R, LANES), F32),
            pltpu.VMEM((R, DIFF_V_DIM), F32),
        ],
        compiler_params=_params("parallel", "parallel", "arbitrary"),
    )(proj3, proj3, proj3, dd, lq1, lk1, lq2, lk2, hg)


def _merge_kernel(on_ref, od_ref, wn_ref, wd_ref, ga_ref, gb_ref, o_ref):
    u1 = _dot(on_ref[...], wn_ref[...])
    u2 = _dot(od_ref[...], wd_ref[...])
    ga = jax.nn.sigmoid(ga_ref[...].astype(F32))
    gb = jax.nn.sigmoid(gb_ref[...].astype(F32))
    o_ref[...] = (ga * u1 + gb * u2).astype(o_ref.dtype)


def _merge(o_nsa, o_d, wn, wd, proj, tm=1024, tn=512):
    T, kn = o_nsa.shape
    kd = o_d.shape[1]
    return pl.pallas_call(
        _merge_kernel,
        grid=(T // tm, D_MODEL // tn),
        in_specs=[
            pl.BlockSpec((tm, kn), lambda i, j: (i, 0)),
            pl.BlockSpec((tm, kd), lambda i, j: (i, 0)),
            pl.BlockSpec((kn, tn), lambda i, j: (0, j)),
            pl.BlockSpec((kd, tn), lambda i, j: (0, j)),
            pl.BlockSpec((tm, tn), lambda i, j: (i, COL_GM // tn + j)),
            pl.BlockSpec((tm, tn), lambda i, j: (i, (COL_GM + D_MODEL) // tn + j)),
        ],
        out_specs=pl.BlockSpec((tm, tn), lambda i, j: (i, j)),
        out_shape=jax.ShapeDtypeStruct((T, D_MODEL), BF16),
        compiler_params=_params("parallel", "arbitrary"),
    )(o_nsa, o_d, wn, wd, proj, proj)


def _out_proj_kernel(a_ref, w_ref, x_ref, o_ref):
    o_ref[...] = x_ref[...] + _dot(a_ref[...], w_ref[...])


def _out_proj(a, w, x2, tm=512, tn=2048):
    T, k = a.shape
    return pl.pallas_call(
        _out_proj_kernel,
        grid=(T // tm, D_MODEL // tn),
        in_specs=[
            pl.BlockSpec((tm, k), lambda i, j: (i, 0)),
            pl.BlockSpec((k, tn), lambda i, j: (0, j)),
            pl.BlockSpec((tm, tn), lambda i, j: (i, j)),
        ],
        out_specs=pl.BlockSpec((tm, tn), lambda i, j: (i, j)),
        out_shape=jax.ShapeDtypeStruct((T, D_MODEL), F32),
        compiler_params=_params("parallel", "arbitrary"),
    )(a, w, x2)


def _ffn_in_kernel(x_ref, g_ref, wa_ref, wb_ref, o_ref, h_ref):
    @pl.when(pl.program_id(1) == 0)
    def _():
        h_ref[...] = _norm_rows(x_ref[...], g_ref[...]).astype(BF16)

    h = h_ref[...]
    a = _dot(h, wa_ref[...])
    b = _dot(h, wb_ref[...])
    o_ref[...] = (a * jax.nn.sigmoid(a) * b).astype(o_ref.dtype)


def _ffn_in(x2, g, w, tm=1024, tn=512):
    T = x2.shape[0]
    nb = D_FF // tn
    return pl.pallas_call(
        _ffn_in_kernel,
        grid=(T // tm, nb),
        in_specs=[
            pl.BlockSpec((tm, D_MODEL), lambda i, j: (i, 0)),
            pl.BlockSpec((1, D_MODEL), lambda i, j: (0, 0)),
            pl.BlockSpec((D_MODEL, tn), lambda i, j: (0, j)),
            pl.BlockSpec((D_MODEL, tn), lambda i, j: (0, j + nb)),
        ],
        out_specs=pl.BlockSpec((tm, tn), lambda i, j: (i, j)),
        out_shape=jax.ShapeDtypeStruct((T, D_FF), BF16),
        scratch_shapes=[pltpu.VMEM((tm, D_MODEL), BF16)],
        compiler_params=_params("parallel", "arbitrary"),
    )(x2, g, w, w)


def _ffn_out_kernel(a_ref, w_ref, x_ref, g_ref, o_ref, acc_ref, *, final_norm):
    k = pl.program_id(1)

    @pl.when(k == 0)
    def _():
        acc_ref[...] = x_ref[...]

    acc_ref[...] += _dot(a_ref[...], w_ref[...])

    @pl.when(k == pl.num_programs(1) - 1)
    def _():
        y = acc_ref[...]
        o_ref[...] = _norm_rows(y, g_ref[...]) if final_norm else y


def _ffn_out(a, w, x2, g, final_norm, tm=512, tk=2816):
    T = a.shape[0]
    kernel = functools.partial(_ffn_out_kernel, final_norm=final_norm)
    return pl.pallas_call(
        kernel,
        grid=(T // tm, D_FF // tk),
        in_specs=[
            pl.BlockSpec((tm, tk), lambda i, k: (i, k)),
            pl.BlockSpec((tk, D_MODEL), lambda i, k: (k, 0)),
            pl.BlockSpec((tm, D_MODEL), lambda i, k: (i, 0)),
            pl.BlockSpec((1, D_MODEL), lambda i, k: (0, 0)),
        ],
        out_specs=pl.BlockSpec((tm, D_MODEL), lambda i, k: (i, 0)),
        out_shape=jax.ShapeDtypeStruct((T, D_MODEL), F32),
        scratch_shapes=[pltpu.VMEM((tm, D_MODEL), F32)],
        compiler_params=_params("parallel", "arbitrary"),
    )(a, w, x2, g)


def _t5_bucket(rel):
    n = jnp.maximum(rel, 0)
    nf = jnp.maximum(n, 1).astype(F32)
    large = REL_MAX_EXACT + (jnp.log(nf / REL_MAX_EXACT) / math.log(REL_MAX_DIST / REL_MAX_EXACT)
                             * (REL_BUCKETS - REL_MAX_EXACT)).astype(jnp.int32)
    large = jnp.minimum(large, REL_BUCKETS - 1)
    return jnp.where(n < REL_MAX_EXACT, n, large)


def _bias_of_rel(table, rel, valid):
    shifted = (table - table[REL_BUCKETS - 1:REL_BUCKETS]) * LOG2E
    vals = jnp.moveaxis(shifted[_t5_bucket(rel)], -1, 0)
    return jnp.where(valid[None], vals, NEG)


def _toeplitz_bias(table, n_i, n_j, offset, max_rel=None):
    period = REL_MAX_DIST
    near = jnp.arange(period, dtype=jnp.int32)
    g = _bias_of_rel(table, near, near >= 0)
    a = near[:, None]
    b = near[None, :]
    band = jnp.tile(g[:, (a - b + offset) % period], (1, n_i // period, n_j // period))
    i = jnp.arange(n_i, dtype=jnp.int32)[:, None]
    j = jnp.arange(n_j, dtype=jnp.int32)[None, :]
    rel = (i - j + offset)[None]
    far = 0.0 if max_rel is None else jnp.where(rel < max_rel, 0.0, NEG)
    return jnp.where(rel < 0, NEG, jnp.where(rel < period, band, far))


def _nsa_tables(table_nsa):
    TQ = NSA_TQ

    def stack(t):
        return t.reshape(NSA_GROUPS, NSA_HPG * TQ, t.shape[-1])

    i = jnp.arange(TQ, dtype=jnp.int32)[:, None]
    jc = jnp.arange(CMP_NEAR, dtype=jnp.int32)[None, :]
    rel_c = i - CMP_STRIDE * (jc - CMP_PAD_FRONT) - (CMP_LEN - 1)
    dc = jnp.stack([
        stack(_bias_of_rel(table_nsa, rel_c, (rel_c >= 0) & (jc >= CMP_PAD_FRONT))),
        stack(_bias_of_rel(table_nsa, rel_c, rel_c >= 0)),
    ])
    ds = jnp.stack([
        stack(_toeplitz_bias(table_nsa, TQ, 2 * TQ, 0)),
        stack(_toeplitz_bias(table_nsa, TQ, 2 * TQ, TQ)),
    ]).astype(BF16)
    win = stack(_toeplitz_bias(table_nsa, TQ, WINDOW + TQ, WINDOW, WINDOW))
    jw = jnp.arange(WINDOW + TQ, dtype=jnp.int32)[None, None, :]
    dw = jnp.stack([jnp.where(jw >= WINDOW - c * TQ, win, NEG) for c in range(3)]).astype(BF16)
    return dc, ds, dw


def _diff_table(table_diff):
    return _toeplitz_bias(table_diff, DIFF_TQ, 2 * DIFF_TK, DIFF_TK).astype(BF16)


def _selection_constants(S):
    n_sel = S // SEL_LEN
    n_cmp = S // CMP_STRIDE
    n_kt = S // NSA_TK
    blocks = NSA_TK // SEL_LEN
    j = jnp.arange(n_sel, dtype=jnp.int32)[:, None]
    col = jnp.arange(n_kt * LANES, dtype=jnp.int32)[None, :]
    perm = ((j == blocks * (col // LANES) + col % LANES) & (col % LANES < blocks)).astype(BF16)
    key = jnp.arange(NSA_TK, dtype=jnp.int32)[:, None]
    u = jnp.arange(LANES, dtype=jnp.int32)[None, :]
    eblk = (u == key // SEL_LEN).astype(BF16)
    c = jnp.arange(n_cmp + CMP_PAD_FRONT + CMP_PAD_BACK, dtype=jnp.int32)[None, :] - CMP_PAD_FRONT
    jb = jnp.arange(n_sel, dtype=jnp.int32)[:, None]
    d = c - (SEL_LEN // CMP_STRIDE) * jb
    wfar = jnp.where((d == 0) | (d == 4), 1.0, jnp.where((d >= 1) & (d <= 3), 2.0, 0.0))
    wfar = jnp.where((c >= 0) & (c < n_cmp), wfar, 0.0).astype(BF16)
    return perm, eblk, wfar


def kernel(x, norm_mix_g, w_in, cmp_pe_k, cmp_pe_v, cmp_w1_k, cmp_w2_k, cmp_w1_v, cmp_w2_v,
           diff_lq1, diff_lk1, diff_lq2, diff_lk2, diff_head_g, w_up_nsa, w_up_diff, w_out,
           norm_ff_g, w_ff_in, w_ff_out, rel_bias_table, norm_final_g):
    B, S, D = x.shape
    T = B * S
    depth = w_in.shape[0]
    n_sel = S // SEL_LEN
    n_cmp = S // CMP_STRIDE
    qscale = HEAD_DIM ** -0.5 * LOG2E

    dc, ds, dw = _nsa_tables(rel_bias_table[:, :NSA_HEADS])
    dd = _diff_table(rel_bias_table[:, NSA_HEADS:])
    perm, eblk, wfar = _selection_constants(S)

    gn0 = COL_QD
    gn1 = gn0 + NSA_HEADS * 3
    cs = jnp.ones((1, N_PROJ), F32)
    cs = cs.at[:, COL_QN:COL_KC].set(qscale).at[:, COL_QD:COL_KD].set(qscale)

    x2 = x.reshape(T, D)
    for l in range(depth):
        w_main = jnp.concatenate([w_in[l][:, :gn0], w_in[l][:, gn1:]], axis=1).astype(BF16)
        wg = w_in[l][:, gn0:gn1].reshape(D, NSA_GROUPS, NSA_HPG * 3)
        wg = jnp.pad(wg, ((0, 0), (0, 0), (0, LANES - NSA_HPG * 3))).reshape(D, NSA_GROUPS * LANES)
        proj, gates = _in_proj(x2, norm_mix_g[l][None], w_main, cs, wg.astype(BF16))
        proj3 = proj.reshape(B, S, N_PROJ)
        gates3 = gates.reshape(B, S, NSA_GROUPS * LANES)

        halves = proj3[:, :, COL_KC:COL_KS].reshape(B, n_cmp, CMP_STRIDE, 2, NSA_GROUPS, HEAD_DIM)
        halves = halves.transpose(3, 0, 4, 1, 2, 5).reshape(
            2, B, NSA_GROUPS, n_cmp, CMP_STRIDE * HEAD_DIM)
        pe = jnp.stack([cmp_pe_k[l], cmp_pe_v[l]]).reshape(2, 1, CMP_LEN * HEAD_DIM)
        w1 = jnp.stack([cmp_w1_k[l], cmp_w1_v[l]]).astype(BF16)
        w2 = jnp.stack([cmp_w2_k[l], cmp_w2_v[l]]).astype(BF16)
        cmp_pad = _compress(halves, pe, w1, w2)

        o_nsa = _nsa(proj3, gates3, cmp_pad, dc, ds, dw, perm, eblk, wfar, n_sel)

        lam_init = 0.8 - 0.6 * math.exp(-0.3 * l)
        o_d = _diff(proj3, dd, diff_lq1[l][None], diff_lk1[l][None], diff_lq2[l][None],
                    diff_lk2[l][None], diff_head_g[l][:, None, :], lam_init)

        mix = _merge(o_nsa.reshape(T, -1), o_d.reshape(T, -1),
                     w_up_nsa[l].astype(BF16), w_up_diff[l].astype(BF16), proj)
        x2 = _out_proj(mix, w_out[l].astype(BF16), x2)

        act = _ffn_in(x2, norm_ff_g[l][None], w_ff_in[l].astype(BF16))
        last = l == depth - 1
        x2 = _ffn_out(act, w_ff_out[l].astype(BF16), x2,
                      norm_final_g[None] if last else norm_ff_g[l][None], final_norm=last)
    return x2.reshape(B, S, D)
```

```python
import functools
import math

import jax
import jax.numpy as jnp
from jax import lax
from jax.experimental import pallas as pl
from jax.experimental.pallas import tpu as pltpu

D_MODEL = 2048
HEAD_DIM = 128
NSA_HEADS = 8
NSA_GROUPS = 2
NSA_HPG = NSA_HEADS // NSA_GROUPS
CMP_LEN = 32
CMP_STRIDE = 16
SEL_LEN = 64
SEL_TOPK = 16
WINDOW = 512
CMP_HIDDEN = 256
DIFF_HEADS = 4
DIFF_V_DIM = 2 * HEAD_DIM
REL_BUCKETS = 32
REL_MAX_EXACT = 16
REL_MAX_DIST = 128
D_FF = -(-8 * D_MODEL // (3 * 256)) * 256
EPS = 1e-6
NEG = -1e30
BIG = 1e30
LOG2E = math.log2(math.e)

F32 = jnp.float32
BF16 = jnp.bfloat16
LANES = 128

VMEM_LIMIT_BYTES = 56 * 1024 * 1024

COL_QN = 0
COL_KC = 1024
COL_KS = 1536
COL_VS = 1792
COL_KW = 2048
COL_VW = 2304
COL_QD = 2560
COL_KD = 3584
COL_VD = 4608
COL_GM = 5632
N_PROJ = 9728

NSA_TQ = 256
NSA_TK = 1024
NSA_ROWS = 32
CMP_NEAR = 32
CMP_WIDTHS = (256, 512, 768, 1024)
CMP_PAD_FRONT = 16
CMP_PAD_BACK = 112
DIFF_TQ = 512
DIFF_TK = 512
DIFF_ROWS = 64


def _dot(a, b):
    return jnp.dot(a, b, preferred_element_type=F32)


def _dot_nt(a, b):
    return lax.dot_general(a, b, (((1,), (1,)), ((), ())), preferred_element_type=F32)


def _params(*sem):
    return pltpu.CompilerParams(dimension_semantics=sem, vmem_limit_bytes=VMEM_LIMIT_BYTES)


def _norm_rows(x, g):
    ms = jnp.mean(x * x, axis=-1, keepdims=True)
    return x * lax.rsqrt(ms + EPS) * g


def _in_proj_kernel(x_ref, g_ref, w_ref, cs_ref, wg_ref, o_ref, og_ref, h_ref):
    @pl.when(pl.program_id(1) == 0)
    def _():
        hb = _norm_rows(x_ref[...], g_ref[...]).astype(BF16)
        h_ref[...] = hb
        og_ref[...] = jax.nn.sigmoid(_dot(hb, wg_ref[...]))

    o_ref[...] = (_dot(h_ref[...], w_ref[...]) * cs_ref[...]).astype(o_ref.dtype)


def _in_proj(x2, g, w, cs, wg, tm=512, tn=2432):
    T = x2.shape[0]
    n = w.shape[1]
    ng = wg.shape[1]
    return pl.pallas_call(
        _in_proj_kernel,
        grid=(T // tm, n // tn),
        in_specs=[
            pl.BlockSpec((tm, D_MODEL), lambda i, j: (i, 0)),
            pl.BlockSpec((1, D_MODEL), lambda i, j: (0, 0)),
            pl.BlockSpec((D_MODEL, tn), lambda i, j: (0, j)),
            pl.BlockSpec((1, tn), lambda i, j: (0, j)),
            pl.BlockSpec((D_MODEL, ng), lambda i, j: (0, 0)),
        ],
        out_specs=[
            pl.BlockSpec((tm, tn), lambda i, j: (i, j)),
            pl.BlockSpec((tm, ng), lambda i, j: (i, 0)),
        ],
        out_shape=[
            jax.ShapeDtypeStruct((T, n), BF16),
            jax.ShapeDtypeStruct((T, ng), F32),
        ],
        scratch_shapes=[pltpu.VMEM((tm, D_MODEL), BF16)],
        compiler_params=_params("parallel", "arbitrary"),
    )(x2, g, w, cs, wg)


def _gelu_tanh(x):
    return 0.5 * x * (1.0 + jnp.tanh(math.sqrt(2.0 / math.pi) * (x + 0.044715 * (x * x * x))))


def _compress_kernel(h_ref, pe_ref, w1_ref, w2_ref, o_ref):
    hv = h_ref[0, 0, 0]
    nc = hv.shape[0]
    half = CMP_STRIDE * HEAD_DIM
    ya = _dot(hv, w1_ref[0, :half, :])
    yb = _dot(hv, w1_ref[0, half:, :])
    yb = pltpu.roll(yb, nc - 1, 0)
    row = lax.broadcasted_iota(jnp.int32, yb.shape, 0)
    yb = jnp.where(row == nc - 1, 0.0, yb)
    pe8 = jnp.broadcast_to(pe_ref[0], (8, 2 * half)).astype(BF16)
    pec = _dot(pe8, w1_ref[0])[0:1]
    hid = _gelu_tanh(ya + yb + pec)
    o_ref[0, 0, 0, :CMP_PAD_FRONT] = jnp.zeros((CMP_PAD_FRONT, HEAD_DIM), o_ref.dtype)
    o_ref[0, 0, 0, CMP_PAD_FRONT:CMP_PAD_FRONT + nc] = (
        _dot(hid.astype(BF16), w2_ref[0]).astype(o_ref.dtype))
    o_ref[0, 0, 0, CMP_PAD_FRONT + nc:] = jnp.zeros((CMP_PAD_BACK, HEAD_DIM), o_ref.dtype)


def _compress(halves, pe, w1, w2):
    _, B, G, nc, hw = halves.shape
    ncp = CMP_PAD_FRONT + nc + CMP_PAD_BACK
    return pl.pallas_call(
        _compress_kernel,
        grid=(2, B, G),
        in_specs=[
            pl.BlockSpec((1, 1, 1, nc, hw), lambda s, b, g: (s, b, g, 0, 0)),
            pl.BlockSpec((1, 1, 2 * hw), lambda s, b, g: (s, 0, 0)),
            pl.BlockSpec((1, 2 * hw, CMP_HIDDEN), lambda s, b, g: (s, 0, 0)),
            pl.BlockSpec((1, CMP_HIDDEN, HEAD_DIM), lambda s, b, g: (s, 0, 0)),
        ],
        out_specs=pl.BlockSpec((1, 1, 1, ncp, HEAD_DIM), lambda s, b, g: (s, b, g, 0, 0)),
        out_shape=jax.ShapeDtypeStruct((2, B, G, ncp, HEAD_DIM), BF16),
        compiler_params=_params("parallel", "parallel", "parallel"),
    )(halves, pe, w1, w2)


def _lanes(x, width):
    return jnp.concatenate([x] * (width // x.shape[1]), axis=1)


def _init_state(m_ref, l_ref, acc_ref):
    m_ref[...] = jnp.full(m_ref.shape, NEG, F32)
    l_ref[...] = jnp.zeros(l_ref.shape, F32)
    acc_ref[...] = jnp.zeros(acc_ref.shape, F32)


def _softmax_tile(slot, width, s_ref, p_ref, a_ref, m_ref, l_ref, chunk):
    n_rows = p_ref.shape[1]
    for r in range(0, n_rows, chunk):
        rows = slice(r, r + chunk)
        m_prev = m_ref[rows, :]
        m_new = jnp.maximum(m_prev, jnp.max(s_ref[slot, rows, :width], axis=-1, keepdims=True))
        a_ref[slot, rows, :] = jnp.exp2(m_prev - m_new)
        m_ref[rows, :] = m_new
    for r in range(0, n_rows, chunk):
        rows = slice(r, r + chunk)
        p = jnp.exp2(s_ref[slot, rows, :width] - _lanes(m_ref[rows, :], width))
        l_ref[rows, :] = (a_ref[slot, rows, :] * l_ref[rows, :]
                          + jnp.sum(p, axis=-1, keepdims=True))
        p_ref[slot, rows, :width] = p.astype(BF16)


def _value_tile(slot, width, v, p_ref, a_ref, acc_ref):
    acc_ref[...] = (acc_ref[...] * _lanes(a_ref[slot], acc_ref.shape[1])
                    + _dot(p_ref[slot, :, :width], v))


def _fold_tile(slot, width, v, s_ref, p_ref, a_ref, m_ref, l_ref, acc_ref, chunk):
    _softmax_tile(slot, width, s_ref, p_ref, a_ref, m_ref, l_ref, chunk)
    _value_tile(slot, width, v, p_ref, a_ref, acc_ref)


def _fold_tile_inplace(slot, width, v, s_ref, p_ref, m_ref, l_ref, acc_ref, chunk):
    n_rows = p_ref.shape[1]
    for r in range(0, n_rows, chunk):
        rows = slice(r, r + chunk)
        s = s_ref[slot, rows, :width]
        m_prev = m_ref[rows, :]
        m_new = jnp.maximum(m_prev, jnp.max(s, axis=-1, keepdims=True))
        alpha = jnp.exp2(m_prev - m_new)
        p = jnp.exp2(s - _lanes(m_new, width))
        l_ref[rows, :] = alpha * l_ref[rows, :] + jnp.sum(p, axis=-1, keepdims=True)
        m_ref[rows, :] = m_new
        acc_ref[rows, :] = acc_ref[rows, :] * _lanes(alpha, acc_ref.shape[1])
        p_ref[0, rows, :width] = p.astype(BF16)
    acc_ref[...] += _dot(p_ref[0, :, :width], v)


def _finish(l_ref, acc_ref):
    return acc_ref[...] / _lanes(l_ref[...], acc_ref.shape[1])


def _split_dot_nt(w, a):
    hi = a.astype(BF16)
    lo = (a - hi.astype(F32)).astype(BF16)
    return _dot_nt(w, hi) + _dot_nt(w, lo)


def _nsa_kernel(q_ref, gate_ref, kc_ref, vc_ref, ks_ref, vs_ref,
                kw0_ref, kw1_ref, kw2_ref, vw0_ref, vw1_ref, vw2_ref,
                dc_ref, ds_ref, dw_ref, perm_ref, eblk_ref, wfar_ref, o_ref,
                s_ref, p_ref, a_ref, m_ref, l_ref, acc_ref, imp_ref, sn_ref, pn_ref, selt_ref,
                sel_ref, q4_ref, pick_ref, oc_ref,
                *, n_sel, top_n):
    qt = pl.program_id(2)
    TQ, P = NSA_TQ, NSA_HPG
    R = P * TQ
    blocks_per_tile = NSA_TQ // SEL_LEN
    stats = (s_ref, p_ref, a_ref, m_ref, l_ref)
    state = stats + (acc_ref,)
    qblk = q_ref[0]
    q4_ref[...] = jnp.concatenate(
        [qblk[:, p * HEAD_DIM:(p + 1) * HEAD_DIM] for p in range(P)], axis=0)

    ncp = kc_ref.shape[3]
    cmp_per_tile = NSA_TQ // CMP_STRIDE
    far_end = cmp_per_tile * qt
    c0 = pl.multiple_of(far_end, cmp_per_tile)

    def compressed(w):
        lane = lax.broadcasted_iota(jnp.int32, (1, w), 1)
        far_bias = jnp.where((lane >= CMP_PAD_FRONT) & (lane < far_end), 0.0, NEG)
        kn = kc_ref[0, 0, 0, pl.ds(c0, CMP_NEAR), :]
        vn = vc_ref[0, 0, 0, pl.ds(c0, CMP_NEAR), :]
        s_ref[0, :, :w] = _dot_nt(q4_ref[...], kc_ref[0, 0, 0, :w, :])
        sn_ref[...] = _dot_nt(q4_ref[...], kn) + dc_ref[0, 0]
        imp_ref[:, :w] = jnp.zeros((TQ, w), F32)

        def cmp_head(p, carry):
            for c in range(TQ // NSA_ROWS):
                rows = pl.ds(pl.multiple_of(p * TQ + c * NSA_ROWS, NSA_ROWS), NSA_ROWS)
                irows = slice(c * NSA_ROWS, (c + 1) * NSA_ROWS)
                sf = s_ref[0, rows, :w] + far_bias
                sn = sn_ref[rows, :]
                m = jnp.maximum(jnp.max(sf, axis=-1, keepdims=True),
                                jnp.max(sn, axis=-1, keepdims=True))
                ef = jnp.exp2(sf - m)
                en = jnp.exp2(sn - m)
                l = jnp.sum(ef, axis=-1, keepdims=True) + jnp.sum(en, axis=-1, keepdims=True)
                inv = jnp.where(m > 0.5 * NEG, 1.0 / l, 0.0)
                pf = ef * inv
                p_ref[0, rows, :w] = pf.astype(BF16)
                pn_ref[rows, :] = en * inv
                imp_ref[irows, :w] += pf
            return carry

        lax.fori_loop(0, P, cmp_head, 0)
        p_near = pn_ref[...]
        oc_ref[...] = (_dot(p_ref[0, :, :w], vc_ref[0, 0, 0, :w, :])
                       + _dot(p_near.astype(BF16), vn))

        imp_near = jnp.sum(p_near.reshape(P, TQ, CMP_NEAR), axis=0)
        jj = lax.broadcasted_iota(jnp.int32, (n_sel, CMP_NEAR), 0)
        jn = lax.broadcasted_iota(jnp.int32, (n_sel, CMP_NEAR), 1)
        d = (far_end - CMP_PAD_FRONT + jn) - (SEL_LEN // CMP_STRIDE) * jj
        w_near = jnp.where((d == 0) | (d == 4), 1.0,
                           jnp.where((d >= 1) & (d <= 3), 2.0, 0.0)).astype(BF16)
        pick_ref[...] = (_split_dot_nt(wfar_ref[:, :w], imp_ref[:, :w])
                         + _split_dot_nt(w_near, imp_near))

    max_far_end = (ks_ref.shape[1] // TQ - 1) * cmp_per_tile
    candidates = sorted(set(CMP_WIDTHS) | {ncp})
    widths = [w for w in candidates if w < max_far_end]
    widths.append(min(w for w in candidates if w >= max_far_end))
    lo = -1
    for w in widths:
        pl.when((far_end > lo) & (far_end <= w))(functools.partial(compressed, w))
        lo = w
    o_c = oc_ref[...]
    p_slc = pick_ref[...]

    bj = lax.broadcasted_iota(jnp.int32, (n_sel, TQ), 0)
    ti = lax.broadcasted_iota(jnp.int32, (n_sel, TQ), 1)
    jt = blocks_per_tile * qt + ti // SEL_LEN
    forced = (bj == 0) | (bj == jt) | (bj == jt - 1)
    pick_ref[...] = jnp.where(forced, -jnp.inf, jnp.where(bj > jt, NEG, p_slc))
    idx = lax.broadcasted_iota(jnp.int32, (n_sel, LANES), 0).astype(F32)

    def pick(_, sc):
        mx = jnp.max(sc, axis=0, keepdims=True)
        first = jnp.min(jnp.where(sc == mx, idx, float(n_sel)), axis=0, keepdims=True)
        return jnp.where(idx == first, -jnp.inf, sc)

    for h in range(0, TQ, LANES):
        pick_ref[:, h:h + LANES] = lax.fori_loop(0, top_n - 3, pick, pick_ref[:, h:h + LANES])
    selneg_t = jnp.where(pick_ref[...] == -jnp.inf, 0.0, NEG)
    sel_ref[...] = selneg_t.T
    selneg = sel_ref[...]
    bj = lax.broadcasted_iota(jnp.int32, (TQ, n_sel), 1)

    nb0 = jnp.maximum(blocks_per_tile * (qt - 1), 0)
    pj = lax.broadcasted_iota(jnp.int32, (n_sel, LANES), 0)
    pu = lax.broadcasted_iota(jnp.int32, (n_sel, LANES), 1)
    perm_near = ((pj == nb0 + pu) & (pu < 2 * blocks_per_tile)).astype(BF16)
    sel_near = _dot(selneg.astype(BF16), perm_near).astype(BF16)
    sel_far = jnp.where(bj < blocks_per_tile * (qt - 1), selneg, NEG).astype(BF16)
    sel_all = _dot(sel_far, perm_ref[...]).astype(BF16)
    n_kt = selt_ref.shape[0] - 1
    for kt in range(n_kt):
        selt_ref[kt] = sel_all[:, kt * LANES:(kt + 1) * LANES]
    col = lax.broadcasted_iota(jnp.int32, (TQ, LANES), 1)
    selt_ref[n_kt] = jnp.where(col < NSA_TK // SEL_LEN, NEG, 0.0).astype(BF16)

    _init_state(m_ref, l_ref, acc_ref)
    near_keys = 2 * NSA_TQ
    ns = pl.multiple_of(nb0 * SEL_LEN, NSA_TQ)
    lhs = jnp.concatenate([q4_ref[...], jnp.concatenate([sel_near] * P, axis=0)], axis=1)
    rhs = jnp.concatenate([ks_ref[0, pl.ds(ns, near_keys), :], eblk_ref[:near_keys, :]], axis=1)
    s_ref[0, :, :near_keys] = _dot_nt(lhs, rhs) + ds_ref[0, 0].astype(F32)

    def far_scores(kt, sel_idx, slot):
        k0 = pl.multiple_of(kt * NSA_TK, NSA_TK)
        lhs_t = jnp.concatenate(
            [q4_ref[...], jnp.concatenate([selt_ref[sel_idx]] * P, axis=0)], axis=1)
        rhs_t = jnp.concatenate([ks_ref[0, pl.ds(k0, NSA_TK), :], eblk_ref[...]], axis=1)
        s_ref[slot, :, :NSA_TK] = _dot_nt(lhs_t, rhs_t)

    def far_softmax(slot):
        _softmax_tile(slot, NSA_TK, *stats, NSA_ROWS)

    def far_values(kt, slot):
        k0 = pl.multiple_of(kt * NSA_TK, NSA_TK)
        _value_tile(slot, NSA_TK, vs_ref[0, pl.ds(k0, NSA_TK), :], p_ref, a_ref, acc_ref)

    n_far = (qt + 2) // 4
    n_pairs = n_far // 2
    odd = jnp.maximum(n_far - 1, 0)
    odd_sel = jnp.where(n_far % 2 == 1, odd, selt_ref.shape[0] - 1)
    last_pair_tile = jnp.maximum(n_far - 2, 0)
    far_scores(odd, odd_sel, 1)
    _fold_tile(0, near_keys, vs_ref[0, pl.ds(ns, near_keys), :], *state, NSA_ROWS)
    far_scores(0, 0, 0)
    far_softmax(1)

    def far_pair(i, carry):
        a = 2 * i
        far_values(jnp.where(i == 0, odd, a - 1), 1)
        far_scores(a + 1, a + 1, 1)
        far_softmax(0)
        nxt = jnp.minimum(a + 2, last_pair_tile)
        far_values(a, 0)
        far_scores(nxt, nxt, 0)
        far_softmax(1)
        return carry

    lax.fori_loop(0, n_pairs, far_pair, 0)
    far_values(jnp.where(n_pairs == 0, odd, 2 * n_pairs - 1), 1)
    o_s = _finish(l_ref, acc_ref)

    _init_state(m_ref, l_ref, acc_ref)
    kw = jnp.concatenate([kw0_ref[0], kw1_ref[0], kw2_ref[0]], axis=0)
    vw = jnp.concatenate([vw0_ref[0], vw1_ref[0], vw2_ref[0]], axis=0)
    s_ref[0, :, :WINDOW + NSA_TQ] = _dot_nt(q4_ref[...], kw) + dw_ref[0, 0].astype(F32)
    _fold_tile(0, WINDOW + NSA_TQ, vw, *state, NSA_ROWS)
    o_w = _finish(l_ref, acc_ref)

    gate = gate_ref[0]
    outs = []
    for p in range(P):
        rows = slice(p * TQ, (p + 1) * TQ)
        gc = gate[:, 3 * p:3 * p + 1]
        gs = gate[:, 3 * p + 1:3 * p + 2]
        gw = gate[:, 3 * p + 2:3 * p + 3]
        outs.append(gc * o_c[rows] + gs * o_s[rows] + gw * o_w[rows])
    o_ref[0] = jnp.concatenate(outs, axis=1).astype(o_ref.dtype)


def _nsa(proj3, gates3, cmp_pad, dc, ds, dw, perm, eblk, wfar, n_sel):
    B, S, _ = proj3.shape
    G = NSA_GROUPS
    ncp = cmp_pad.shape[3]
    top_n = min(SEL_TOPK, n_sel)
    assert top_n > 3, "the selection loop assumes the three forced blocks fit in the top-n"
    qw = NSA_HPG * HEAD_DIM
    R = NSA_HPG * NSA_TQ
    wide = max(NSA_TK, ncp, WINDOW + NSA_TQ)
    wblk = lambda col, back: pl.BlockSpec(
        (1, NSA_TQ, HEAD_DIM),
        lambda b, g, t, col=col, back=back: (b, jnp.maximum(t - back, 0), col // HEAD_DIM + g))
    kernel = functools.partial(_nsa_kernel, n_sel=n_sel, top_n=top_n)
    return pl.pallas_call(
        kernel,
        grid=(B, G, S // NSA_TQ),
        in_specs=[
            pl.BlockSpec((1, NSA_TQ, qw), lambda b, g, t: (b, t, g)),
            pl.BlockSpec((1, NSA_TQ, LANES), lambda b, g, t: (b, t, g)),
            pl.BlockSpec((1, 1, 1, ncp, HEAD_DIM), lambda b, g, t: (0, b, g, 0, 0)),
            pl.BlockSpec((1, 1, 1, ncp, HEAD_DIM), lambda b, g, t: (1, b, g, 0, 0)),
            pl.BlockSpec((1, S, HEAD_DIM), lambda b, g, t: (b, 0, COL_KS // HEAD_DIM + g)),
            pl.BlockSpec((1, S, HEAD_DIM), lambda b, g, t: (b, 0, COL_VS // HEAD_DIM + g)),
            wblk(COL_KW, 2), wblk(COL_KW, 1), wblk(COL_KW, 0),
            wblk(COL_VW, 2), wblk(COL_VW, 1), wblk(COL_VW, 0),
            pl.BlockSpec((1, 1) + dc.shape[2:], lambda b, g, t: (jnp.minimum(t, 1), g, 0, 0)),
            pl.BlockSpec((1, 1) + ds.shape[2:], lambda b, g, t: (jnp.minimum(t, 1), g, 0, 0)),
            pl.BlockSpec((1, 1) + dw.shape[2:], lambda b, g, t: (jnp.minimum(t, 2), g, 0, 0)),
            pl.BlockSpec(perm.shape, lambda b, g, t: (0, 0)),
            pl.BlockSpec(eblk.shape, lambda b, g, t: (0, 0)),
            pl.BlockSpec(wfar.shape, lambda b, g, t: (0, 0)),
        ],
        out_specs=pl.BlockSpec((1, NSA_TQ, qw), lambda b, g, t: (b, t, g)),
        out_shape=jax.ShapeDtypeStruct((B, S, NSA_HEADS * HEAD_DIM), BF16),
        scratch_shapes=[
            pltpu.VMEM((2, R, wide), F32),
            pltpu.VMEM((2, R, wide), BF16),
            pltpu.VMEM((2, R, LANES), F32),
            pltpu.VMEM((R, LANES), F32),
            pltpu.VMEM((R, LANES), F32),
            pltpu.VMEM((R, HEAD_DIM), F32),
            pltpu.VMEM((NSA_TQ, ncp), F32),
            pltpu.VMEM((R, CMP_NEAR), F32),
            pltpu.VMEM((R, CMP_NEAR), F32),
            pltpu.VMEM((S // NSA_TK + 1, NSA_TQ, LANES), BF16),
            pltpu.VMEM((NSA_TQ, n_sel), F32),
            pltpu.VMEM((R, HEAD_DIM), BF16),
            pltpu.VMEM((n_sel, NSA_TQ), F32),
            pltpu.VMEM((R, HEAD_DIM), F32),
        ],
        compiler_params=_params("parallel", "parallel", "arbitrary"),
    )(proj3, gates3, cmp_pad, cmp_pad, proj3, proj3,
      proj3, proj3, proj3, proj3, proj3, proj3, dc, ds, dw, perm, eblk, wfar)


def _diff_kernel(q_ref, k_ref, v_ref, dd_ref, lq1_ref, lk1_ref, lq2_ref, lk2_ref, hg_ref,
                 o_ref, s_ref, p_ref, m_ref, l_ref, acc_ref, *, lam_init):
    qt = pl.program_id(2)
    TQ = DIFF_TQ
    q = q_ref[0]
    zero = jnp.zeros((TQ, HEAD_DIM), BF16)
    lhs = jnp.concatenate([
        jnp.concatenate([q[:, :HEAD_DIM], zero], axis=1),
        jnp.concatenate([zero, q[:, HEAD_DIM:]], axis=1)], axis=0)
    lam = (jnp.exp(jnp.sum(lq1_ref[...] * lk1_ref[...], axis=-1, keepdims=True))
           - jnp.exp(jnp.sum(lq2_ref[...] * lk2_ref[...], axis=-1, keepdims=True)) + lam_init)

    def scores(kt, slot, bias):
        k0 = pl.multiple_of(kt * DIFF_TK, DIFF_TK)
        sc = _dot_nt(lhs, k_ref[0, pl.ds(k0, DIFF_TK), :])
        if bias is not None:
            sc = (sc.reshape(2, TQ, DIFF_TK) + bias[None]).reshape(2 * TQ, DIFF_TK)
        s_ref[slot] = sc

    def fold(kt, slot):
        k0 = pl.multiple_of(kt * DIFF_TK, DIFF_TK)
        _fold_tile_inplace(slot, DIFF_TK, v_ref[0, pl.ds(k0, DIFF_TK), :],
                           s_ref, p_ref, m_ref, l_ref, acc_ref, DIFF_ROWS)

    _init_state(m_ref, l_ref, acc_ref)
    n_far = jnp.maximum(qt - 1, 0)
    prev = jnp.maximum(qt - 1, 0)
    odd = jnp.maximum(n_far - 1, 0)
    last_pair_tile = jnp.maximum(n_far - 2, 0)
    scores(qt, 0, dd_ref[0, :, DIFF_TK:].astype(F32))
    fold(qt, 0)
    scores(prev, 0, jnp.where(qt >= 1, dd_ref[0, :, :DIFF_TK].astype(F32), NEG))
    fold(prev, 0)
    scores(odd, 0, jnp.where(n_far % 2 == 1, jnp.zeros((TQ, DIFF_TK), F32), NEG))
    scores(0, 1, None)
    fold(odd, 0)

    def far_pair(i, carry):
        a = 2 * i
        scores(a + 1, 0, None)
        fold(a, 1)
        scores(jnp.minimum(a + 2, last_pair_tile), 1, None)
        fold(a + 1, 0)
        return carry

    lax.fori_loop(0, n_far // 2, far_pair, 0)
    a = _finish(l_ref, acc_ref)
    o = a[:TQ] - lam * a[TQ:]
    y = _norm_rows(o, hg_ref[0]) * (1.0 - lam_init)
    o_ref[0] = y.astype(o_ref.dtype)


def _diff(proj3, dd, lq1, lk1, lq2, lk2, hg, lam_init):
    B, S, _ = proj3.shape
    H = DIFF_HEADS
    w = 2 * HEAD_DIM
    R = 2 * DIFF_TQ
    vec = pl.BlockSpec((1, HEAD_DIM), lambda b, h, t: (0, 0))
    kernel = functools.partial(_diff_kernel, lam_init=lam_init)
    return pl.pallas_call(
        kernel,
        grid=(B, H, S // DIFF_TQ),
        in_specs=[
            pl.BlockSpec((1, DIFF_TQ, w), lambda b, h, t: (b, t, COL_QD // w + h)),
            pl.BlockSpec((1, S, w), lambda b, h, t: (b, 0, COL_KD // w + h)),
            pl.BlockSpec((1, S, w), lambda b, h, t: (b, 0, COL_VD // w + h)),
            pl.BlockSpec((1,) + dd.shape[1:], lambda b, h, t: (h, 0, 0)),
            vec, vec, vec, vec,
            pl.BlockSpec((1, 1, w), lambda b, h, t: (h, 0, 0)),
        ],
        out_specs=pl.BlockSpec((1, DIFF_TQ, w), lambda b, h, t: (b, t, h)),
        out_shape=jax.ShapeDtypeStruct((B, S, H * w), BF16),
        scratch_shapes=[
            pltpu.VMEM((2, R, DIFF_TK), F32),
            pltpu.VMEM((1, R, DIFF_TK), BF16),
            pltpu.VMEM((R, LANES), F32),
            pltpu.VMEM((R, LANES), F32),
            pltpu.VMEM((R, DIFF_V_DIM), F32),
        ],
        compiler_params=_params("parallel", "parallel", "arbitrary"),
    )(proj3, proj3, proj3, dd, lq1, lk1, lq2, lk2, hg)


def _merge_kernel(on_ref, od_ref, wn_ref, wd_ref, ga_ref, gb_ref, o_ref):
    u1 = _dot(on_ref[...], wn_ref[...])
    u2 = _dot(od_ref[...], wd_ref[...])
    ga = jax.nn.sigmoid(ga_ref[...].astype(F32))
    gb = jax.nn.sigmoid(gb_ref[...].astype(F32))
    o_ref[...] = (ga * u1 + gb * u2).astype(o_ref.dtype)


def _merge(o_nsa, o_d, wn, wd, proj, tm=1024, tn=512):
    T, kn = o_nsa.shape
    kd = o_d.shape[1]
    return pl.pallas_call(
        _merge_kernel,
        grid=(T // tm, D_MODEL // tn),
        in_specs=[
            pl.BlockSpec((tm, kn), lambda i, j: (i, 0)),
            pl.BlockSpec((tm, kd), lambda i, j: (i, 0)),
            pl.BlockSpec((kn, tn), lambda i, j: (0, j)),
            pl.BlockSpec((kd, tn), lambda i, j: (0, j)),
            pl.BlockSpec((tm, tn), lambda i, j: (i, COL_GM // tn + j)),
            pl.BlockSpec((tm, tn), lambda i, j: (i, (COL_GM + D_MODEL) // tn + j)),
        ],
        out_specs=pl.BlockSpec((tm, tn), lambda i, j: (i, j)),
        out_shape=jax.ShapeDtypeStruct((T, D_MODEL), BF16),
        compiler_params=_params("parallel", "arbitrary"),
    )(o_nsa, o_d, wn, wd, proj, proj)


def _out_proj_kernel(a_ref, w_ref, x_ref, o_ref):
    o_ref[...] = x_ref[...] + _dot(a_ref[...], w_ref[...])


def _out_proj(a, w, x2, tm=512, tn=2048):
    T, k = a.shape
    return pl.pallas_call(
        _out_proj_kernel,
        grid=(T // tm, D_MODEL // tn),
        in_specs=[
            pl.BlockSpec((tm, k), lambda i, j: (i, 0)),
            pl.BlockSpec((k, tn), lambda i, j: (0, j)),
            pl.BlockSpec((tm, tn), lambda i, j: (i, j)),
        ],
        out_specs=pl.BlockSpec((tm, tn), lambda i, j: (i, j)),
        out_shape=jax.ShapeDtypeStruct((T, D_MODEL), F32),
        compiler_params=_params("parallel", "arbitrary"),
    )(a, w, x2)


def _ffn_in_kernel(x_ref, g_ref, wa_ref, wb_ref, o_ref, h_ref):
    @pl.when(pl.program_id(1) == 0)
    def _():
        h_ref[...] = _norm_rows(x_ref[...], g_ref[...]).astype(BF16)

    h = h_ref[...]
    a = _dot(h, wa_ref[...])
    b = _dot(h, wb_ref[...])
    o_ref[...] = (a * jax.nn.sigmoid(a) * b).astype(o_ref.dtype)


def _ffn_in(x2, g, w, tm=1024, tn=512):
    T = x2.shape[0]
    nb = D_FF // tn
    return pl.pallas_call(
        _ffn_in_kernel,
        grid=(T // tm, nb),
        in_specs=[
            pl.BlockSpec((tm, D_MODEL), lambda i, j: (i, 0)),
            pl.BlockSpec((1, D_MODEL), lambda i, j: (0, 0)),
            pl.BlockSpec((D_MODEL, tn), lambda i, j: (0, j)),
            pl.BlockSpec((D_MODEL, tn), lambda i, j: (0, j + nb)),
        ],
        out_specs=pl.BlockSpec((tm, tn), lambda i, j: (i, j)),
        out_shape=jax.ShapeDtypeStruct((T, D_FF), BF16),
        scratch_shapes=[pltpu.VMEM((tm, D_MODEL), BF16)],
        compiler_params=_params("parallel", "arbitrary"),
    )(x2, g, w, w)


def _ffn_out_kernel(a_ref, w_ref, x_ref, g_ref, o_ref, acc_ref, *, final_norm):
    k = pl.program_id(1)

    @pl.when(k == 0)
    def _():
        acc_ref[...] = x_ref[...]

    acc_ref[...] += _dot(a_ref[...], w_ref[...])

    @pl.when(k == pl.num_programs(1) - 1)
    def _():
        y = acc_ref[...]
        o_ref[...] = _norm_rows(y, g_ref[...]) if final_norm else y


def _ffn_out(a, w, x2, g, final_norm, tm=512, tk=2816):
    T = a.shape[0]
    kernel = functools.partial(_ffn_out_kernel, final_norm=final_norm)
    return pl.pallas_call(
        kernel,
        grid=(T // tm, D_FF // tk),
        in_specs=[
            pl.BlockSpec((tm, tk), lambda i, k: (i, k)),
            pl.BlockSpec((tk, D_MODEL), lambda i, k: (k, 0)),
            pl.BlockSpec((tm, D_MODEL), lambda i, k: (i, 0)),
            pl.BlockSpec((1, D_MODEL), lambda i, k: (0, 0)),
        ],
        out_specs=pl.BlockSpec((tm, D_MODEL), lambda i, k: (i, 0)),
        out_shape=jax.ShapeDtypeStruct((T, D_MODEL), F32),
        scratch_shapes=[pltpu.VMEM((tm, D_MODEL), F32)],
        compiler_params=_params("parallel", "arbitrary"),
    )(a, w, x2, g)


def _t5_bucket(rel):
    n = jnp.maximum(rel, 0)
    nf = jnp.maximum(n, 1).astype(F32)
    large = REL_MAX_EXACT + (jnp.log(nf / REL_MAX_EXACT) / math.log(REL_MAX_DIST / REL_MAX_EXACT)
                             * (REL_BUCKETS - REL_MAX_EXACT)).astype(jnp.int32)
    large = jnp.minimum(large, REL_BUCKETS - 1)
    return jnp.where(n < REL_MAX_EXACT, n, large)


def _bias_of_rel(table, rel, valid):
    shifted = (table - table[REL_BUCKETS - 1:REL_BUCKETS]) * LOG2E
    vals = jnp.moveaxis(shifted[_t5_bucket(rel)], -1, 0)
    return jnp.where(valid[None], vals, NEG)


def _toeplitz_bias(table, n_i, n_j, offset, max_rel=None):
    period = REL_MAX_DIST
    near = jnp.arange(period, dtype=jnp.int32)
    g = _bias_of_rel(table, near, near >= 0)
    w = jnp.roll(g[:, ::-1], (offset + 1) % period, axis=1)
    circ = jnp.tile(w, (1, 2 * period))[:, :period * (2 * period - 1)]
    circ = circ.reshape(-1, period, 2 * period - 1)[:, :, :period]
    band = jnp.tile(circ, (1, n_i // period, n_j // period))
    i = jnp.arange(n_i, dtype=jnp.int32)[:, None]
    j = jnp.arange(n_j, dtype=jnp.int32)[None, :]
    rel = (i - j + offset)[None]
    far = 0.0 if max_rel is None else jnp.where(rel < max_rel, 0.0, NEG)
    return jnp.where(rel < 0, NEG, jnp.where(rel < period, band, far))


def _nsa_tables(table_nsa):
    TQ = NSA_TQ

    def stack(t):
        return t.reshape(NSA_GROUPS, NSA_HPG * TQ, t.shape[-1])

    i = jnp.arange(TQ, dtype=jnp.int32)[:, None]
    jc = jnp.arange(CMP_NEAR, dtype=jnp.int32)[None, :]
    rel_c = i - CMP_STRIDE * (jc - CMP_PAD_FRONT) - (CMP_LEN - 1)
    dc = jnp.stack([
        stack(_bias_of_rel(table_nsa, rel_c, (rel_c >= 0) & (jc >= CMP_PAD_FRONT))),
        stack(_bias_of_rel(table_nsa, rel_c, rel_c >= 0)),
    ])
    ds = jnp.stack([
        stack(_toeplitz_bias(table_nsa, TQ, 2 * TQ, 0)),
        stack(_toeplitz_bias(table_nsa, TQ, 2 * TQ, TQ)),
    ]).astype(BF16)
    win = stack(_toeplitz_bias(table_nsa, TQ, WINDOW + TQ, WINDOW, WINDOW))
    jw = jnp.arange(WINDOW + TQ, dtype=jnp.int32)[None, None, :]
    dw = jnp.stack([jnp.where(jw >= WINDOW - c * TQ, win, NEG) for c in range(3)]).astype(BF16)
    return dc, ds, dw


def _diff_table(table_diff):
    return _toeplitz_bias(table_diff, DIFF_TQ, 2 * DIFF_TK, DIFF_TK).astype(BF16)


def _selection_constants(S):
    n_sel = S // SEL_LEN
    n_cmp = S // CMP_STRIDE
    n_kt = S // NSA_TK
    blocks = NSA_TK // SEL_LEN
    j = jnp.arange(n_sel, dtype=jnp.int32)[:, None]
    col = jnp.arange(n_kt * LANES, dtype=jnp.int32)[None, :]
    perm = ((j == blocks * (col // LANES) + col % LANES) & (col % LANES < blocks)).astype(BF16)
    key = jnp.arange(NSA_TK, dtype=jnp.int32)[:, None]
    u = jnp.arange(LANES, dtype=jnp.int32)[None, :]
    eblk = (u == key // SEL_LEN).astype(BF16)
    c = jnp.arange(n_cmp + CMP_PAD_FRONT + CMP_PAD_BACK, dtype=jnp.int32)[None, :] - CMP_PAD_FRONT
    jb = jnp.arange(n_sel, dtype=jnp.int32)[:, None]
    d = c - (SEL_LEN // CMP_STRIDE) * jb
    wfar = jnp.where((d == 0) | (d == 4), 1.0, jnp.where((d >= 1) & (d <= 3), 2.0, 0.0))
    wfar = jnp.where((c >= 0) & (c < n_cmp), wfar, 0.0).astype(BF16)
    return perm, eblk, wfar


def kernel(x, norm_mix_g, w_in, cmp_pe_k, cmp_pe_v, cmp_w1_k, cmp_w2_k, cmp_w1_v, cmp_w2_v,
           diff_lq1, diff_lk1, diff_lq2, diff_lk2, diff_head_g, w_up_nsa, w_up_diff, w_out,
           norm_ff_g, w_ff_in, w_ff_out, rel_bias_table, norm_final_g):
    B, S, D = x.shape
    T = B * S
    depth = w_in.shape[0]
    n_sel = S // SEL_LEN
    n_cmp = S // CMP_STRIDE
    qscale = HEAD_DIM ** -0.5 * LOG2E

    dc, ds, dw = _nsa_tables(rel_bias_table[:, :NSA_HEADS])
    dd = _diff_table(rel_bias_table[:, NSA_HEADS:])
    perm, eblk, wfar = _selection_constants(S)

    gn0 = COL_QD
    gn1 = gn0 + NSA_HEADS * 3
    cs = jnp.ones((1, N_PROJ), F32)
    cs = cs.at[:, COL_QN:COL_KC].set(qscale).at[:, COL_QD:COL_KD].set(qscale)

    x2 = x.reshape(T, D)
    for l in range(depth):
        w_main = jnp.concatenate([w_in[l][:, :gn0], w_in[l][:, gn1:]], axis=1).astype(BF16)
        wg = w_in[l][:, gn0:gn1].reshape(D, NSA_GROUPS, NSA_HPG * 3)
        wg = jnp.pad(wg, ((0, 0), (0, 0), (0, LANES - NSA_HPG * 3))).reshape(D, NSA_GROUPS * LANES)
        proj, gates = _in_proj(x2, norm_mix_g[l][None], w_main, cs, wg.astype(BF16))
        proj3 = proj.reshape(B, S, N_PROJ)
        gates3 = gates.reshape(B, S, NSA_GROUPS * LANES)

        halves = proj3[:, :, COL_KC:COL_KS].reshape(B, n_cmp, CMP_STRIDE, 2, NSA_GROUPS, HEAD_DIM)
        halves = halves.transpose(3, 0, 4, 1, 2, 5).reshape(
            2, B, NSA_GROUPS, n_cmp, CMP_STRIDE * HEAD_DIM)
        pe = jnp.stack([cmp_pe_k[l], cmp_pe_v[l]]).reshape(2, 1, CMP_LEN * HEAD_DIM)
        w1 = jnp.stack([cmp_w1_k[l], cmp_w1_v[l]]).astype(BF16)
        w2 = jnp.stack([cmp_w2_k[l], cmp_w2_v[l]]).astype(BF16)
        cmp_pad = _compress(halves, pe, w1, w2)

        o_nsa = _nsa(proj3, gates3, cmp_pad, dc, ds, dw, perm, eblk, wfar, n_sel)

        lam_init = 0.8 - 0.6 * math.exp(-0.3 * l)
        o_d = _diff(proj3, dd, diff_lq1[l][None], diff_lk1[l][None], diff_lq2[l][None],
                    diff_lk2[l][None], diff_head_g[l][:, None, :], lam_init)

        mix = _merge(o_nsa.reshape(T, -1), o_d.reshape(T, -1),
                     w_up_nsa[l].astype(BF16), w_up_diff[l].astype(BF16), proj)
        x2 = _out_proj(mix, w_out[l].astype(BF16), x2)

        act = _ffn_in(x2, norm_ff_g[l][None], w_ff_in[l].astype(BF16))
        last = l == depth - 1
        x2 = _ffn_out(act, w_ff_out[l].astype(BF16), x2,
                      norm_final_g[None] if last else norm_ff_g[l][None], final_norm=last)
    return x2.reshape(B, S, D)
```

```python
import functools
import math

import jax
import jax.numpy as jnp
from jax import lax
from jax.experimental import pallas as pl
from jax.experimental.pallas import tpu as pltpu

D_MODEL = 2048
HEAD_DIM = 128
NSA_HEADS = 8
NSA_GROUPS = 2
NSA_HPG = NSA_HEADS // NSA_GROUPS
CMP_LEN = 32
CMP_STRIDE = 16
SEL_LEN = 64
SEL_TOPK = 16
WINDOW = 512
CMP_HIDDEN = 256
DIFF_HEADS = 4
DIFF_V_DIM = 2 * HEAD_DIM
REL_BUCKETS = 32
REL_MAX_EXACT = 16
REL_MAX_DIST = 128
D_FF = -(-8 * D_MODEL // (3 * 256)) * 256
EPS = 1e-6
NEG = -1e30
BIG = 1e30
LOG2E = math.log2(math.e)

F32 = jnp.float32
BF16 = jnp.bfloat16
LANES = 128

VMEM_LIMIT_BYTES = 56 * 1024 * 1024

COL_QN = 0
COL_KC = 1024
COL_KS = 1536
COL_VS = 1792
COL_KW = 2048
COL_VW = 2304
COL_QD = 2560
COL_KD = 3584
COL_VD = 4608
COL_GM = 5632
N_PROJ = 9728

NSA_TQ = 256
NSA_TK = 1024
NSA_ROWS = 32
CMP_NEAR = 32
CMP_WIDTHS = (256, 512, 768, 1024)
CMP_PAD_FRONT = 16
CMP_PAD_BACK = 112
DIFF_TQ = 512
DIFF_TK = 512
DIFF_ROWS = 64


def _dot(a, b):
    return jnp.dot(a, b, preferred_element_type=F32)


def _dot_nt(a, b):
    return lax.dot_general(a, b, (((1,), (1,)), ((), ())), preferred_element_type=F32)


def _params(*sem):
    return pltpu.CompilerParams(dimension_semantics=sem, vmem_limit_bytes=VMEM_LIMIT_BYTES)


def _norm_rows(x, g):
    ms = jnp.mean(x * x, axis=-1, keepdims=True)
    return x * lax.rsqrt(ms + EPS) * g


def _in_proj_kernel(x_ref, g_ref, w_ref, cs_ref, wg_ref, o_ref, og_ref, h_ref):
    @pl.when(pl.program_id(1) == 0)
    def _():
        hb = _norm_rows(x_ref[...], g_ref[...]).astype(BF16)
        h_ref[...] = hb
        og_ref[...] = jax.nn.sigmoid(_dot(hb, wg_ref[...]))

    o_ref[...] = (_dot(h_ref[...], w_ref[...]) * cs_ref[...]).astype(o_ref.dtype)


def _in_proj(x2, g, w, cs, wg, tm=512, tn=2432):
    T = x2.shape[0]
    n = w.shape[1]
    ng = wg.shape[1]
    return pl.pallas_call(
        _in_proj_kernel,
        grid=(T // tm, n // tn),
        in_specs=[
            pl.BlockSpec((tm, D_MODEL), lambda i, j: (i, 0)),
            pl.BlockSpec((1, D_MODEL), lambda i, j: (0, 0)),
            pl.BlockSpec((D_MODEL, tn), lambda i, j: (0, j)),
            pl.BlockSpec((1, tn), lambda i, j: (0, j)),
            pl.BlockSpec((D_MODEL, ng), lambda i, j: (0, 0)),
        ],
        out_specs=[
            pl.BlockSpec((tm, tn), lambda i, j: (i, j)),
            pl.BlockSpec((tm, ng), lambda i, j: (i, 0)),
        ],
        out_shape=[
            jax.ShapeDtypeStruct((T, n), BF16),
            jax.ShapeDtypeStruct((T, ng), F32),
        ],
        scratch_shapes=[pltpu.VMEM((tm, D_MODEL), BF16)],
        compiler_params=_params("parallel", "arbitrary"),
    )(x2, g, w, cs, wg)


def _gelu_tanh(x):
    return 0.5 * x * (1.0 + jnp.tanh(math.sqrt(2.0 / math.pi) * (x + 0.044715 * (x * x * x))))


def _compress_kernel(h_ref, pe_ref, w1_ref, w2_ref, o_ref):
    hv = h_ref[0, 0, 0]
    nc = hv.shape[0]
    half = CMP_STRIDE * HEAD_DIM
    ya = _dot(hv, w1_ref[0, :half, :])
    yb = _dot(hv, w1_ref[0, half:, :])
    yb = pltpu.roll(yb, nc - 1, 0)
    row = lax.broadcasted_iota(jnp.int32, yb.shape, 0)
    yb = jnp.where(row == nc - 1, 0.0, yb)
    pe8 = jnp.broadcast_to(pe_ref[0], (8, 2 * half)).astype(BF16)
    pec = _dot(pe8, w1_ref[0])[0:1]
    hid = _gelu_tanh(ya + yb + pec)
    o_ref[0, 0, 0, :CMP_PAD_FRONT] = jnp.zeros((CMP_PAD_FRONT, HEAD_DIM), o_ref.dtype)
    o_ref[0, 0, 0, CMP_PAD_FRONT:CMP_PAD_FRONT + nc] = (
        _dot(hid.astype(BF16), w2_ref[0]).astype(o_ref.dtype))
    o_ref[0, 0, 0, CMP_PAD_FRONT + nc:] = jnp.zeros((CMP_PAD_BACK, HEAD_DIM), o_ref.dtype)


def _compress(halves, pe, w1, w2):
    _, B, G, nc, hw = halves.shape
    ncp = CMP_PAD_FRONT + nc + CMP_PAD_BACK
    return pl.pallas_call(
        _compress_kernel,
        grid=(2, B, G),
        in_specs=[
            pl.BlockSpec((1, 1, 1, nc, hw), lambda s, b, g: (s, b, g, 0, 0)),
            pl.BlockSpec((1, 1, 2 * hw), lambda s, b, g: (s, 0, 0)),
            pl.BlockSpec((1, 2 * hw, CMP_HIDDEN), lambda s, b, g: (s, 0, 0)),
            pl.BlockSpec((1, CMP_HIDDEN, HEAD_DIM), lambda s, b, g: (s, 0, 0)),
        ],
        out_specs=pl.BlockSpec((1, 1, 1, ncp, HEAD_DIM), lambda s, b, g: (s, b, g, 0, 0)),
        out_shape=jax.ShapeDtypeStruct((2, B, G, ncp, HEAD_DIM), BF16),
        compiler_params=_params("parallel", "parallel", "parallel"),
    )(halves, pe, w1, w2)


def _lanes(x, width):
    return jnp.concatenate([x] * (width // x.shape[1]), axis=1)


def _init_state(m_ref, l_ref, acc_ref):
    m_ref[...] = jnp.full(m_ref.shape, NEG, F32)
    l_ref[...] = jnp.zeros(l_ref.shape, F32)
    acc_ref[...] = jnp.zeros(acc_ref.shape, F32)


def _softmax_tile(slot, width, s_ref, p_ref, a_ref, m_ref, l_ref, chunk):
    n_rows = p_ref.shape[1]
    for r in range(0, n_rows, chunk):
        rows = slice(r, r + chunk)
        m_prev = m_ref[rows, :]
        m_new = jnp.maximum(m_prev, jnp.max(s_ref[slot, rows, :width], axis=-1, keepdims=True))
        a_ref[slot, rows, :] = jnp.exp2(m_prev - m_new)
        m_ref[rows, :] = m_new
    for r in range(0, n_rows, chunk):
        rows = slice(r, r + chunk)
        p = jnp.exp2(s_ref[slot, rows, :width] - _lanes(m_ref[rows, :], width))
        l_ref[rows, :] = (a_ref[slot, rows, :] * l_ref[rows, :]
                          + jnp.sum(p, axis=-1, keepdims=True))
        p_ref[slot, rows, :width] = p.astype(BF16)


def _value_tile(slot, width, v, p_ref, a_ref, acc_ref):
    acc_ref[...] = (acc_ref[...] * _lanes(a_ref[slot], acc_ref.shape[1])
                    + _dot(p_ref[slot, :, :width], v))


def _fold_tile(slot, width, v, s_ref, p_ref, a_ref, m_ref, l_ref, acc_ref, chunk):
    _softmax_tile(slot, width, s_ref, p_ref, a_ref, m_ref, l_ref, chunk)
    _value_tile(slot, width, v, p_ref, a_ref, acc_ref)


def _fold_tile_inplace(slot, width, v, s_ref, p_ref, m_ref, l_ref, acc_ref, chunk):
    n_rows = p_ref.shape[1]
    for r in range(0, n_rows, chunk):
        rows = slice(r, r + chunk)
        s = s_ref[slot, rows, :width]
        m_prev = m_ref[rows, :]
        m_new = jnp.maximum(m_prev, jnp.max(s, axis=-1, keepdims=True))
        alpha = jnp.exp2(m_prev - m_new)
        p = jnp.exp2(s - _lanes(m_new, width))
        l_ref[rows, :] = alpha * l_ref[rows, :] + jnp.sum(p, axis=-1, keepdims=True)
        m_ref[rows, :] = m_new
        acc_ref[rows, :] = acc_ref[rows, :] * _lanes(alpha, acc_ref.shape[1])
        p_ref[0, rows, :width] = p.astype(BF16)
    acc_ref[...] += _dot(p_ref[0, :, :width], v)


def _finish(l_ref, acc_ref):
    return acc_ref[...] / _lanes(l_ref[...], acc_ref.shape[1])


def _split_dot_nt(w, a):
    hi = a.astype(BF16)
    lo = (a - hi.astype(F32)).astype(BF16)
    return _dot_nt(w, hi) + _dot_nt(w, lo)


def _nsa_kernel(q_ref, gate_ref, kc_ref, vc_ref, ks_ref, vs_ref,
                kw0_ref, kw1_ref, kw2_ref, vw0_ref, vw1_ref, vw2_ref,
                dc_ref, ds_ref, dw_ref, perm_ref, eblk_ref, wfar_ref, o_ref,
                s_ref, p_ref, a_ref, m_ref, l_ref, acc_ref, imp_ref, sn_ref, pn_ref, selt_ref,
                sel_ref, q4_ref, pick_ref, oc_ref,
                *, n_sel, top_n):
    qt = pl.program_id(2)
    TQ, P = NSA_TQ, NSA_HPG
    R = P * TQ
    blocks_per_tile = NSA_TQ // SEL_LEN
    stats = (s_ref, p_ref, a_ref, m_ref, l_ref)
    state = stats + (acc_ref,)
    qblk = q_ref[0]
    q4_ref[...] = jnp.concatenate(
        [qblk[:, p * HEAD_DIM:(p + 1) * HEAD_DIM] for p in range(P)], axis=0)

    ncp = kc_ref.shape[3]
    cmp_per_tile = NSA_TQ // CMP_STRIDE
    far_end = cmp_per_tile * qt
    c0 = pl.multiple_of(far_end, cmp_per_tile)

    def compressed(w):
        lane = lax.broadcasted_iota(jnp.int32, (1, w), 1)
        far_bias = jnp.where((lane >= CMP_PAD_FRONT) & (lane < far_end), 0.0, NEG)
        kn = kc_ref[0, 0, 0, pl.ds(c0, CMP_NEAR), :]
        vn = vc_ref[0, 0, 0, pl.ds(c0, CMP_NEAR), :]
        s_ref[0, :, :w] = _dot_nt(q4_ref[...], kc_ref[0, 0, 0, :w, :])
        sn_ref[...] = _dot_nt(q4_ref[...], kn) + dc_ref[0, 0]
        imp_ref[:, :w] = jnp.zeros((TQ, w), F32)

        def cmp_head(p, carry):
            for c in range(TQ // NSA_ROWS):
                rows = pl.ds(pl.multiple_of(p * TQ + c * NSA_ROWS, NSA_ROWS), NSA_ROWS)
                irows = slice(c * NSA_ROWS, (c + 1) * NSA_ROWS)
                sf = s_ref[0, rows, :w] + far_bias
                sn = sn_ref[rows, :]
                m = jnp.maximum(jnp.max(sf, axis=-1, keepdims=True),
                                jnp.max(sn, axis=-1, keepdims=True))
                ef = jnp.exp2(sf - m)
                en = jnp.exp2(sn - m)
                l = jnp.sum(ef, axis=-1, keepdims=True) + jnp.sum(en, axis=-1, keepdims=True)
                inv = jnp.where(m > 0.5 * NEG, 1.0 / l, 0.0)
                pf = ef * inv
                p_ref[0, rows, :w] = pf.astype(BF16)
                pn_ref[rows, :] = en * inv
                imp_ref[irows, :w] += pf
            return carry

        lax.fori_loop(0, P, cmp_head, 0)
        p_near = pn_ref[...]
        oc_ref[...] = (_dot(p_ref[0, :, :w], vc_ref[0, 0, 0, :w, :])
                       + _dot(p_near.astype(BF16), vn))

        imp_near = jnp.sum(p_near.reshape(P, TQ, CMP_NEAR), axis=0)
        jj = lax.broadcasted_iota(jnp.int32, (n_sel, CMP_NEAR), 0)
        jn = lax.broadcasted_iota(jnp.int32, (n_sel, CMP_NEAR), 1)
        d = (far_end - CMP_PAD_FRONT + jn) - (SEL_LEN // CMP_STRIDE) * jj
        w_near = jnp.where((d == 0) | (d == 4), 1.0,
                           jnp.where((d >= 1) & (d <= 3), 2.0, 0.0)).astype(BF16)
        pick_ref[...] = (_split_dot_nt(wfar_ref[:, :w], imp_ref[:, :w])
                         + _split_dot_nt(w_near, imp_near))

    max_far_end = (ks_ref.shape[1] // TQ - 1) * cmp_per_tile
    candidates = sorted(set(CMP_WIDTHS) | {ncp})
    widths = [w for w in candidates if w < max_far_end]
    widths.append(min(w for w in candidates if w >= max_far_end))
    lo = -1
    for w in widths:
        pl.when((far_end > lo) & (far_end <= w))(functools.partial(compressed, w))
        lo = w
    o_c = oc_ref[...]
    p_slc = pick_ref[...]

    bj = lax.broadcasted_iota(jnp.int32, (n_sel, TQ), 0)
    ti = lax.broadcasted_iota(jnp.int32, (n_sel, TQ), 1)
    jt = blocks_per_tile * qt + ti // SEL_LEN
    forced = (bj == 0) | (bj == jt) | (bj == jt - 1)
    pick_ref[...] = jnp.where(forced, -jnp.inf, jnp.where(bj > jt, NEG, p_slc))
    idx = lax.broadcasted_iota(jnp.int32, (n_sel, LANES), 0).astype(F32)

    def pick(_, sc):
        mx = jnp.max(sc, axis=0, keepdims=True)
        first = jnp.min(jnp.where(sc == mx, idx, float(n_sel)), axis=0, keepdims=True)
        return jnp.where(idx == first, -jnp.inf, sc)

    for h in range(0, TQ, LANES):
        pick_ref[:, h:h + LANES] = lax.fori_loop(0, top_n - 3, pick, pick_ref[:, h:h + LANES])
    selneg_t = jnp.where(pick_ref[...] == -jnp.inf, 0.0, NEG)
    sel_ref[...] = selneg_t.T
    selneg = sel_ref[...]
    bj = lax.broadcasted_iota(jnp.int32, (TQ, n_sel), 1)

    nb0 = jnp.maximum(blocks_per_tile * (qt - 1), 0)
    pj = lax.broadcasted_iota(jnp.int32, (n_sel, LANES), 0)
    pu = lax.broadcasted_iota(jnp.int32, (n_sel, LANES), 1)
    perm_near = ((pj == nb0 + pu) & (pu < 2 * blocks_per_tile)).astype(BF16)
    sel_near = _dot(selneg.astype(BF16), perm_near).astype(BF16)
    sel_far = jnp.where(bj < blocks_per_tile * (qt - 1), selneg, NEG).astype(BF16)
    sel_all = _dot(sel_far, perm_ref[...]).astype(BF16)
    n_kt = selt_ref.shape[0] - 1
    for kt in range(n_kt):
        selt_ref[kt] = sel_all[:, kt * LANES:(kt + 1) * LANES]
    col = lax.broadcasted_iota(jnp.int32, (TQ, LANES), 1)
    selt_ref[n_kt] = jnp.where(col < NSA_TK // SEL_LEN, NEG, 0.0).astype(BF16)

    _init_state(m_ref, l_ref, acc_ref)
    near_keys = 2 * NSA_TQ
    ns = pl.multiple_of(nb0 * SEL_LEN, NSA_TQ)
    lhs = jnp.concatenate([q4_ref[...], jnp.concatenate([sel_near] * P, axis=0)], axis=1)
    rhs = jnp.concatenate([ks_ref[0, pl.ds(ns, near_keys), :], eblk_ref[:near_keys, :]], axis=1)
    s_ref[0, :, :near_keys] = _dot_nt(lhs, rhs) + ds_ref[0, 0].astype(F32)

    def far_scores(kt, sel_idx, slot):
        k0 = pl.multiple_of(kt * NSA_TK, NSA_TK)
        lhs_t = jnp.concatenate(
            [q4_ref[...], jnp.concatenate([selt_ref[sel_idx]] * P, axis=0)], axis=1)
        rhs_t = jnp.concatenate([ks_ref[0, pl.ds(k0, NSA_TK), :], eblk_ref[...]], axis=1)
        s_ref[slot, :, :NSA_TK] = _dot_nt(lhs_t, rhs_t)

    def far_softmax(slot):
        _softmax_tile(slot, NSA_TK, *stats, NSA_ROWS)

    def far_values(kt, slot):
        k0 = pl.multiple_of(kt * NSA_TK, NSA_TK)
        _value_tile(slot, NSA_TK, vs_ref[0, pl.ds(k0, NSA_TK), :], p_ref, a_ref, acc_ref)

    n_far = (qt + 2) // 4
    n_pairs = n_far // 2
    odd = jnp.maximum(n_far - 1, 0)
    odd_sel = jnp.where(n_far % 2 == 1, odd, selt_ref.shape[0] - 1)
    last_pair_tile = jnp.maximum(n_far - 2, 0)
    far_scores(odd, odd_sel, 1)
    _fold_tile(0, near_keys, vs_ref[0, pl.ds(ns, near_keys), :], *state, NSA_ROWS)
    far_scores(0, 0, 0)
    far_softmax(1)

    def far_pair(i, carry):
        a = 2 * i
        far_values(jnp.where(i == 0, odd, a - 1), 1)
        far_scores(a + 1, a + 1, 1)
        far_softmax(0)
        nxt = jnp.minimum(a + 2, last_pair_tile)
        far_values(a, 0)
        far_scores(nxt, nxt, 0)
        far_softmax(1)
        return carry

    lax.fori_loop(0, n_pairs, far_pair, 0)
    far_values(jnp.where(n_pairs == 0, odd, 2 * n_pairs - 1), 1)
    o_s = _finish(l_ref, acc_ref)

    _init_state(m_ref, l_ref, acc_ref)
    kw = jnp.concatenate([kw0_ref[0], kw1_ref[0], kw2_ref[0]], axis=0)
    vw = jnp.concatenate([vw0_ref[0], vw1_ref[0], vw2_ref[0]], axis=0)
    s_ref[0, :, :WINDOW + NSA_TQ] = _dot_nt(q4_ref[...], kw) + dw_ref[0, 0].astype(F32)
    _fold_tile(0, WINDOW + NSA_TQ, vw, *state, NSA_ROWS)
    o_w = _finish(l_ref, acc_ref)

    gate = gate_ref[0]
    outs = []
    for p in range(P):
        rows = slice(p * TQ, (p + 1) * TQ)
        gc = gate[:, 3 * p:3 * p + 1]
        gs = gate[:, 3 * p + 1:3 * p + 2]
        gw = gate[:, 3 * p + 2:3 * p + 3]
        outs.append(gc * o_c[rows] + gs * o_s[rows] + gw * o_w[rows])
    o_ref[0] = jnp.concatenate(outs, axis=1).astype(o_ref.dtype)


def _nsa(proj3, gates3, cmp_pad, dc, ds, dw, perm, eblk, wfar, n_sel):
    B, S, _ = proj3.shape
    G = NSA_GROUPS
    ncp = cmp_pad.shape[3]
    top_n = min(SEL_TOPK, n_sel)
    assert top_n > 3, "the selection loop assumes the three forced blocks fit in the top-n"
    qw = NSA_HPG * HEAD_DIM
    R = NSA_HPG * NSA_TQ
    wide = max(NSA_TK, ncp, WINDOW + NSA_TQ)
    wblk = lambda col, back: pl.BlockSpec(
        (1, NSA_TQ, HEAD_DIM),
        lambda b, g, t, col=col, back=back: (b, jnp.maximum(t - back, 0), col // HEAD_DIM + g))
    kernel = functools.partial(_nsa_kernel, n_sel=n_sel, top_n=top_n)
    return pl.pallas_call(
        kernel,
        grid=(B, G, S // NSA_TQ),
        in_specs=[
            pl.BlockSpec((1, NSA_TQ, qw), lambda b, g, t: (b, t, g)),
            pl.BlockSpec((1, NSA_TQ, LANES), lambda b, g, t: (b, t, g)),
            pl.BlockSpec((1, 1, 1, ncp, HEAD_DIM), lambda b, g, t: (0, b, g, 0, 0)),
            pl.BlockSpec((1, 1, 1, ncp, HEAD_DIM), lambda b, g, t: (1, b, g, 0, 0)),
            pl.BlockSpec((1, S, HEAD_DIM), lambda b, g, t: (b, 0, COL_KS // HEAD_DIM + g)),
            pl.BlockSpec((1, S, HEAD_DIM), lambda b, g, t: (b, 0, COL_VS // HEAD_DIM + g)),
            wblk(COL_KW, 2), wblk(COL_KW, 1), wblk(COL_KW, 0),
            wblk(COL_VW, 2), wblk(COL_VW, 1), wblk(COL_VW, 0),
            pl.BlockSpec((1, 1) + dc.shape[2:], lambda b, g, t: (jnp.minimum(t, 1), g, 0, 0)),
            pl.BlockSpec((1, 1) + ds.shape[2:], lambda b, g, t: (jnp.minimum(t, 1), g, 0, 0)),
            pl.BlockSpec((1, 1) + dw.shape[2:], lambda b, g, t: (jnp.minimum(t, 2), g, 0, 0)),
            pl.BlockSpec(perm.shape, lambda b, g, t: (0, 0)),
            pl.BlockSpec(eblk.shape, lambda b, g, t: (0, 0)),
            pl.BlockSpec(wfar.shape, lambda b, g, t: (0, 0)),
        ],
        out_specs=pl.BlockSpec((1, NSA_TQ, qw), lambda b, g, t: (b, t, g)),
        out_shape=jax.ShapeDtypeStruct((B, S, NSA_HEADS * HEAD_DIM), BF16),
        scratch_shapes=[
            pltpu.VMEM((2, R, wide), F32),
            pltpu.VMEM((2, R, wide), BF16),
            pltpu.VMEM((2, R, LANES), F32),
            pltpu.VMEM((R, LANES), F32),
            pltpu.VMEM((R, LANES), F32),
            pltpu.VMEM((R, HEAD_DIM), F32),
            pltpu.VMEM((NSA_TQ, ncp), F32),
            pltpu.VMEM((R, CMP_NEAR), F32),
            pltpu.VMEM((R, CMP_NEAR), F32),
            pltpu.VMEM((S // NSA_TK + 1, NSA_TQ, LANES), BF16),
            pltpu.VMEM((NSA_TQ, n_sel), F32),
            pltpu.VMEM((R, HEAD_DIM), BF16),
            pltpu.VMEM((n_sel, NSA_TQ), F32),
            pltpu.VMEM((R, HEAD_DIM), F32),
        ],
        compiler_params=_params("parallel", "parallel", "arbitrary"),
    )(proj3, gates3, cmp_pad, cmp_pad, proj3, proj3,
      proj3, proj3, proj3, proj3, proj3, proj3, dc, ds, dw, perm, eblk, wfar)


def _diff_kernel(q_ref, k_ref, v_ref, dd_ref, lq1_ref, lk1_ref, lq2_ref, lk2_ref, hg_ref,
                 o_ref, s_ref, p_ref, m_ref, l_ref, acc_ref, *, lam_init):
    qt = pl.program_id(2)
    TQ = DIFF_TQ
    q = q_ref[0]
    zero = jnp.zeros((TQ, HEAD_DIM), BF16)
    lhs = jnp.concatenate([
        jnp.concatenate([q[:, :HEAD_DIM], zero], axis=1),
        jnp.concatenate([zero, q[:, HEAD_DIM:]], axis=1)], axis=0)
    lam = (jnp.exp(jnp.sum(lq1_ref[...] * lk1_ref[...], axis=-1, keepdims=True))
           - jnp.exp(jnp.sum(lq2_ref[...] * lk2_ref[...], axis=-1, keepdims=True)) + lam_init)

    def scores(kt, slot, bias):
        k0 = pl.multiple_of(kt * DIFF_TK, DIFF_TK)
        sc = _dot_nt(lhs, k_ref[0, pl.ds(k0, DIFF_TK), :])
        if bias is not None:
            sc = (sc.reshape(2, TQ, DIFF_TK) + bias[None]).reshape(2 * TQ, DIFF_TK)
        s_ref[slot] = sc

    def fold(kt, slot):
        k0 = pl.multiple_of(kt * DIFF_TK, DIFF_TK)
        _fold_tile_inplace(slot, DIFF_TK, v_ref[0, pl.ds(k0, DIFF_TK), :],
                           s_ref, p_ref, m_ref, l_ref, acc_ref, DIFF_ROWS)

    _init_state(m_ref, l_ref, acc_ref)
    n_far = jnp.maximum(qt - 1, 0)
    prev = jnp.maximum(qt - 1, 0)
    odd = jnp.maximum(n_far - 1, 0)
    last_pair_tile = jnp.maximum(n_far - 2, 0)
    scores(qt, 0, dd_ref[0, :, DIFF_TK:].astype(F32))
    fold(qt, 0)
    scores(prev, 0, jnp.where(qt >= 1, dd_ref[0, :, :DIFF_TK].astype(F32), NEG))
    fold(prev, 0)
    scores(0, 1, None)

    @pl.when(n_far % 2 == 1)
    def _():
        scores(odd, 0, None)
        fold(odd, 0)

    def far_pair(i, carry):
        a = 2 * i
        scores(a + 1, 0, None)
        fold(a, 1)
        scores(jnp.minimum(a + 2, last_pair_tile), 1, None)
        fold(a + 1, 0)
        return carry

    lax.fori_loop(0, n_far // 2, far_pair, 0)
    a = _finish(l_ref, acc_ref)
    o = a[:TQ] - lam * a[TQ:]
    y = _norm_rows(o, hg_ref[0]) * (1.0 - lam_init)
    o_ref[0] = y.astype(o_ref.dtype)


def _diff(proj3, dd, lq1, lk1, lq2, lk2, hg, lam_init):
    B, S, _ = proj3.shape
    H = DIFF_HEADS
    w = 2 * HEAD_DIM
    R = 2 * DIFF_TQ
    vec = pl.BlockSpec((1, HEAD_DIM), lambda b, h, t: (0, 0))
    kernel = functools.partial(_diff_kernel, lam_init=lam_init)
    return pl.pallas_call(
        kernel,
        grid=(B, H, S // DIFF_TQ),
        in_specs=[
            pl.BlockSpec((1, DIFF_TQ, w), lambda b, h, t: (b, t, COL_QD // w + h)),
            pl.BlockSpec((1, S, w), lambda b, h, t: (b, 0, COL_KD // w + h)),
            pl.BlockSpec((1, S, w), lambda b, h, t: (b, 0, COL_VD // w + h)),
            pl.BlockSpec((1,) + dd.shape[1:], lambda b, h, t: (h, 0, 0)),
            vec, vec, vec, vec,
            pl.BlockSpec((1, 1, w), lambda b, h, t: (h, 0, 0)),
        ],
        out_specs=pl.BlockSpec((1, DIFF_TQ, w), lambda b, h, t: (b, t, h)),
        out_shape=jax.ShapeDtypeStruct((B, S, H * w), BF16),
        scratch_shapes=[
            pltpu.VMEM((2, R, DIFF_TK), F32),
            pltpu.VMEM((1, R, DIFF_TK), BF16),
            pltpu.VMEM((R, LANES), F32),
            pltpu.VMEM((R, LANES), F32),
            pltpu.VMEM((R, DIFF_V_DIM), F32),
        ],
        compiler_params=_params("parallel", "parallel", "arbitrary"),
    )(proj3, proj3, proj3, dd, lq1, lk1, lq2, lk2, hg)


def _merge_kernel(on_ref, od_ref, wn_ref, wd_ref, ga_ref, gb_ref, o_ref):
    u1 = _dot(on_ref[...], wn_ref[...])
    u2 = _dot(od_ref[...], wd_ref[...])
    ga = jax.nn.sigmoid(ga_ref[...].astype(F32))
    gb = jax.nn.sigmoid(gb_ref[...].astype(F32))
    o_ref[...] = (ga * u1 + gb * u2).astype(o_ref.dtype)


def _merge(o_nsa, o_d, wn, wd, proj, tm=1024, tn=512):
    T, kn = o_nsa.shape
    kd = o_d.shape[1]
    return pl.pallas_call(
        _merge_kernel,
        grid=(T // tm, D_MODEL // tn),
        in_specs=[
            pl.BlockSpec((tm, kn), lambda i, j: (i, 0)),
            pl.BlockSpec((tm, kd), lambda i, j: (i, 0)),
            pl.BlockSpec((kn, tn), lambda i, j: (0, j)),
            pl.BlockSpec((kd, tn), lambda i, j: (0, j)),
            pl.BlockSpec((tm, tn), lambda i, j: (i, COL_GM // tn + j)),
            pl.BlockSpec((tm, tn), lambda i, j: (i, (COL_GM + D_MODEL) // tn + j)),
        ],
        out_specs=pl.BlockSpec((tm, tn), lambda i, j: (i, j)),
        out_shape=jax.ShapeDtypeStruct((T, D_MODEL), BF16),
        compiler_params=_params("parallel", "arbitrary"),
    )(o_nsa, o_d, wn, wd, proj, proj)


def _out_proj_kernel(a_ref, w_ref, x_ref, o_ref):
    o_ref[...] = x_ref[...] + _dot(a_ref[...], w_ref[...])


def _out_proj(a, w, x2, tm=512, tn=2048):
    T, k = a.shape
    return pl.pallas_call(
        _out_proj_kernel,
        grid=(T // tm, D_MODEL // tn),
        in_specs=[
            pl.BlockSpec((tm, k), lambda i, j: (i, 0)),
            pl.BlockSpec((k, tn), lambda i, j: (0, j)),
            pl.BlockSpec((tm, tn), lambda i, j: (i, j)),
        ],
        out_specs=pl.BlockSpec((tm, tn), lambda i, j: (i, j)),
        out_shape=jax.ShapeDtypeStruct((T, D_MODEL), F32),
        compiler_params=_params("parallel", "arbitrary"),
    )(a, w, x2)


def _ffn_in_kernel(x_ref, g_ref, wa_ref, wb_ref, o_ref, h_ref):
    @pl.when(pl.program_id(1) == 0)
    def _():
        h_ref[...] = _norm_rows(x_ref[...], g_ref[...]).astype(BF16)

    h = h_ref[...]
    a = _dot(h, wa_ref[...])
    b = _dot(h, wb_ref[...])
    o_ref[...] = (a * jax.nn.sigmoid(a) * b).astype(o_ref.dtype)


def _ffn_in(x2, g, w, tm=1024, tn=512):
    T = x2.shape[0]
    nb = D_FF // tn
    return pl.pallas_call(
        _ffn_in_kernel,
        grid=(T // tm, nb),
        in_specs=[
            pl.BlockSpec((tm, D_MODEL), lambda i, j: (i, 0)),
            pl.BlockSpec((1, D_MODEL), lambda i, j: (0, 0)),
            pl.BlockSpec((D_MODEL, tn), lambda i, j: (0, j)),
            pl.BlockSpec((D_MODEL, tn), lambda i, j: (0, j + nb)),
        ],
        out_specs=pl.BlockSpec((tm, tn), lambda i, j: (i, j)),
        out_shape=jax.ShapeDtypeStruct((T, D_FF), BF16),
        scratch_shapes=[pltpu.VMEM((tm, D_MODEL), BF16)],
        compiler_params=_params("parallel", "arbitrary"),
    )(x2, g, w, w)


def _ffn_out_kernel(a_ref, w_ref, x_ref, g_ref, o_ref, acc_ref, *, final_norm):
    k = pl.program_id(1)

    @pl.when(k == 0)
    def _():
        acc_ref[...] = x_ref[...]

    acc_ref[...] += _dot(a_ref[...], w_ref[...])

    @pl.when(k == pl.num_programs(1) - 1)
    def _():
        y = acc_ref[...]
        o_ref[...] = _norm_rows(y, g_ref[...]) if final_norm else y


def _ffn_out(a, w, x2, g, final_norm, tm=512, tk=2816):
    T = a.shape[0]
    kernel = functools.partial(_ffn_out_kernel, final_norm=final_norm)
    return pl.pallas_call(
        kernel,
        grid=(T // tm, D_FF // tk),
        in_specs=[
            pl.BlockSpec((tm, tk), lambda i, k: (i, k)),
            pl.BlockSpec((tk, D_MODEL), lambda i, k: (k, 0)),
            pl.BlockSpec((tm, D_MODEL), lambda i, k: (i, 0)),
            pl.BlockSpec((1, D_MODEL), lambda i, k: (0, 0)),
        ],
        out_specs=pl.BlockSpec((tm, D_MODEL), lambda i, k: (i, 0)),
        out_shape=jax.ShapeDtypeStruct((T, D_MODEL), F32),
        scratch_shapes=[pltpu.VMEM((tm, D_MODEL), F32)],
        compiler_params=_params("parallel", "arbitrary"),
    )(a, w, x2, g)


def _t5_bucket(rel):
    n = jnp.maximum(rel, 0)
    nf = jnp.maximum(n, 1).astype(F32)
    large = REL_MAX_EXACT + (jnp.log(nf / REL_MAX_EXACT) / math.log(REL_MAX_DIST / REL_MAX_EXACT)
                             * (REL_BUCKETS - REL_MAX_EXACT)).astype(jnp.int32)
    large = jnp.minimum(large, REL_BUCKETS - 1)
    return jnp.where(n < REL_MAX_EXACT, n, large)


def _bias_of_rel(table, rel, valid):
    shifted = (table - table[REL_BUCKETS - 1:REL_BUCKETS]) * LOG2E
    vals = jnp.moveaxis(shifted[_t5_bucket(rel)], -1, 0)
    return jnp.where(valid[None], vals, NEG)


def _toeplitz_bias(table, n_i, n_j, offset, max_rel=None):
    period = REL_MAX_DIST
    near = jnp.arange(period, dtype=jnp.int32)
    g = _bias_of_rel(table, near, near >= 0)
    w = jnp.roll(g[:, ::-1], (offset + 1) % period, axis=1)
    circ = jnp.tile(w, (1, 2 * period))[:, :period * (2 * period - 1)]
    circ = circ.reshape(-1, period, 2 * period - 1)[:, :, :period]
    band = jnp.tile(circ, (1, n_i // period, n_j // period))
    i = jnp.arange(n_i, dtype=jnp.int32)[:, None]
    j = jnp.arange(n_j, dtype=jnp.int32)[None, :]
    rel = (i - j + offset)[None]
    far = 0.0 if max_rel is None else jnp.where(rel < max_rel, 0.0, NEG)
    return jnp.where(rel < 0, NEG, jnp.where(rel < period, band, far))


def _nsa_tables(table_nsa):
    TQ = NSA_TQ

    def stack(t):
        return t.reshape(NSA_GROUPS, NSA_HPG * TQ, t.shape[-1])

    i = jnp.arange(TQ, dtype=jnp.int32)[:, None]
    jc = jnp.arange(CMP_NEAR, dtype=jnp.int32)[None, :]
    rel_c = i - CMP_STRIDE * (jc - CMP_PAD_FRONT) - (CMP_LEN - 1)
    dc = jnp.stack([
        stack(_bias_of_rel(table_nsa, rel_c, (rel_c >= 0) & (jc >= CMP_PAD_FRONT))),
        stack(_bias_of_rel(table_nsa, rel_c, rel_c >= 0)),
    ])
    ds = jnp.stack([
        stack(_toeplitz_bias(table_nsa, TQ, 2 * TQ, 0)),
        stack(_toeplitz_bias(table_nsa, TQ, 2 * TQ, TQ)),
    ]).astype(BF16)
    win = stack(_toeplitz_bias(table_nsa, TQ, WINDOW + TQ, WINDOW, WINDOW))
    jw = jnp.arange(WINDOW + TQ, dtype=jnp.int32)[None, None, :]
    dw = jnp.stack([jnp.where(jw >= WINDOW - c * TQ, win, NEG) for c in range(3)]).astype(BF16)
    return dc, ds, dw


def _diff_table(table_diff):
    return _toeplitz_bias(table_diff, DIFF_TQ, 2 * DIFF_TK, DIFF_TK).astype(BF16)


def _selection_constants(S):
    n_sel = S // SEL_LEN
    n_cmp = S // CMP_STRIDE
    n_kt = S // NSA_TK
    blocks = NSA_TK // SEL_LEN
    j = jnp.arange(n_sel, dtype=jnp.int32)[:, None]
    col = jnp.arange(n_kt * LANES, dtype=jnp.int32)[None, :]
    perm = ((j == blocks * (col // LANES) + col % LANES) & (col % LANES < blocks)).astype(BF16)
    key = jnp.arange(NSA_TK, dtype=jnp.int32)[:, None]
    u = jnp.arange(LANES, dtype=jnp.int32)[None, :]
    eblk = (u == key // SEL_LEN).astype(BF16)
    c = jnp.arange(n_cmp + CMP_PAD_FRONT + CMP_PAD_BACK, dtype=jnp.int32)[None, :] - CMP_PAD_FRONT
    jb = jnp.arange(n_sel, dtype=jnp.int32)[:, None]
    d = c - (SEL_LEN // CMP_STRIDE) * jb
    wfar = jnp.where((d == 0) | (d == 4), 1.0, jnp.where((d >= 1) & (d <= 3), 2.0, 0.0))
    wfar = jnp.where((c >= 0) & (c < n_cmp), wfar, 0.0).astype(BF16)
    return perm, eblk, wfar


def kernel(x, norm_mix_g, w_in, cmp_pe_k, cmp_pe_v, cmp_w1_k, cmp_w2_k, cmp_w1_v, cmp_w2_v,
           diff_lq1, diff_lk1, diff_lq2, diff_lk2, diff_head_g, w_up_nsa, w_up_diff, w_out,
           norm_ff_g, w_ff_in, w_ff_out, rel_bias_table, norm_final_g):
    B, S, D = x.shape
    T = B * S
    depth = w_in.shape[0]
    n_sel = S // SEL_LEN
    n_cmp = S // CMP_STRIDE
    qscale = HEAD_DIM ** -0.5 * LOG2E

    dc, ds, dw = _nsa_tables(rel_bias_table[:, :NSA_HEADS])
    dd = _diff_table(rel_bias_table[:, NSA_HEADS:])
    perm, eblk, wfar = _selection_constants(S)

    gn0 = COL_QD
    gn1 = gn0 + NSA_HEADS * 3
    cs = jnp.ones((1, N_PROJ), F32)
    cs = cs.at[:, COL_QN:COL_KC].set(qscale).at[:, COL_QD:COL_KD].set(qscale)

    x2 = x.reshape(T, D)
    for l in range(depth):
        w_main = jnp.concatenate([w_in[l][:, :gn0], w_in[l][:, gn1:]], axis=1).astype(BF16)
        wg = w_in[l][:, gn0:gn1].reshape(D, NSA_GROUPS, NSA_HPG * 3)
        wg = jnp.pad(wg, ((0, 0), (0, 0), (0, LANES - NSA_HPG * 3))).reshape(D, NSA_GROUPS * LANES)
        proj, gates = _in_proj(x2, norm_mix_g[l][None], w_main, cs, wg.astype(BF16))
        proj3 = proj.reshape(B, S, N_PROJ)
        gates3 = gates.reshape(B, S, NSA_GROUPS * LANES)

        halves = proj3[:, :, COL_KC:COL_KS].reshape(B, n_cmp, CMP_STRIDE, 2, NSA_GROUPS, HEAD_DIM)
        halves = halves.transpose(3, 0, 4, 1, 2, 5).reshape(
            2, B, NSA_GROUPS, n_cmp, CMP_STRIDE * HEAD_DIM)
        pe = jnp.stack([cmp_pe_k[l], cmp_pe_v[l]]).reshape(2, 1, CMP_LEN * HEAD_DIM)
        w1 = jnp.stack([cmp_w1_k[l], cmp_w1_v[l]]).astype(BF16)
        w2 = jnp.stack([cmp_w2_k[l], cmp_w2_v[l]]).astype(BF16)
        cmp_pad = _compress(halves, pe, w1, w2)

        o_nsa = _nsa(proj3, gates3, cmp_pad, dc, ds, dw, perm, eblk, wfar, n_sel)

        lam_init = 0.8 - 0.6 * math.exp(-0.3 * l)
        o_d = _diff(proj3, dd, diff_lq1[l][None], diff_lk1[l][None], diff_lq2[l][None],
                    diff_lk2[l][None], diff_head_g[l][:, None, :], lam_init)

        mix = _merge(o_nsa.reshape(T, -1), o_d.reshape(T, -1),
                     w_up_nsa[l].astype(BF16), w_up_diff[l].astype(BF16), proj)
        x2 = _out_proj(mix, w_out[l].astype(BF16), x2)

        act = _ffn_in(x2, norm_ff_g[l][None], w_ff_in[l].astype(BF16))
        last = l == depth - 1
        x2 = _ffn_out(act, w_ff_out[l].astype(BF16), x2,
                      norm_final_g[None] if last else norm_ff_g[l][None], final_norm=last)
    return x2.reshape(B, S, D)
```

```python
import functools
import math

import jax
import jax.numpy as jnp
from jax import lax
from jax.experimental import pallas as pl
from jax.experimental.pallas import tpu as pltpu

D_MODEL = 2048
HEAD_DIM = 128
NSA_HEADS = 8
NSA_GROUPS = 2
NSA_HPG = NSA_HEADS // NSA_GROUPS
CMP_LEN = 32
CMP_STRIDE = 16
SEL_LEN = 64
SEL_TOPK = 16
WINDOW = 512
CMP_HIDDEN = 256
DIFF_HEADS = 4
DIFF_V_DIM = 2 * HEAD_DIM
REL_BUCKETS = 32
REL_MAX_EXACT = 16
REL_MAX_DIST = 128
D_FF = -(-8 * D_MODEL // (3 * 256)) * 256
EPS = 1e-6
NEG = -1e30
BIG = 1e30
LOG2E = math.log2(math.e)

F32 = jnp.float32
BF16 = jnp.bfloat16
LANES = 128

VMEM_LIMIT_BYTES = 56 * 1024 * 1024

COL_QN = 0
COL_KC = 1024
COL_KS = 1536
COL_VS = 1792
COL_KW = 2048
COL_VW = 2304
COL_QD = 2560
COL_KD = 3584
COL_VD = 4608
COL_GM = 5632
N_PROJ = 9728

NSA_TQ = 256
NSA_TK = 1024
NSA_ROWS = 32
CMP_NEAR = 32
CMP_WIDTHS = (256, 512, 768, 1024)
CMP_PAD_FRONT = 16
CMP_PAD_BACK = 112
DIFF_TQ = 512
DIFF_TK = 512
DIFF_ROWS = 64


def _dot(a, b):
    return jnp.dot(a, b, preferred_element_type=F32)


def _dot_nt(a, b):
    return lax.dot_general(a, b, (((1,), (1,)), ((), ())), preferred_element_type=F32)


def _params(*sem):
    return pltpu.CompilerParams(dimension_semantics=sem, vmem_limit_bytes=VMEM_LIMIT_BYTES)


def _norm_rows(x, g):
    ms = jnp.mean(x * x, axis=-1, keepdims=True)
    return x * lax.rsqrt(ms + EPS) * g


def _in_proj_kernel(x_ref, g_ref, w_ref, cs_ref, wg_ref, o_ref, og_ref, h_ref):
    @pl.when(pl.program_id(1) == 0)
    def _():
        hb = _norm_rows(x_ref[...], g_ref[...]).astype(BF16)
        h_ref[...] = hb
        og_ref[...] = jax.nn.sigmoid(_dot(hb, wg_ref[...]))

    o_ref[...] = (_dot(h_ref[...], w_ref[...]) * cs_ref[...]).astype(o_ref.dtype)


def _in_proj(x2, g, w, cs, wg, tm=512, tn=2432):
    T = x2.shape[0]
    n = w.shape[1]
    ng = wg.shape[1]
    return pl.pallas_call(
        _in_proj_kernel,
        grid=(T // tm, n // tn),
        in_specs=[
            pl.BlockSpec((tm, D_MODEL), lambda i, j: (i, 0)),
            pl.BlockSpec((1, D_MODEL), lambda i, j: (0, 0)),
            pl.BlockSpec((D_MODEL, tn), lambda i, j: (0, j)),
            pl.BlockSpec((1, tn), lambda i, j: (0, j)),
            pl.BlockSpec((D_MODEL, ng), lambda i, j: (0, 0)),
        ],
        out_specs=[
            pl.BlockSpec((tm, tn), lambda i, j: (i, j)),
            pl.BlockSpec((tm, ng), lambda i, j: (i, 0)),
        ],
        out_shape=[
            jax.ShapeDtypeStruct((T, n), BF16),
            jax.ShapeDtypeStruct((T, ng), F32),
        ],
        scratch_shapes=[pltpu.VMEM((tm, D_MODEL), BF16)],
        compiler_params=_params("parallel", "arbitrary"),
    )(x2, g, w, cs, wg)


def _gelu_tanh(x):
    return 0.5 * x * (1.0 + jnp.tanh(math.sqrt(2.0 / math.pi) * (x + 0.044715 * (x * x * x))))


def _compress_kernel(h_ref, pe_ref, w1_ref, w2_ref, o_ref):
    hv = h_ref[0, 0, 0]
    nc = hv.shape[0]
    half = CMP_STRIDE * HEAD_DIM
    ya = _dot(hv, w1_ref[0, :half, :])
    yb = _dot(hv, w1_ref[0, half:, :])
    yb = pltpu.roll(yb, nc - 1, 0)
    row = lax.broadcasted_iota(jnp.int32, yb.shape, 0)
    yb = jnp.where(row == nc - 1, 0.0, yb)
    pe8 = jnp.broadcast_to(pe_ref[0], (8, 2 * half)).astype(BF16)
    pec = _dot(pe8, w1_ref[0])[0:1]
    hid = _gelu_tanh(ya + yb + pec)
    o_ref[0, 0, 0, :CMP_PAD_FRONT] = jnp.zeros((CMP_PAD_FRONT, HEAD_DIM), o_ref.dtype)
    o_ref[0, 0, 0, CMP_PAD_FRONT:CMP_PAD_FRONT + nc] = (
        _dot(hid.astype(BF16), w2_ref[0]).astype(o_ref.dtype))
    o_ref[0, 0, 0, CMP_PAD_FRONT + nc:] = jnp.zeros((CMP_PAD_BACK, HEAD_DIM), o_ref.dtype)


def _compress(halves, pe, w1, w2):
    _, B, G, nc, hw = halves.shape
    ncp = CMP_PAD_FRONT + nc + CMP_PAD_BACK
    return pl.pallas_call(
        _compress_kernel,
        grid=(2, B, G),
        in_specs=[
            pl.BlockSpec((1, 1, 1, nc, hw), lambda s, b, g: (s, b, g, 0, 0)),
            pl.BlockSpec((1, 1, 2 * hw), lambda s, b, g: (s, 0, 0)),
            pl.BlockSpec((1, 2 * hw, CMP_HIDDEN), lambda s, b, g: (s, 0, 0)),
            pl.BlockSpec((1, CMP_HIDDEN, HEAD_DIM), lambda s, b, g: (s, 0, 0)),
        ],
        out_specs=pl.BlockSpec((1, 1, 1, ncp, HEAD_DIM), lambda s, b, g: (s, b, g, 0, 0)),
        out_shape=jax.ShapeDtypeStruct((2, B, G, ncp, HEAD_DIM), BF16),
        compiler_params=_params("parallel", "parallel", "parallel"),
    )(halves, pe, w1, w2)


def _lanes(x, width):
    return jnp.concatenate([x] * (width // x.shape[1]), axis=1)


def _init_state(m_ref, l_ref, acc_ref):
    m_ref[...] = jnp.full(m_ref.shape, NEG, F32)
    l_ref[...] = jnp.zeros(l_ref.shape, F32)
    acc_ref[...] = jnp.zeros(acc_ref.shape, F32)


def _softmax_tile(slot, width, s_ref, p_ref, a_ref, m_ref, l_ref, chunk):
    n_rows = p_ref.shape[1]
    for r in range(0, n_rows, chunk):
        rows = slice(r, r + chunk)
        m_prev = m_ref[rows, :]
        m_new = jnp.maximum(m_prev, jnp.max(s_ref[slot, rows, :width], axis=-1, keepdims=True))
        a_ref[slot, rows, :] = jnp.exp2(m_prev - m_new)
        m_ref[rows, :] = m_new
    for r in range(0, n_rows, chunk):
        rows = slice(r, r + chunk)
        p = jnp.exp2(s_ref[slot, rows, :width] - _lanes(m_ref[rows, :], width))
        l_ref[rows, :] = (a_ref[slot, rows, :] * l_ref[rows, :]
                          + jnp.sum(p, axis=-1, keepdims=True))
        p_ref[slot, rows, :width] = p.astype(BF16)


def _value_tile(slot, width, v, p_ref, a_ref, acc_ref):
    acc_ref[...] = (acc_ref[...] * _lanes(a_ref[slot], acc_ref.shape[1])
                    + _dot(p_ref[slot, :, :width], v))


def _fold_tile(slot, width, v, s_ref, p_ref, a_ref, m_ref, l_ref, acc_ref, chunk):
    _softmax_tile(slot, width, s_ref, p_ref, a_ref, m_ref, l_ref, chunk)
    _value_tile(slot, width, v, p_ref, a_ref, acc_ref)


def _fold_tile_inplace(slot, width, v, s_ref, p_ref, m_ref, l_ref, acc_ref, chunk):
    n_rows = p_ref.shape[1]
    for r in range(0, n_rows, chunk):
        rows = slice(r, r + chunk)
        s = s_ref[slot, rows, :width]
        m_prev = m_ref[rows, :]
        m_new = jnp.maximum(m_prev, jnp.max(s, axis=-1, keepdims=True))
        alpha = jnp.exp2(m_prev - m_new)
        p = jnp.exp2(s - _lanes(m_new, width))
        l_ref[rows, :] = alpha * l_ref[rows, :] + jnp.sum(p, axis=-1, keepdims=True)
        m_ref[rows, :] = m_new
        acc_ref[rows, :] = acc_ref[rows, :] * _lanes(alpha, acc_ref.shape[1])
        p_ref[0, rows, :width] = p.astype(BF16)
    acc_ref[...] += _dot(p_ref[0, :, :width], v)


def _finish(l_ref, acc_ref):
    return acc_ref[...] / _lanes(l_ref[...], acc_ref.shape[1])


def _split_dot_nt(w, a):
    hi = a.astype(BF16)
    lo = (a - hi.astype(F32)).astype(BF16)
    return _dot_nt(w, hi) + _dot_nt(w, lo)


def _nsa_kernel(q_ref, gate_ref, kc_ref, vc_ref, ks_ref, vs_ref,
                kw0_ref, kw1_ref, kw2_ref, vw0_ref, vw1_ref, vw2_ref,
                dc_ref, ds_ref, dw_ref, perm_ref, eblk_ref, wfar_ref, o_ref,
                s_ref, p_ref, a_ref, m_ref, l_ref, acc_ref, imp_ref, sn_ref, pn_ref, selt_ref,
                sel_ref, q4_ref, pick_ref, oc_ref, mw_ref, lw_ref, accw_ref,
                *, n_sel, top_n):
    qt = pl.program_id(2)
    TQ, P = NSA_TQ, NSA_HPG
    R = P * TQ
    blocks_per_tile = NSA_TQ // SEL_LEN
    stats = (s_ref, p_ref, a_ref, m_ref, l_ref)
    state = stats + (acc_ref,)
    qblk = q_ref[0]
    q4_ref[...] = jnp.concatenate(
        [qblk[:, p * HEAD_DIM:(p + 1) * HEAD_DIM] for p in range(P)], axis=0)

    ncp = kc_ref.shape[3]
    cmp_per_tile = NSA_TQ // CMP_STRIDE
    far_end = cmp_per_tile * qt
    c0 = pl.multiple_of(far_end, cmp_per_tile)

    def compressed(w):
        lane = lax.broadcasted_iota(jnp.int32, (1, w), 1)
        far_bias = jnp.where((lane >= CMP_PAD_FRONT) & (lane < far_end), 0.0, NEG)
        kn = kc_ref[0, 0, 0, pl.ds(c0, CMP_NEAR), :]
        vn = vc_ref[0, 0, 0, pl.ds(c0, CMP_NEAR), :]
        s_ref[0, :, :w] = _dot_nt(q4_ref[...], kc_ref[0, 0, 0, :w, :])
        sn_ref[...] = _dot_nt(q4_ref[...], kn) + dc_ref[0, 0]
        imp_ref[:, :w] = jnp.zeros((TQ, w), F32)

        def cmp_head(p, carry):
            for c in range(TQ // NSA_ROWS):
                rows = pl.ds(pl.multiple_of(p * TQ + c * NSA_ROWS, NSA_ROWS), NSA_ROWS)
                irows = slice(c * NSA_ROWS, (c + 1) * NSA_ROWS)
                sf = s_ref[0, rows, :w] + far_bias
                sn = sn_ref[rows, :]
                m = jnp.maximum(jnp.max(sf, axis=-1, keepdims=True),
                                jnp.max(sn, axis=-1, keepdims=True))
                ef = jnp.exp2(sf - m)
                en = jnp.exp2(sn - m)
                l = jnp.sum(ef, axis=-1, keepdims=True) + jnp.sum(en, axis=-1, keepdims=True)
                inv = jnp.where(m > 0.5 * NEG, 1.0 / l, 0.0)
                pf = ef * inv
                p_ref[0, rows, :w] = pf.astype(BF16)
                pn_ref[rows, :] = en * inv
                imp_ref[irows, :w] += pf
            return carry

        lax.fori_loop(0, P, cmp_head, 0)
        p_near = pn_ref[...]
        oc_ref[...] = (_dot(p_ref[0, :, :w], vc_ref[0, 0, 0, :w, :])
                       + _dot(p_near.astype(BF16), vn))

        imp_near = jnp.sum(p_near.reshape(P, TQ, CMP_NEAR), axis=0)
        jj = lax.broadcasted_iota(jnp.int32, (n_sel, CMP_NEAR), 0)
        jn = lax.broadcasted_iota(jnp.int32, (n_sel, CMP_NEAR), 1)
        d = (far_end - CMP_PAD_FRONT + jn) - (SEL_LEN // CMP_STRIDE) * jj
        w_near = jnp.where((d == 0) | (d == 4), 1.0,
                           jnp.where((d >= 1) & (d <= 3), 2.0, 0.0)).astype(BF16)
        pick_ref[...] = (_split_dot_nt(wfar_ref[:, :w], imp_ref[:, :w])
                         + _split_dot_nt(w_near, imp_near))

    max_far_end = (ks_ref.shape[1] // TQ - 1) * cmp_per_tile
    candidates = sorted(set(CMP_WIDTHS) | {ncp})
    widths = [w for w in candidates if w < max_far_end]
    widths.append(min(w for w in candidates if w >= max_far_end))
    lo = -1
    for w in widths:
        pl.when((far_end > lo) & (far_end <= w))(functools.partial(compressed, w))
        lo = w
    o_c = oc_ref[...]
    p_slc = pick_ref[...]

    bj = lax.broadcasted_iota(jnp.int32, (n_sel, TQ), 0)
    ti = lax.broadcasted_iota(jnp.int32, (n_sel, TQ), 1)
    jt = blocks_per_tile * qt + ti // SEL_LEN
    forced = (bj == 0) | (bj == jt) | (bj == jt - 1)
    pick_ref[...] = jnp.where(forced, -jnp.inf, jnp.where(bj > jt, NEG, p_slc))
    idx = lax.broadcasted_iota(jnp.int32, (n_sel, LANES), 0).astype(F32)

    def pick(_, sc):
        mx = jnp.max(sc, axis=0, keepdims=True)
        first = jnp.min(jnp.where(sc == mx, idx, float(n_sel)), axis=0, keepdims=True)
        return jnp.where(idx == first, -jnp.inf, sc)

    for h in range(0, TQ, LANES):
        pick_ref[:, h:h + LANES] = lax.fori_loop(0, top_n - 3, pick, pick_ref[:, h:h + LANES])
    selneg_t = jnp.where(pick_ref[...] == -jnp.inf, 0.0, NEG)
    sel_ref[...] = selneg_t.T
    selneg = sel_ref[...]
    bj = lax.broadcasted_iota(jnp.int32, (TQ, n_sel), 1)

    nb0 = jnp.maximum(blocks_per_tile * (qt - 1), 0)
    pj = lax.broadcasted_iota(jnp.int32, (n_sel, LANES), 0)
    pu = lax.broadcasted_iota(jnp.int32, (n_sel, LANES), 1)
    perm_near = ((pj == nb0 + pu) & (pu < 2 * blocks_per_tile)).astype(BF16)
    sel_near = _dot(selneg.astype(BF16), perm_near).astype(BF16)
    sel_far = jnp.where(bj < blocks_per_tile * (qt - 1), selneg, NEG).astype(BF16)
    sel_all = _dot(sel_far, perm_ref[...]).astype(BF16)
    n_kt = selt_ref.shape[0] - 1
    for kt in range(n_kt):
        selt_ref[kt] = sel_all[:, kt * LANES:(kt + 1) * LANES]
    col = lax.broadcasted_iota(jnp.int32, (TQ, LANES), 1)
    selt_ref[n_kt] = jnp.where(col < NSA_TK // SEL_LEN, NEG, 0.0).astype(BF16)

    _init_state(m_ref, l_ref, acc_ref)
    near_keys = 2 * NSA_TQ
    ns = pl.multiple_of(nb0 * SEL_LEN, NSA_TQ)
    lhs = jnp.concatenate([q4_ref[...], jnp.concatenate([sel_near] * P, axis=0)], axis=1)
    rhs = jnp.concatenate([ks_ref[0, pl.ds(ns, near_keys), :], eblk_ref[:near_keys, :]], axis=1)
    s_ref[0, :, :near_keys] = _dot_nt(lhs, rhs) + ds_ref[0, 0].astype(F32)

    def far_scores(kt, sel_idx, slot):
        k0 = pl.multiple_of(kt * NSA_TK, NSA_TK)
        lhs_t = jnp.concatenate(
            [q4_ref[...], jnp.concatenate([selt_ref[sel_idx]] * P, axis=0)], axis=1)
        rhs_t = jnp.concatenate([ks_ref[0, pl.ds(k0, NSA_TK), :], eblk_ref[...]], axis=1)
        s_ref[slot, :, :NSA_TK] = _dot_nt(lhs_t, rhs_t)

    def far_softmax(slot):
        _softmax_tile(slot, NSA_TK, *stats, NSA_ROWS)

    def far_values(kt, slot):
        k0 = pl.multiple_of(kt * NSA_TK, NSA_TK)
        _value_tile(slot, NSA_TK, vs_ref[0, pl.ds(k0, NSA_TK), :], p_ref, a_ref, acc_ref)

    n_far = (qt + 2) // 4
    n_pairs = n_far // 2
    odd = jnp.maximum(n_far - 1, 0)
    odd_sel = jnp.where(n_far % 2 == 1, odd, selt_ref.shape[0] - 1)
    last_pair_tile = jnp.maximum(n_far - 2, 0)
    far_scores(odd, odd_sel, 1)
    _fold_tile(0, near_keys, vs_ref[0, pl.ds(ns, near_keys), :], *state, NSA_ROWS)
    far_scores(0, 0, 0)
    far_softmax(1)

    def far_pair(i, carry):
        a = 2 * i
        far_values(jnp.where(i == 0, odd, a - 1), 1)
        far_scores(a + 1, a + 1, 1)
        far_softmax(0)
        nxt = jnp.minimum(a + 2, last_pair_tile)
        far_values(a, 0)
        far_scores(nxt, nxt, 0)
        far_softmax(1)
        return carry

    lax.fori_loop(0, n_pairs, far_pair, 0)
    far_values(jnp.where(n_pairs == 0, odd, 2 * n_pairs - 1), 1)
    o_s = _finish(l_ref, acc_ref)

    _init_state(mw_ref, lw_ref, accw_ref)
    kw = jnp.concatenate([kw0_ref[0], kw1_ref[0], kw2_ref[0]], axis=0)
    vw = jnp.concatenate([vw0_ref[0], vw1_ref[0], vw2_ref[0]], axis=0)
    s_ref[0, :, :WINDOW + NSA_TQ] = _dot_nt(q4_ref[...], kw) + dw_ref[0, 0].astype(F32)
    _fold_tile(0, WINDOW + NSA_TQ, vw, s_ref, p_ref, a_ref, mw_ref, lw_ref, accw_ref, NSA_ROWS)
    o_w = _finish(lw_ref, accw_ref)

    gate = gate_ref[0]
    outs = []
    for p in range(P):
        rows = slice(p * TQ, (p + 1) * TQ)
        gc = gate[:, 3 * p:3 * p + 1]
        gs = gate[:, 3 * p + 1:3 * p + 2]
        gw = gate[:, 3 * p + 2:3 * p + 3]
        outs.append(gc * o_c[rows] + gs * o_s[rows] + gw * o_w[rows])
    o_ref[0] = jnp.concatenate(outs, axis=1).astype(o_ref.dtype)


def _nsa(proj3, gates3, cmp_pad, dc, ds, dw, perm, eblk, wfar, n_sel):
    B, S, _ = proj3.shape
    G = NSA_GROUPS
    ncp = cmp_pad.shape[3]
    top_n = min(SEL_TOPK, n_sel)
    assert top_n > 3, "the selection loop assumes the three forced blocks fit in the top-n"
    qw = NSA_HPG * HEAD_DIM
    R = NSA_HPG * NSA_TQ
    wide = max(NSA_TK, ncp, WINDOW + NSA_TQ)
    wblk = lambda col, back: pl.BlockSpec(
        (1, NSA_TQ, HEAD_DIM),
        lambda b, g, t, col=col, back=back: (b, jnp.maximum(t - back, 0), col // HEAD_DIM + g))
    kernel = functools.partial(_nsa_kernel, n_sel=n_sel, top_n=top_n)
    return pl.pallas_call(
        kernel,
        grid=(B, G, S // NSA_TQ),
        in_specs=[
            pl.BlockSpec((1, NSA_TQ, qw), lambda b, g, t: (b, t, g)),
            pl.BlockSpec((1, NSA_TQ, LANES), lambda b, g, t: (b, t, g)),
            pl.BlockSpec((1, 1, 1, ncp, HEAD_DIM), lambda b, g, t: (0, b, g, 0, 0)),
            pl.BlockSpec((1, 1, 1, ncp, HEAD_DIM), lambda b, g, t: (1, b, g, 0, 0)),
            pl.BlockSpec((1, S, HEAD_DIM), lambda b, g, t: (b, 0, COL_KS // HEAD_DIM + g)),
            pl.BlockSpec((1, S, HEAD_DIM), lambda b, g, t: (b, 0, COL_VS // HEAD_DIM + g)),
            wblk(COL_KW, 2), wblk(COL_KW, 1), wblk(COL_KW, 0),
            wblk(COL_VW, 2), wblk(COL_VW, 1), wblk(COL_VW, 0),
            pl.BlockSpec((1, 1) + dc.shape[2:], lambda b, g, t: (jnp.minimum(t, 1), g, 0, 0)),
            pl.BlockSpec((1, 1) + ds.shape[2:], lambda b, g, t: (jnp.minimum(t, 1), g, 0, 0)),
            pl.BlockSpec((1, 1) + dw.shape[2:], lambda b, g, t: (jnp.minimum(t, 2), g, 0, 0)),
            pl.BlockSpec(perm.shape, lambda b, g, t: (0, 0)),
            pl.BlockSpec(eblk.shape, lambda b, g, t: (0, 0)),
            pl.BlockSpec(wfar.shape, lambda b, g, t: (0, 0)),
        ],
        out_specs=pl.BlockSpec((1, NSA_TQ, qw), lambda b, g, t: (b, t, g)),
        out_shape=jax.ShapeDtypeStruct((B, S, NSA_HEADS * HEAD_DIM), BF16),
        scratch_shapes=[
            pltpu.VMEM((2, R, wide), F32),
            pltpu.VMEM((2, R, wide), BF16),
            pltpu.VMEM((2, R, LANES), F32),
            pltpu.VMEM((R, LANES), F32),
            pltpu.VMEM((R, LANES), F32),
            pltpu.VMEM((R, HEAD_DIM), F32),
            pltpu.VMEM((NSA_TQ, ncp), F32),
            pltpu.VMEM((R, CMP_NEAR), F32),
            pltpu.VMEM((R, CMP_NEAR), F32),
            pltpu.VMEM((S // NSA_TK + 1, NSA_TQ, LANES), BF16),
            pltpu.VMEM((NSA_TQ, n_sel), F32),
            pltpu.VMEM((R, HEAD_DIM), BF16),
            pltpu.VMEM((n_sel, NSA_TQ), F32),
            pltpu.VMEM((R, HEAD_DIM), F32),
            pltpu.VMEM((R, LANES), F32),
            pltpu.VMEM((R, LANES), F32),
            pltpu.VMEM((R, HEAD_DIM), F32),
        ],
        compiler_params=_params("parallel", "parallel", "arbitrary"),
    )(proj3, gates3, cmp_pad, cmp_pad, proj3, proj3,
      proj3, proj3, proj3, proj3, proj3, proj3, dc, ds, dw, perm, eblk, wfar)


def _diff_kernel(q_ref, k_ref, v_ref, dd_ref, lq1_ref, lk1_ref, lq2_ref, lk2_ref, hg_ref,
                 o_ref, s_ref, p_ref, m_ref, l_ref, acc_ref, *, lam_init):
    qt = pl.program_id(2)
    TQ = DIFF_TQ
    q = q_ref[0]
    zero = jnp.zeros((TQ, HEAD_DIM), BF16)
    lhs = jnp.concatenate([
        jnp.concatenate([q[:, :HEAD_DIM], zero], axis=1),
        jnp.concatenate([zero, q[:, HEAD_DIM:]], axis=1)], axis=0)
    lam = (jnp.exp(jnp.sum(lq1_ref[...] * lk1_ref[...], axis=-1, keepdims=True))
           - jnp.exp(jnp.sum(lq2_ref[...] * lk2_ref[...], axis=-1, keepdims=True)) + lam_init)

    def scores(kt, slot, bias):
        k0 = pl.multiple_of(kt * DIFF_TK, DIFF_TK)
        sc = _dot_nt(lhs, k_ref[0, pl.ds(k0, DIFF_TK), :])
        if bias is not None:
            sc = (sc.reshape(2, TQ, DIFF_TK) + bias[None]).reshape(2 * TQ, DIFF_TK)
        s_ref[slot] = sc

    def fold(kt, slot):
        k0 = pl.multiple_of(kt * DIFF_TK, DIFF_TK)
        _fold_tile_inplace(slot, DIFF_TK, v_ref[0, pl.ds(k0, DIFF_TK), :],
                           s_ref, p_ref, m_ref, l_ref, acc_ref, DIFF_ROWS)

    _init_state(m_ref, l_ref, acc_ref)
    n_far = jnp.maximum(qt - 1, 0)
    prev = jnp.maximum(qt - 1, 0)
    odd = jnp.maximum(n_far - 1, 0)
    last_pair_tile = jnp.maximum(n_far - 2, 0)
    scores(qt, 0, dd_ref[0, :, DIFF_TK:].astype(F32))
    fold(qt, 0)
    scores(prev, 0, jnp.where(qt >= 1, dd_ref[0, :, :DIFF_TK].astype(F32), NEG))
    fold(prev, 0)
    scores(0, 1, None)

    @pl.when(n_far % 2 == 1)
    def _():
        scores(odd, 0, None)
        fold(odd, 0)

    def far_pair(i, carry):
        a = 2 * i
        scores(a + 1, 0, None)
        fold(a, 1)
        scores(jnp.minimum(a + 2, last_pair_tile), 1, None)
        fold(a + 1, 0)
        return carry

    lax.fori_loop(0, n_far // 2, far_pair, 0)
    a = _finish(l_ref, acc_ref)
    o = a[:TQ] - lam * a[TQ:]
    y = _norm_rows(o, hg_ref[0]) * (1.0 - lam_init)
    o_ref[0] = y.astype(o_ref.dtype)


def _diff(proj3, dd, lq1, lk1, lq2, lk2, hg, lam_init):
    B, S, _ = proj3.shape
    H = DIFF_HEADS
    w = 2 * HEAD_DIM
    R = 2 * DIFF_TQ
    vec = pl.BlockSpec((1, HEAD_DIM), lambda b, h, t: (0, 0))
    kernel = functools.partial(_diff_kernel, lam_init=lam_init)
    return pl.pallas_call(
        kernel,
        grid=(B, H, S // DIFF_TQ),
        in_specs=[
            pl.BlockSpec((1, DIFF_TQ, w), lambda b, h, t: (b, t, COL_QD // w + h)),
            pl.BlockSpec((1, S, w), lambda b, h, t: (b, 0, COL_KD // w + h)),
            pl.BlockSpec((1, S, w), lambda b, h, t: (b, 0, COL_VD // w + h)),
            pl.BlockSpec((1,) + dd.shape[1:], lambda b, h, t: (h, 0, 0)),
            vec, vec, vec, vec,
            pl.BlockSpec((1, 1, w), lambda b, h, t: (h, 0, 0)),
        ],
        out_specs=pl.BlockSpec((1, DIFF_TQ, w), lambda b, h, t: (b, t, h)),
        out_shape=jax.ShapeDtypeStruct((B, S, H * w), BF16),
        scratch_shapes=[
            pltpu.VMEM((2, R, DIFF_TK), F32),
            pltpu.VMEM((1, R, DIFF_TK), BF16),
            pltpu.VMEM((R, LANES), F32),
            pltpu.VMEM((R, LANES), F32),
            pltpu.VMEM((R, DIFF_V_DIM), F32),
        ],
        compiler_params=_params("parallel", "parallel", "arbitrary"),
    )(proj3, proj3, proj3, dd, lq1, lk1, lq2, lk2, hg)


def _merge_kernel(on_ref, od_ref, wn_ref, wd_ref, ga_ref, gb_ref, o_ref):
    u1 = _dot(on_ref[...], wn_ref[...])
    u2 = _dot(od_ref[...], wd_ref[...])
    ga = jax.nn.sigmoid(ga_ref[...].astype(F32))
    gb = jax.nn.sigmoid(gb_ref[...].astype(F32))
    o_ref[...] = (ga * u1 + gb * u2).astype(o_ref.dtype)


def _merge(o_nsa, o_d, wn, wd, proj, tm=1024, tn=512):
    T, kn = o_nsa.shape
    kd = o_d.shape[1]
    return pl.pallas_call(
        _merge_kernel,
        grid=(T // tm, D_MODEL // tn),
        in_specs=[
            pl.BlockSpec((tm, kn), lambda i, j: (i, 0)),
            pl.BlockSpec((tm, kd), lambda i, j: (i, 0)),
            pl.BlockSpec((kn, tn), lambda i, j: (0, j)),
            pl.BlockSpec((kd, tn), lambda i, j: (0, j)),
            pl.BlockSpec((tm, tn), lambda i, j: (i, COL_GM // tn + j)),
            pl.BlockSpec((tm, tn), lambda i, j: (i, (COL_GM + D_MODEL) // tn + j)),
        ],
        out_specs=pl.BlockSpec((tm, tn), lambda i, j: (i, j)),
        out_shape=jax.ShapeDtypeStruct((T, D_MODEL), BF16),
        compiler_params=_params("parallel", "arbitrary"),
    )(o_nsa, o_d, wn, wd, proj, proj)


def _out_proj_kernel(a_ref, w_ref, x_ref, o_ref):
    o_ref[...] = x_ref[...] + _dot(a_ref[...], w_ref[...])


def _out_proj(a, w, x2, tm=512, tn=2048):
    T, k = a.shape
    return pl.pallas_call(
        _out_proj_kernel,
        grid=(T // tm, D_MODEL // tn),
        in_specs=[
            pl.BlockSpec((tm, k), lambda i, j: (i, 0)),
            pl.BlockSpec((k, tn), lambda i, j: (0, j)),
            pl.BlockSpec((tm, tn), lambda i, j: (i, j)),
        ],
        out_specs=pl.BlockSpec((tm, tn), lambda i, j: (i, j)),
        out_shape=jax.ShapeDtypeStruct((T, D_MODEL), F32),
        compiler_params=_params("parallel", "arbitrary"),
    )(a, w, x2)


def _ffn_in_kernel(x_ref, g_ref, wa_ref, wb_ref, o_ref, h_ref):
    @pl.when(pl.program_id(1) == 0)
    def _():
        h_ref[...] = _norm_rows(x_ref[...], g_ref[...]).astype(BF16)

    h = h_ref[...]
    a = _dot(h, wa_ref[...])
    b = _dot(h, wb_ref[...])
    o_ref[...] = (a * jax.nn.sigmoid(a) * b).astype(o_ref.dtype)


def _ffn_in(x2, g, w, tm=1024, tn=512):
    T = x2.shape[0]
    nb = D_FF // tn
    return pl.pallas_call(
        _ffn_in_kernel,
        grid=(T // tm, nb),
        in_specs=[
            pl.BlockSpec((tm, D_MODEL), lambda i, j: (i, 0)),
            pl.BlockSpec((1, D_MODEL), lambda i, j: (0, 0)),
            pl.BlockSpec((D_MODEL, tn), lambda i, j: (0, j)),
            pl.BlockSpec((D_MODEL, tn), lambda i, j: (0, j + nb)),
        ],
        out_specs=pl.BlockSpec((tm, tn), lambda i, j: (i, j)),
        out_shape=jax.ShapeDtypeStruct((T, D_FF), BF16),
        scratch_shapes=[pltpu.VMEM((tm, D_MODEL), BF16)],
        compiler_params=_params("parallel", "arbitrary"),
    )(x2, g, w, w)


def _ffn_out_kernel(a_ref, w_ref, x_ref, g_ref, o_ref, acc_ref, *, final_norm):
    k = pl.program_id(1)

    @pl.when(k == 0)
    def _():
        acc_ref[...] = x_ref[...]

    acc_ref[...] += _dot(a_ref[...], w_ref[...])

    @pl.when(k == pl.num_programs(1) - 1)
    def _():
        y = acc_ref[...]
        o_ref[...] = _norm_rows(y, g_ref[...]) if final_norm else y


def _ffn_out(a, w, x2, g, final_norm, tm=512, tk=2816):
    T = a.shape[0]
    kernel = functools.partial(_ffn_out_kernel, final_norm=final_norm)
    return pl.pallas_call(
        kernel,
        grid=(T // tm, D_FF // tk),
        in_specs=[
            pl.BlockSpec((tm, tk), lambda i, k: (i, k)),
            pl.BlockSpec((tk, D_MODEL), lambda i, k: (k, 0)),
            pl.BlockSpec((tm, D_MODEL), lambda i, k: (i, 0)),
            pl.BlockSpec((1, D_MODEL), lambda i, k: (0, 0)),
        ],
        out_specs=pl.BlockSpec((tm, D_MODEL), lambda i, k: (i, 0)),
        out_shape=jax.ShapeDtypeStruct((T, D_MODEL), F32),
        scratch_shapes=[pltpu.VMEM((tm, D_MODEL), F32)],
        compiler_params=_params("parallel", "arbitrary"),
    )(a, w, x2, g)


def _t5_bucket(rel):
    n = jnp.maximum(rel, 0)
    nf = jnp.maximum(n, 1).astype(F32)
    large = REL_MAX_EXACT + (jnp.log(nf / REL_MAX_EXACT) / math.log(REL_MAX_DIST / REL_MAX_EXACT)
                             * (REL_BUCKETS - REL_MAX_EXACT)).astype(jnp.int32)
    large = jnp.minimum(large, REL_BUCKETS - 1)
    return jnp.where(n < REL_MAX_EXACT, n, large)


def _bias_of_rel(table, rel, valid):
    shifted = (table - table[REL_BUCKETS - 1:REL_BUCKETS]) * LOG2E
    vals = jnp.moveaxis(shifted[_t5_bucket(rel)], -1, 0)
    return jnp.where(valid[None], vals, NEG)


def _toeplitz_bias(table, n_i, n_j, offset, max_rel=None):
    period = REL_MAX_DIST
    near = jnp.arange(period, dtype=jnp.int32)
    g = _bias_of_rel(table, near, near >= 0)
    w = jnp.roll(g[:, ::-1], (offset + 1) % period, axis=1)
    circ = jnp.tile(w, (1, 2 * period))[:, :period * (2 * period - 1)]
    circ = circ.reshape(-1, period, 2 * period - 1)[:, :, :period]
    band = jnp.tile(circ, (1, n_i // period, n_j // period))
    i = jnp.arange(n_i, dtype=jnp.int32)[:, None]
    j = jnp.arange(n_j, dtype=jnp.int32)[None, :]
    rel = (i - j + offset)[None]
    far = 0.0 if max_rel is None else jnp.where(rel < max_rel, 0.0, NEG)
    return jnp.where(rel < 0, NEG, jnp.where(rel < period, band, far))


def _nsa_tables(table_nsa):
    TQ = NSA_TQ

    def stack(t):
        return t.reshape(NSA_GROUPS, NSA_HPG * TQ, t.shape[-1])

    i = jnp.arange(TQ, dtype=jnp.int32)[:, None]
    jc = jnp.arange(CMP_NEAR, dtype=jnp.int32)[None, :]
    rel_c = i - CMP_STRIDE * (jc - CMP_PAD_FRONT) - (CMP_LEN - 1)
    dc = jnp.stack([
        stack(_bias_of_rel(table_nsa, rel_c, (rel_c >= 0) & (jc >= CMP_PAD_FRONT))),
        stack(_bias_of_rel(table_nsa, rel_c, rel_c >= 0)),
    ])
    ds = jnp.stack([
        stack(_toeplitz_bias(table_nsa, TQ, 2 * TQ, 0)),
        stack(_toeplitz_bias(table_nsa, TQ, 2 * TQ, TQ)),
    ]).astype(BF16)
    win = stack(_toeplitz_bias(table_nsa, TQ, WINDOW + TQ, WINDOW, WINDOW))
    jw = jnp.arange(WINDOW + TQ, dtype=jnp.int32)[None, None, :]
    dw = jnp.stack([jnp.where(jw >= WINDOW - c * TQ, win, NEG) for c in range(3)]).astype(BF16)
    return dc, ds, dw


def _diff_table(table_diff):
    return _toeplitz_bias(table_diff, DIFF_TQ, 2 * DIFF_TK, DIFF_TK).astype(BF16)


def _selection_constants(S):
    n_sel = S // SEL_LEN
    n_cmp = S // CMP_STRIDE
    n_kt = S // NSA_TK
    blocks = NSA_TK // SEL_LEN
    j = jnp.arange(n_sel, dtype=jnp.int32)[:, None]
    col = jnp.arange(n_kt * LANES, dtype=jnp.int32)[None, :]
    perm = ((j == blocks * (col // LANES) + col % LANES) & (col % LANES < blocks)).astype(BF16)
    key = jnp.arange(NSA_TK, dtype=jnp.int32)[:, None]
    u = jnp.arange(LANES, dtype=jnp.int32)[None, :]
    eblk = (u == key // SEL_LEN).astype(BF16)
    c = jnp.arange(n_cmp + CMP_PAD_FRONT + CMP_PAD_BACK, dtype=jnp.int32)[None, :] - CMP_PAD_FRONT
    jb = jnp.arange(n_sel, dtype=jnp.int32)[:, None]
    d = c - (SEL_LEN // CMP_STRIDE) * jb
    wfar = jnp.where((d == 0) | (d == 4), 1.0, jnp.where((d >= 1) & (d <= 3), 2.0, 0.0))
    wfar = jnp.where((c >= 0) & (c < n_cmp), wfar, 0.0).astype(BF16)
    return perm, eblk, wfar


def kernel(x, norm_mix_g, w_in, cmp_pe_k, cmp_pe_v, cmp_w1_k, cmp_w2_k, cmp_w1_v, cmp_w2_v,
           diff_lq1, diff_lk1, diff_lq2, diff_lk2, diff_head_g, w_up_nsa, w_up_diff, w_out,
           norm_ff_g, w_ff_in, w_ff_out, rel_bias_table, norm_final_g):
    B, S, D = x.shape
    T = B * S
    depth = w_in.shape[0]
    n_sel = S // SEL_LEN
    n_cmp = S // CMP_STRIDE
    qscale = HEAD_DIM ** -0.5 * LOG2E

    dc, ds, dw = _nsa_tables(rel_bias_table[:, :NSA_HEADS])
    dd = _diff_table(rel_bias_table[:, NSA_HEADS:])
    perm, eblk, wfar = _selection_constants(S)

    gn0 = COL_QD
    gn1 = gn0 + NSA_HEADS * 3
    cs = jnp.ones((1, N_PROJ), F32)
    cs = cs.at[:, COL_QN:COL_KC].set(qscale).at[:, COL_QD:COL_KD].set(qscale)

    x2 = x.reshape(T, D)
    for l in range(depth):
        w_main = jnp.concatenate([w_in[l][:, :gn0], w_in[l][:, gn1:]], axis=1).astype(BF16)
        wg = w_in[l][:, gn0:gn1].reshape(D, NSA_GROUPS, NSA_HPG * 3)
        wg = jnp.pad(wg, ((0, 0), (0, 0), (0, LANES - NSA_HPG * 3))).reshape(D, NSA_GROUPS * LANES)
        proj, gates = _in_proj(x2, norm_mix_g[l][None], w_main, cs, wg.astype(BF16))
        proj3 = proj.reshape(B, S, N_PROJ)
        gates3 = gates.reshape(B, S, NSA_GROUPS * LANES)

        halves = proj3[:, :, COL_KC:COL_KS].reshape(B, n_cmp, CMP_STRIDE, 2, NSA_GROUPS, HEAD_DIM)
        halves = halves.transpose(3, 0, 4, 1, 2, 5).reshape(
            2, B, NSA_GROUPS, n_cmp, CMP_STRIDE * HEAD_DIM)
        pe = jnp.stack([cmp_pe_k[l], cmp_pe_v[l]]).reshape(2, 1, CMP_LEN * HEAD_DIM)
        w1 = jnp.stack([cmp_w1_k[l], cmp_w1_v[l]]).astype(BF16)
        w2 = jnp.stack([cmp_w2_k[l], cmp_w2_v[l]]).astype(BF16)
        cmp_pad = _compress(halves, pe, w1, w2)

        o_nsa = _nsa(proj3, gates3, cmp_pad, dc, ds, dw, perm, eblk, wfar, n_sel)

        lam_init = 0.8 - 0.6 * math.exp(-0.3 * l)
        o_d = _diff(proj3, dd, diff_lq1[l][None], diff_lk1[l][None], diff_lq2[l][None],
                    diff_lk2[l][None], diff_head_g[l][:, None, :], lam_init)

        mix = _merge(o_nsa.reshape(T, -1), o_d.reshape(T, -1),
                     w_up_nsa[l].astype(BF16), w_up_diff[l].astype(BF16), proj)
        x2 = _out_proj(mix, w_out[l].astype(BF16), x2)

        act = _ffn_in(x2, norm_ff_g[l][None], w_ff_in[l].astype(BF16))
        last = l == depth - 1
        x2 = _ffn_out(act, w_ff_out[l].astype(BF16), x2,
                      norm_final_g[None] if last else norm_ff_g[l][None], final_norm=last)
    return x2.reshape(B, S, D)
```

```python
import functools
import math

import jax
import jax.numpy as jnp
from jax import lax
from jax.experimental import pallas as pl
from jax.experimental.pallas import tpu as pltpu

D_MODEL = 2048
HEAD_DIM = 128
NSA_HEADS = 8
NSA_GROUPS = 2
NSA_HPG = NSA_HEADS // NSA_GROUPS
CMP_LEN = 32
CMP_STRIDE = 16
SEL_LEN = 64
SEL_TOPK = 16
WINDOW = 512
CMP_HIDDEN = 256
DIFF_HEADS = 4
DIFF_V_DIM = 2 * HEAD_DIM
REL_BUCKETS = 32
REL_MAX_EXACT = 16
REL_MAX_DIST = 128
D_FF = -(-8 * D_MODEL // (3 * 256)) * 256
EPS = 1e-6
NEG = -1e30
BIG = 1e30
LOG2E = math.log2(math.e)

F32 = jnp.float32
BF16 = jnp.bfloat16
LANES = 128

VMEM_LIMIT_BYTES = 56 * 1024 * 1024

COL_QN = 0
COL_KC = 1024
COL_KS = 1536
COL_VS = 1792
COL_KW = 2048
COL_VW = 2304
COL_QD = 2560
COL_KD = 3584
COL_VD = 4608
COL_GM = 5632
N_PROJ = 9728

NSA_TQ = 256
NSA_TK = 1024
NSA_ROWS = 32
CMP_NEAR = 32
CMP_WIDTHS = (256, 512, 768, 1024)
CMP_PAD_FRONT = 16
CMP_PAD_BACK = 112
DIFF_TQ = 512
DIFF_TK = 512
DIFF_ROWS = 64


def _dot(a, b):
    return jnp.dot(a, b, preferred_element_type=F32)


def _dot_nt(a, b):
    return lax.dot_general(a, b, (((1,), (1,)), ((), ())), preferred_element_type=F32)


def _params(*sem):
    return pltpu.CompilerParams(dimension_semantics=sem, vmem_limit_bytes=VMEM_LIMIT_BYTES)


def _norm_rows(x, g):
    ms = jnp.mean(x * x, axis=-1, keepdims=True)
    return x * lax.rsqrt(ms + EPS) * g


def _in_proj_kernel(x_ref, g_ref, w_ref, cs_ref, wg_ref, o_ref, og_ref, h_ref):
    @pl.when(pl.program_id(1) == 0)
    def _():
        hb = _norm_rows(x_ref[...], g_ref[...]).astype(BF16)
        h_ref[...] = hb
        og_ref[...] = jax.nn.sigmoid(_dot(hb, wg_ref[...]))

    o_ref[...] = (_dot(h_ref[...], w_ref[...]) * cs_ref[...]).astype(o_ref.dtype)


def _in_proj(x2, g, w, cs, wg, tm=512, tn=2432):
    T = x2.shape[0]
    n = w.shape[1]
    ng = wg.shape[1]
    return pl.pallas_call(
        _in_proj_kernel,
        grid=(T // tm, n // tn),
        in_specs=[
            pl.BlockSpec((tm, D_MODEL), lambda i, j: (i, 0)),
            pl.BlockSpec((1, D_MODEL), lambda i, j: (0, 0)),
            pl.BlockSpec((D_MODEL, tn), lambda i, j: (0, j)),
            pl.BlockSpec((1, tn), lambda i, j: (0, j)),
            pl.BlockSpec((D_MODEL, ng), lambda i, j: (0, 0)),
        ],
        out_specs=[
            pl.BlockSpec((tm, tn), lambda i, j: (i, j)),
            pl.BlockSpec((tm, ng), lambda i, j: (i, 0)),
        ],
        out_shape=[
            jax.ShapeDtypeStruct((T, n), BF16),
            jax.ShapeDtypeStruct((T, ng), F32),
        ],
        scratch_shapes=[pltpu.VMEM((tm, D_MODEL), BF16)],
        compiler_params=_params("parallel", "arbitrary"),
    )(x2, g, w, cs, wg)


def _gelu_tanh(x):
    return 0.5 * x * (1.0 + jnp.tanh(math.sqrt(2.0 / math.pi) * (x + 0.044715 * (x * x * x))))


def _compress_kernel(h_ref, pe_ref, w1_ref, w2_ref, o_ref):
    hv = h_ref[0, 0, 0]
    nc = hv.shape[0]
    half = CMP_STRIDE * HEAD_DIM
    ya = _dot(hv, w1_ref[0, :half, :])
    yb = _dot(hv, w1_ref[0, half:, :])
    yb = pltpu.roll(yb, nc - 1, 0)
    row = lax.broadcasted_iota(jnp.int32, yb.shape, 0)
    yb = jnp.where(row == nc - 1, 0.0, yb)
    pe8 = jnp.broadcast_to(pe_ref[0], (8, 2 * half)).astype(BF16)
    pec = _dot(pe8, w1_ref[0])[0:1]
    hid = _gelu_tanh(ya + yb + pec)
    o_ref[0, 0, 0, :CMP_PAD_FRONT] = jnp.zeros((CMP_PAD_FRONT, HEAD_DIM), o_ref.dtype)
    o_ref[0, 0, 0, CMP_PAD_FRONT:CMP_PAD_FRONT + nc] = (
        _dot(hid.astype(BF16), w2_ref[0]).astype(o_ref.dtype))
    o_ref[0, 0, 0, CMP_PAD_FRONT + nc:] = jnp.zeros((CMP_PAD_BACK, HEAD_DIM), o_ref.dtype)


def _compress(halves, pe, w1, w2):
    _, B, G, nc, hw = halves.shape
    ncp = CMP_PAD_FRONT + nc + CMP_PAD_BACK
    return pl.pallas_call(
        _compress_kernel,
        grid=(2, B, G),
        in_specs=[
            pl.BlockSpec((1, 1, 1, nc, hw), lambda s, b, g: (s, b, g, 0, 0)),
            pl.BlockSpec((1, 1, 2 * hw), lambda s, b, g: (s, 0, 0)),
            pl.BlockSpec((1, 2 * hw, CMP_HIDDEN), lambda s, b, g: (s, 0, 0)),
            pl.BlockSpec((1, CMP_HIDDEN, HEAD_DIM), lambda s, b, g: (s, 0, 0)),
        ],
        out_specs=pl.BlockSpec((1, 1, 1, ncp, HEAD_DIM), lambda s, b, g: (s, b, g, 0, 0)),
        out_shape=jax.ShapeDtypeStruct((2, B, G, ncp, HEAD_DIM), BF16),
        compiler_params=_params("parallel", "parallel", "parallel"),
    )(halves, pe, w1, w2)


def _lanes(x, width):
    return jnp.concatenate([x] * (width // x.shape[1]), axis=1)


def _init_state(m_ref, l_ref, acc_ref):
    m_ref[...] = jnp.full(m_ref.shape, NEG, F32)
    l_ref[...] = jnp.zeros(l_ref.shape, F32)
    acc_ref[...] = jnp.zeros(acc_ref.shape, F32)


def _softmax_tile(slot, width, s_ref, p_ref, a_ref, m_ref, l_ref, chunk):
    n_rows = p_ref.shape[1]
    for r in range(0, n_rows, chunk):
        rows = slice(r, r + chunk)
        m_prev = m_ref[rows, :]
        m_new = jnp.maximum(m_prev, jnp.max(s_ref[slot, rows, :width], axis=-1, keepdims=True))
        a_ref[slot, rows, :] = jnp.exp2(m_prev - m_new)
        m_ref[rows, :] = m_new
    for r in range(0, n_rows, chunk):
        rows = slice(r, r + chunk)
        p = jnp.exp2(s_ref[slot, rows, :width] - _lanes(m_ref[rows, :], width))
        l_ref[rows, :] = (a_ref[slot, rows, :] * l_ref[rows, :]
                          + jnp.sum(p, axis=-1, keepdims=True))
        p_ref[slot, rows, :width] = p.astype(BF16)


def _value_tile(slot, width, v, p_ref, a_ref, acc_ref):
    acc_ref[...] = (acc_ref[...] * _lanes(a_ref[slot], acc_ref.shape[1])
                    + _dot(p_ref[slot, :, :width], v))


def _fold_tile(slot, width, v, s_ref, p_ref, a_ref, m_ref, l_ref, acc_ref, chunk):
    _softmax_tile(slot, width, s_ref, p_ref, a_ref, m_ref, l_ref, chunk)
    _value_tile(slot, width, v, p_ref, a_ref, acc_ref)


def _fold_tile_inplace(slot, width, v, s_ref, p_ref, m_ref, l_ref, acc_ref, chunk):
    n_rows = p_ref.shape[1]
    for r in range(0, n_rows, chunk):
        rows = slice(r, r + chunk)
        s = s_ref[slot, rows, :width]
        m_prev = m_ref[rows, :]
        m_new = jnp.maximum(m_prev, jnp.max(s, axis=-1, keepdims=True))
        alpha = jnp.exp2(m_prev - m_new)
        p = jnp.exp2(s - _lanes(m_new, width))
        l_ref[rows, :] = alpha * l_ref[rows, :] + jnp.sum(p, axis=-1, keepdims=True)
        m_ref[rows, :] = m_new
        acc_ref[rows, :] = acc_ref[rows, :] * _lanes(alpha, acc_ref.shape[1])
        p_ref[0, rows, :width] = p.astype(BF16)
    acc_ref[...] += _dot(p_ref[0, :, :width], v)


def _finish(l_ref, acc_ref):
    return acc_ref[...] / _lanes(l_ref[...], acc_ref.shape[1])


def _split_dot_nt(w, a):
    hi = a.astype(BF16)
    lo = (a - hi.astype(F32)).astype(BF16)
    return _dot_nt(w, hi) + _dot_nt(w, lo)


def _nsa_kernel(q_ref, gate_ref, kc_ref, vc_ref, ks_ref, vs_ref,
                kw0_ref, kw1_ref, kw2_ref, vw0_ref, vw1_ref, vw2_ref,
                dc_ref, ds_ref, dw_ref, perm_ref, eblk_ref, wfar_ref, o_ref,
                s_ref, p_ref, a_ref, m_ref, l_ref, acc_ref, imp_ref, sn_ref, pn_ref, selt_ref,
                sel_ref, q4_ref, pick_ref, oc_ref,
                *, n_sel, top_n):
    qt = pl.program_id(2)
    TQ, P = NSA_TQ, NSA_HPG
    R = P * TQ
    blocks_per_tile = NSA_TQ // SEL_LEN
    stats = (s_ref, p_ref, a_ref, m_ref, l_ref)
    state = stats + (acc_ref,)
    qblk = q_ref[0]
    q4_ref[...] = jnp.concatenate(
        [qblk[:, p * HEAD_DIM:(p + 1) * HEAD_DIM] for p in range(P)], axis=0)

    ncp = kc_ref.shape[3]
    cmp_per_tile = NSA_TQ // CMP_STRIDE
    far_end = cmp_per_tile * qt
    c0 = pl.multiple_of(far_end, cmp_per_tile)

    def compressed(w):
        lane = lax.broadcasted_iota(jnp.int32, (1, w), 1)
        far_bias = jnp.where((lane >= CMP_PAD_FRONT) & (lane < far_end), 0.0, NEG)
        kn = kc_ref[0, 0, 0, pl.ds(c0, CMP_NEAR), :]
        vn = vc_ref[0, 0, 0, pl.ds(c0, CMP_NEAR), :]
        s_ref[0, :, :w] = _dot_nt(q4_ref[...], kc_ref[0, 0, 0, :w, :])
        sn_ref[...] = _dot_nt(q4_ref[...], kn) + dc_ref[0, 0]
        imp_ref[:, :w] = jnp.zeros((TQ, w), F32)

        def cmp_head(p, carry):
            for c in range(TQ // NSA_ROWS):
                rows = pl.ds(pl.multiple_of(p * TQ + c * NSA_ROWS, NSA_ROWS), NSA_ROWS)
                irows = slice(c * NSA_ROWS, (c + 1) * NSA_ROWS)
                sf = s_ref[0, rows, :w] + far_bias
                sn = sn_ref[rows, :]
                m = jnp.maximum(jnp.max(sf, axis=-1, keepdims=True),
                                jnp.max(sn, axis=-1, keepdims=True))
                ef = jnp.exp2(sf - m)
                en = jnp.exp2(sn - m)
                l = jnp.sum(ef, axis=-1, keepdims=True) + jnp.sum(en, axis=-1, keepdims=True)
                inv = jnp.where(m > 0.5 * NEG, 1.0 / l, 0.0)
                pf = ef * inv
                p_ref[0, rows, :w] = pf.astype(BF16)
                pn_ref[rows, :] = en * inv
                imp_ref[irows, :w] += pf
            return carry

        lax.fori_loop(0, P, cmp_head, 0)
        p_near = pn_ref[...]
        oc_ref[...] = (_dot(p_ref[0, :, :w], vc_ref[0, 0, 0, :w, :])
                       + _dot(p_near.astype(BF16), vn))

        imp_near = jnp.sum(p_near.reshape(P, TQ, CMP_NEAR), axis=0)
        jj = lax.broadcasted_iota(jnp.int32, (n_sel, CMP_NEAR), 0)
        jn = lax.broadcasted_iota(jnp.int32, (n_sel, CMP_NEAR), 1)
        d = (far_end - CMP_PAD_FRONT + jn) - (SEL_LEN // CMP_STRIDE) * jj
        w_near = jnp.where((d == 0) | (d == 4), 1.0,
                           jnp.where((d >= 1) & (d <= 3), 2.0, 0.0)).astype(BF16)
        pick_ref[...] = (_split_dot_nt(wfar_ref[:, :w], imp_ref[:, :w])
                         + _split_dot_nt(w_near, imp_near))

    max_far_end = (ks_ref.shape[1] // TQ - 1) * cmp_per_tile
    candidates = sorted(set(CMP_WIDTHS) | {ncp})
    widths = [w for w in candidates if w < max_far_end]
    widths.append(min(w for w in candidates if w >= max_far_end))
    lo = -1
    for w in widths:
        pl.when((far_end > lo) & (far_end <= w))(functools.partial(compressed, w))
        lo = w
    o_c = oc_ref[...]
    p_slc = pick_ref[...]

    bj = lax.broadcasted_iota(jnp.int32, (n_sel, TQ), 0)
    ti = lax.broadcasted_iota(jnp.int32, (n_sel, TQ), 1)
    jt = blocks_per_tile * qt + ti // SEL_LEN
    forced = (bj == 0) | (bj == jt) | (bj == jt - 1)
    pick_ref[...] = jnp.where(forced, -jnp.inf, jnp.where(bj > jt, NEG, p_slc))
    idx = lax.broadcasted_iota(jnp.int32, (n_sel, LANES), 0).astype(F32)

    def pick(_, sc):
        mx = jnp.max(sc, axis=0, keepdims=True)
        first = jnp.min(jnp.where(sc == mx, idx, float(n_sel)), axis=0, keepdims=True)
        return jnp.where(idx == first, -jnp.inf, sc)

    for h in range(0, TQ, LANES):
        pick_ref[:, h:h + LANES] = lax.fori_loop(0, top_n - 3, pick, pick_ref[:, h:h + LANES])
    selneg_t = jnp.where(pick_ref[...] == -jnp.inf, 0.0, NEG)
    sel_ref[...] = selneg_t.T
    selneg = sel_ref[...]
    bj = lax.broadcasted_iota(jnp.int32, (TQ, n_sel), 1)

    nb0 = jnp.maximum(blocks_per_tile * (qt - 1), 0)
    pj = lax.broadcasted_iota(jnp.int32, (n_sel, LANES), 0)
    pu = lax.broadcasted_iota(jnp.int32, (n_sel, LANES), 1)
    perm_near = ((pj == nb0 + pu) & (pu < 2 * blocks_per_tile)).astype(BF16)
    sel_near = _dot(selneg.astype(BF16), perm_near).astype(BF16)
    sel_far = jnp.where(bj < blocks_per_tile * (qt - 1), selneg, NEG).astype(BF16)
    sel_all = _dot(sel_far, perm_ref[...]).astype(BF16)
    n_kt = selt_ref.shape[0] - 1
    for kt in range(n_kt):
        selt_ref[kt] = sel_all[:, kt * LANES:(kt + 1) * LANES]
    col = lax.broadcasted_iota(jnp.int32, (TQ, LANES), 1)
    selt_ref[n_kt] = jnp.where(col < NSA_TK // SEL_LEN, NEG, 0.0).astype(BF16)

    _init_state(m_ref, l_ref, acc_ref)
    near_keys = 2 * NSA_TQ
    ns = pl.multiple_of(nb0 * SEL_LEN, NSA_TQ)
    lhs = jnp.concatenate([q4_ref[...], jnp.concatenate([sel_near] * P, axis=0)], axis=1)
    rhs = jnp.concatenate([ks_ref[0, pl.ds(ns, near_keys), :], eblk_ref[:near_keys, :]], axis=1)
    s_ref[0, :, :near_keys] = _dot_nt(lhs, rhs) + ds_ref[0, 0].astype(F32)

    def far_scores(kt, sel_idx, slot):
        k0 = pl.multiple_of(kt * NSA_TK, NSA_TK)
        lhs_t = jnp.concatenate(
            [q4_ref[...], jnp.concatenate([selt_ref[sel_idx]] * P, axis=0)], axis=1)
        rhs_t = jnp.concatenate([ks_ref[0, pl.ds(k0, NSA_TK), :], eblk_ref[...]], axis=1)
        s_ref[slot, :, :NSA_TK] = _dot_nt(lhs_t, rhs_t)

    def far_softmax(slot):
        _softmax_tile(slot, NSA_TK, *stats, NSA_ROWS)

    def far_values(kt, slot):
        k0 = pl.multiple_of(kt * NSA_TK, NSA_TK)
        _value_tile(slot, NSA_TK, vs_ref[0, pl.ds(k0, NSA_TK), :], p_ref, a_ref, acc_ref)

    n_far = (qt + 2) // 4
    n_pairs = n_far // 2
    odd = jnp.maximum(n_far - 1, 0)
    odd_sel = jnp.where(n_far % 2 == 1, odd, selt_ref.shape[0] - 1)
    last_pair_tile = jnp.maximum(n_far - 2, 0)
    far_scores(odd, odd_sel, 1)
    _fold_tile(0, near_keys, vs_ref[0, pl.ds(ns, near_keys), :], *state, NSA_ROWS)
    far_scores(0, 0, 0)
    far_softmax(1)

    def far_pair(i, carry):
        a = 2 * i
        far_values(jnp.where(i == 0, odd, a - 1), 1)
        far_scores(a + 1, a + 1, 1)
        far_softmax(0)
        nxt = jnp.minimum(a + 2, last_pair_tile)
        far_values(a, 0)
        far_scores(nxt, nxt, 0)
        far_softmax(1)
        return carry

    lax.fori_loop(0, n_pairs, far_pair, 0)
    far_values(jnp.where(n_pairs == 0, odd, 2 * n_pairs - 1), 1)
    o_s = _finish(l_ref, acc_ref)

    _init_state(m_ref, l_ref, acc_ref)
    kw = jnp.concatenate([kw0_ref[0], kw1_ref[0], kw2_ref[0]], axis=0)
    vw = jnp.concatenate([vw0_ref[0], vw1_ref[0], vw2_ref[0]], axis=0)
    s_ref[0, :, :WINDOW + NSA_TQ] = _dot_nt(q4_ref[...], kw) + dw_ref[0, 0].astype(F32)
    _fold_tile(0, WINDOW + NSA_TQ, vw, *state, NSA_ROWS)
    o_w = _finish(l_ref, acc_ref)

    gate = gate_ref[0]
    outs = []
    for p in range(P):
        rows = slice(p * TQ, (p + 1) * TQ)
        gc = gate[:, 3 * p:3 * p + 1]
        gs = gate[:, 3 * p + 1:3 * p + 2]
        gw = gate[:, 3 * p + 2:3 * p + 3]
        outs.append(gc * o_c[rows] + gs * o_s[rows] + gw * o_w[rows])
    o_ref[0] = jnp.concatenate(outs, axis=1).astype(o_ref.dtype)


def _nsa(proj3, gates3, cmp_pad, dc, ds, dw, perm, eblk, wfar, n_sel):
    B, S, _ = proj3.shape
    G = NSA_GROUPS
    ncp = cmp_pad.shape[3]
    top_n = min(SEL_TOPK, n_sel)
    assert top_n > 3, "the selection loop assumes the three forced blocks fit in the top-n"
    qw = NSA_HPG * HEAD_DIM
    R = NSA_HPG * NSA_TQ
    wide = max(NSA_TK, ncp, WINDOW + NSA_TQ)
    wblk = lambda col, back: pl.BlockSpec(
        (1, NSA_TQ, HEAD_DIM),
        lambda b, g, t, col=col, back=back: (b, jnp.maximum(t - back, 0), col // HEAD_DIM + g))
    kernel = functools.partial(_nsa_kernel, n_sel=n_sel, top_n=top_n)
    return pl.pallas_call(
        kernel,
        grid=(B, G, S // NSA_TQ),
        in_specs=[
            pl.BlockSpec((1, NSA_TQ, qw), lambda b, g, t: (b, t, g)),
            pl.BlockSpec((1, NSA_TQ, LANES), lambda b, g, t: (b, t, g)),
            pl.BlockSpec((1, 1, 1, ncp, HEAD_DIM), lambda b, g, t: (0, b, g, 0, 0)),
            pl.BlockSpec((1, 1, 1, ncp, HEAD_DIM), lambda b, g, t: (1, b, g, 0, 0)),
            pl.BlockSpec((1, S, HEAD_DIM), lambda b, g, t: (b, 0, COL_KS // HEAD_DIM + g)),
            pl.BlockSpec((1, S, HEAD_DIM), lambda b, g, t: (b, 0, COL_VS // HEAD_DIM + g)),
            wblk(COL_KW, 2), wblk(COL_KW, 1), wblk(COL_KW, 0),
            wblk(COL_VW, 2), wblk(COL_VW, 1), wblk(COL_VW, 0),
            pl.BlockSpec((1, 1) + dc.shape[2:], lambda b, g, t: (jnp.minimum(t, 1), g, 0, 0)),
            pl.BlockSpec((1, 1) + ds.shape[2:], lambda b, g, t: (jnp.minimum(t, 1), g, 0, 0)),
            pl.BlockSpec((1, 1) + dw.shape[2:], lambda b, g, t: (jnp.minimum(t, 2), g, 0, 0)),
            pl.BlockSpec(perm.shape, lambda b, g, t: (0, 0)),
            pl.BlockSpec(eblk.shape, lambda b, g, t: (0, 0)),
            pl.BlockSpec(wfar.shape, lambda b, g, t: (0, 0)),
        ],
        out_specs=pl.BlockSpec((1, NSA_TQ, qw), lambda b, g, t: (b, t, g)),
        out_shape=jax.ShapeDtypeStruct((B, S, NSA_HEADS * HEAD_DIM), BF16),
        scratch_shapes=[
            pltpu.VMEM((2, R, wide), F32),
            pltpu.VMEM((2, R, wide), BF16),
            pltpu.VMEM((2, R, LANES), F32),
            pltpu.VMEM((R, LANES), F32),
            pltpu.VMEM((R, LANES), F32),
            pltpu.VMEM((R, HEAD_DIM), F32),
            pltpu.VMEM((NSA_TQ, ncp), F32),
            pltpu.VMEM((R, CMP_NEAR), F32),
            pltpu.VMEM((R, CMP_NEAR), F32),
            pltpu.VMEM((S // NSA_TK + 1, NSA_TQ, LANES), BF16),
            pltpu.VMEM((NSA_TQ, n_sel), F32),
            pltpu.VMEM((R, HEAD_DIM), BF16),
            pltpu.VMEM((n_sel, NSA_TQ), F32),
            pltpu.VMEM((R, HEAD_DIM), F32),
        ],
        compiler_params=_params("parallel", "parallel", "arbitrary"),
    )(proj3, gates3, cmp_pad, cmp_pad, proj3, proj3,
      proj3, proj3, proj3, proj3, proj3, proj3, dc, ds, dw, perm, eblk, wfar)


def _diff_kernel(q_ref, k_ref, v_ref, dd_ref, lq1_ref, lk1_ref, lq2_ref, lk2_ref, hg_ref,
                 o_ref, s_ref, p_ref, m_ref, l_ref, acc_ref, *, lam_init):
    qt = pl.program_id(2)
    TQ = DIFF_TQ
    q = q_ref[0]
    zero = jnp.zeros((TQ, HEAD_DIM), BF16)
    lhs = jnp.concatenate([
        jnp.concatenate([q[:, :HEAD_DIM], zero], axis=1),
        jnp.concatenate([zero, q[:, HEAD_DIM:]], axis=1)], axis=0)
    lam = (jnp.exp(jnp.sum(lq1_ref[...] * lk1_ref[...], axis=-1, keepdims=True))
           - jnp.exp(jnp.sum(lq2_ref[...] * lk2_ref[...], axis=-1, keepdims=True)) + lam_init)

    def scores(kt, slot, bias):
        k0 = pl.multiple_of(kt * DIFF_TK, DIFF_TK)
        sc = _dot_nt(lhs, k_ref[0, pl.ds(k0, DIFF_TK), :])
        if bias is not None:
            sc = (sc.reshape(2, TQ, DIFF_TK) + bias[None]).reshape(2 * TQ, DIFF_TK)
        s_ref[slot] = sc

    def fold(kt, slot):
        k0 = pl.multiple_of(kt * DIFF_TK, DIFF_TK)
        _fold_tile_inplace(slot, DIFF_TK, v_ref[0, pl.ds(k0, DIFF_TK), :],
                           s_ref, p_ref, m_ref, l_ref, acc_ref, DIFF_ROWS)

    _init_state(m_ref, l_ref, acc_ref)
    n_far = jnp.maximum(qt - 1, 0)
    prev = jnp.maximum(qt - 1, 0)
    odd = jnp.maximum(n_far - 1, 0)
    last_pair_tile = jnp.maximum(n_far - 2, 0)
    scores(qt, 0, dd_ref[0, :, DIFF_TK:].astype(F32))
    fold(qt, 0)
    scores(prev, 0, jnp.where(qt >= 1, dd_ref[0, :, :DIFF_TK].astype(F32), NEG))
    fold(prev, 0)
    scores(0, 1, None)

    @pl.when(n_far % 2 == 1)
    def _():
        scores(odd, 0, None)
        fold(odd, 0)

    def far_pair(i, carry):
        a = 2 * i
        scores(a + 1, 0, None)
        fold(a, 1)
        scores(jnp.minimum(a + 2, last_pair_tile), 1, None)
        fold(a + 1, 0)
        return carry

    lax.fori_loop(0, n_far // 2, far_pair, 0)
    a = _finish(l_ref, acc_ref)
    o = a[:TQ] - lam * a[TQ:]
    y = _norm_rows(o, hg_ref[0]) * (1.0 - lam_init)
    o_ref[0] = y.astype(o_ref.dtype)


def _diff(proj3, dd, lq1, lk1, lq2, lk2, hg, lam_init):
    B, S, _ = proj3.shape
    H = DIFF_HEADS
    w = 2 * HEAD_DIM
    R = 2 * DIFF_TQ
    vec = pl.BlockSpec((1, HEAD_DIM), lambda b, h, t: (0, 0))
    kernel = functools.partial(_diff_kernel, lam_init=lam_init)
    return pl.pallas_call(
        kernel,
        grid=(B, H, S // DIFF_TQ),
        in_specs=[
            pl.BlockSpec((1, DIFF_TQ, w), lambda b, h, t: (b, t, COL_QD // w + h)),
            pl.BlockSpec((1, S, w), lambda b, h, t: (b, 0, COL_KD // w + h)),
            pl.BlockSpec((1, S, w), lambda b, h, t: (b, 0, COL_VD // w + h)),
            pl.BlockSpec((1,) + dd.shape[1:], lambda b, h, t: (h, 0, 0)),
            vec, vec, vec, vec,
            pl.BlockSpec((1, 1, w), lambda b, h, t: (h, 0, 0)),
        ],
        out_specs=pl.BlockSpec((1, DIFF_TQ, w), lambda b, h, t: (b, t, h)),
        out_shape=jax.ShapeDtypeStruct((B, S, H * w), BF16),
        scratch_shapes=[
            pltpu.VMEM((2, R, DIFF_TK), F32),
            pltpu.VMEM((1, R, DIFF_TK), BF16),
            pltpu.VMEM((R, LANES), F32),
            pltpu.VMEM((R, LANES), F32),
            pltpu.VMEM((R, DIFF_V_DIM), F32),
        ],
        compiler_params=_params("parallel", "parallel", "arbitrary"),
    )(proj3, proj3, proj3, dd, lq1, lk1, lq2, lk2, hg)


def _merge_kernel(on_ref, od_ref, wn_ref, wd_ref, ga_ref, gb_ref, o_ref):
    u1 = _dot(on_ref[...], wn_ref[...])
    u2 = _dot(od_ref[...], wd_ref[...])
    ga = jax.nn.sigmoid(ga_ref[...].astype(F32))
    gb = jax.nn.sigmoid(gb_ref[...].astype(F32))
    o_ref[...] = (ga * u1 + gb * u2).astype(o_ref.dtype)


def _merge(o_nsa, o_d, wn, wd, proj, tm=1024, tn=512):
    T, kn = o_nsa.shape
    kd = o_d.shape[1]
    return pl.pallas_call(
        _merge_kernel,
        grid=(T // tm, D_MODEL // tn),
        in_specs=[
            pl.BlockSpec((tm, kn), lambda i, j: (i, 0)),
            pl.BlockSpec((tm, kd), lambda i, j: (i, 0)),
            pl.BlockSpec((kn, tn), lambda i, j: (0, j)),
            pl.BlockSpec((kd, tn), lambda i, j: (0, j)),
            pl.BlockSpec((tm, tn), lambda i, j: (i, COL_GM // tn + j)),
            pl.BlockSpec((tm, tn), lambda i, j: (i, (COL_GM + D_MODEL) // tn + j)),
        ],
        out_specs=pl.BlockSpec((tm, tn), lambda i, j: (i, j)),
        out_shape=jax.ShapeDtypeStruct((T, D_MODEL), BF16),
        compiler_params=_params("parallel", "arbitrary"),
    )(o_nsa, o_d, wn, wd, proj, proj)


def _out_proj_kernel(a_ref, w_ref, x_ref, o_ref):
    o_ref[...] = x_ref[...] + _dot(a_ref[...], w_ref[...])


def _out_proj(a, w, x2, tm=512, tn=2048):
    T, k = a.shape
    return pl.pallas_call(
        _out_proj_kernel,
        grid=(T // tm, D_MODEL // tn),
        in_specs=[
            pl.BlockSpec((tm, k), lambda i, j: (i, 0)),
            pl.BlockSpec((k, tn), lambda i, j: (0, j)),
            pl.BlockSpec((tm, tn), lambda i, j: (i, j)),
        ],
        out_specs=pl.BlockSpec((tm, tn), lambda i, j: (i, j)),
        out_shape=jax.ShapeDtypeStruct((T, D_MODEL), F32),
        compiler_params=_params("parallel", "arbitrary"),
    )(a, w, x2)


def _merge_out_kernel(on_ref, od_ref, wn_ref, wd_ref, *rest):
    gate_refs, (wo_ref, x_ref, o_ref) = rest[:-3], rest[-3:]
    n = len(gate_refs) // 2
    u1 = _dot(on_ref[...], wn_ref[...])
    u2 = _dot(od_ref[...], wd_ref[...])
    ga = jax.nn.sigmoid(jnp.concatenate([r[...] for r in gate_refs[:n]], axis=1).astype(F32))
    gb = jax.nn.sigmoid(jnp.concatenate([r[...] for r in gate_refs[n:]], axis=1).astype(F32))
    mix = (ga * u1 + gb * u2).astype(BF16)
    o_ref[...] = x_ref[...] + _dot(mix, wo_ref[...])


def _merge_out(o_nsa, o_d, wn, wd, proj, wo, x2, tm=512, gw=512):
    T, kn = o_nsa.shape
    kd = o_d.shape[1]
    n = D_MODEL // gw
    const = lambda shape: pl.BlockSpec(shape, lambda i: (0, 0), pipeline_mode=pl.Buffered(1))
    gates = [pl.BlockSpec((tm, gw), lambda i, c=c: (i, COL_GM // gw + c)) for c in range(2 * n)]
    return pl.pallas_call(
        _merge_out_kernel,
        grid=(T // tm,),
        in_specs=[pl.BlockSpec((tm, kn), lambda i: (i, 0)), pl.BlockSpec((tm, kd), lambda i: (i, 0)),
                  const((kn, D_MODEL)), const((kd, D_MODEL))] + gates
                 + [const((D_MODEL, D_MODEL)), pl.BlockSpec((tm, D_MODEL), lambda i: (i, 0))],
        out_specs=pl.BlockSpec((tm, D_MODEL), lambda i: (i, 0)),
        out_shape=jax.ShapeDtypeStruct((T, D_MODEL), F32),
        compiler_params=_params("parallel"),
    )(o_nsa, o_d, wn, wd, *([proj] * (2 * n)), wo, x2)


def _ffn_in_kernel(x_ref, g_ref, wa_ref, wb_ref, o_ref, h_ref):
    @pl.when(pl.program_id(1) == 0)
    def _():
        h_ref[...] = _norm_rows(x_ref[...], g_ref[...]).astype(BF16)

    h = h_ref[...]
    a = _dot(h, wa_ref[...])
    b = _dot(h, wb_ref[...])
    o_ref[...] = (a * jax.nn.sigmoid(a) * b).astype(o_ref.dtype)


def _ffn_in(x2, g, w, tm=1024, tn=512):
    T = x2.shape[0]
    nb = D_FF // tn
    return pl.pallas_call(
        _ffn_in_kernel,
        grid=(T // tm, nb),
        in_specs=[
            pl.BlockSpec((tm, D_MODEL), lambda i, j: (i, 0)),
            pl.BlockSpec((1, D_MODEL), lambda i, j: (0, 0)),
            pl.BlockSpec((D_MODEL, tn), lambda i, j: (0, j)),
            pl.BlockSpec((D_MODEL, tn), lambda i, j: (0, j + nb)),
        ],
        out_specs=pl.BlockSpec((tm, tn), lambda i, j: (i, j)),
        out_shape=jax.ShapeDtypeStruct((T, D_FF), BF16),
        scratch_shapes=[pltpu.VMEM((tm, D_MODEL), BF16)],
        compiler_params=_params("parallel", "arbitrary"),
    )(x2, g, w, w)


def _ffn_out_kernel(a_ref, w_ref, x_ref, g_ref, o_ref, acc_ref, *, final_norm):
    k = pl.program_id(1)

    @pl.when(k == 0)
    def _():
        acc_ref[...] = x_ref[...]

    acc_ref[...] += _dot(a_ref[...], w_ref[...])

    @pl.when(k == pl.num_programs(1) - 1)
    def _():
        y = acc_ref[...]
        o_ref[...] = _norm_rows(y, g_ref[...]) if final_norm else y


def _ffn_out(a, w, x2, g, final_norm, tm=512, tk=2816):
    T = a.shape[0]
    kernel = functools.partial(_ffn_out_kernel, final_norm=final_norm)
    return pl.pallas_call(
        kernel,
        grid=(T // tm, D_FF // tk),
        in_specs=[
            pl.BlockSpec((tm, tk), lambda i, k: (i, k)),
            pl.BlockSpec((tk, D_MODEL), lambda i, k: (k, 0)),
            pl.BlockSpec((tm, D_MODEL), lambda i, k: (i, 0)),
            pl.BlockSpec((1, D_MODEL), lambda i, k: (0, 0)),
        ],
        out_specs=pl.BlockSpec((tm, D_MODEL), lambda i, k: (i, 0)),
        out_shape=jax.ShapeDtypeStruct((T, D_MODEL), F32),
        scratch_shapes=[pltpu.VMEM((tm, D_MODEL), F32)],
        compiler_params=_params("parallel", "arbitrary"),
    )(a, w, x2, g)


def _t5_bucket(rel):
    n = jnp.maximum(rel, 0)
    nf = jnp.maximum(n, 1).astype(F32)
    large = REL_MAX_EXACT + (jnp.log(nf / REL_MAX_EXACT) / math.log(REL_MAX_DIST / REL_MAX_EXACT)
                             * (REL_BUCKETS - REL_MAX_EXACT)).astype(jnp.int32)
    large = jnp.minimum(large, REL_BUCKETS - 1)
    return jnp.where(n < REL_MAX_EXACT, n, large)


def _bias_of_rel(table, rel, valid):
    shifted = (table - table[REL_BUCKETS - 1:REL_BUCKETS]) * LOG2E
    vals = jnp.moveaxis(shifted[_t5_bucket(rel)], -1, 0)
    return jnp.where(valid[None], vals, NEG)


def _toeplitz_bias(table, n_i, n_j, offset, max_rel=None):
    period = REL_MAX_DIST
    near = jnp.arange(period, dtype=jnp.int32)
    g = _bias_of_rel(table, near, near >= 0)
    w = jnp.roll(g[:, ::-1], (offset + 1) % period, axis=1)
    circ = jnp.tile(w, (1, 2 * period))[:, :period * (2 * period - 1)]
    circ = circ.reshape(-1, period, 2 * period - 1)[:, :, :period]
    band = jnp.tile(circ, (1, n_i // period, n_j // period))
    i = jnp.arange(n_i, dtype=jnp.int32)[:, None]
    j = jnp.arange(n_j, dtype=jnp.int32)[None, :]
    rel = (i - j + offset)[None]
    far = 0.0 if max_rel is None else jnp.where(rel < max_rel, 0.0, NEG)
    return jnp.where(rel < 0, NEG, jnp.where(rel < period, band, far))


def _nsa_tables(table_nsa):
    TQ = NSA_TQ

    def stack(t):
        return t.reshape(NSA_GROUPS, NSA_HPG * TQ, t.shape[-1])

    i = jnp.arange(TQ, dtype=jnp.int32)[:, None]
    jc = jnp.arange(CMP_NEAR, dtype=jnp.int32)[None, :]
    rel_c = i - CMP_STRIDE * (jc - CMP_PAD_FRONT) - (CMP_LEN - 1)
    dc = jnp.stack([
        stack(_bias_of_rel(table_nsa, rel_c, (rel_c >= 0) & (jc >= CMP_PAD_FRONT))),
        stack(_bias_of_rel(table_nsa, rel_c, rel_c >= 0)),
    ])
    ds = jnp.stack([
        stack(_toeplitz_bias(table_nsa, TQ, 2 * TQ, 0)),
        stack(_toeplitz_bias(table_nsa, TQ, 2 * TQ, TQ)),
    ]).astype(BF16)
    win = stack(_toeplitz_bias(table_nsa, TQ, WINDOW + TQ, WINDOW, WINDOW))
    jw = jnp.arange(WINDOW + TQ, dtype=jnp.int32)[None, None, :]
    dw = jnp.stack([jnp.where(jw >= WINDOW - c * TQ, win, NEG) for c in range(3)]).astype(BF16)
    return dc, ds, dw


def _diff_table(table_diff):
    return _toeplitz_bias(table_diff, DIFF_TQ, 2 * DIFF_TK, DIFF_TK).astype(BF16)


def _selection_constants(S):
    n_sel = S // SEL_LEN
    n_cmp = S // CMP_STRIDE
    n_kt = S // NSA_TK
    blocks = NSA_TK // SEL_LEN
    j = jnp.arange(n_sel, dtype=jnp.int32)[:, None]
    col = jnp.arange(n_kt * LANES, dtype=jnp.int32)[None, :]
    perm = ((j == blocks * (col // LANES) + col % LANES) & (col % LANES < blocks)).astype(BF16)
    key = jnp.arange(NSA_TK, dtype=jnp.int32)[:, None]
    u = jnp.arange(LANES, dtype=jnp.int32)[None, :]
    eblk = (u == key // SEL_LEN).astype(BF16)
    c = jnp.arange(n_cmp + CMP_PAD_FRONT + CMP_PAD_BACK, dtype=jnp.int32)[None, :] - CMP_PAD_FRONT
    jb = jnp.arange(n_sel, dtype=jnp.int32)[:, None]
    d = c - (SEL_LEN // CMP_STRIDE) * jb
    wfar = jnp.where((d == 0) | (d == 4), 1.0, jnp.where((d >= 1) & (d <= 3), 2.0, 0.0))
    wfar = jnp.where((c >= 0) & (c < n_cmp), wfar, 0.0).astype(BF16)
    return perm, eblk, wfar


def kernel(x, norm_mix_g, w_in, cmp_pe_k, cmp_pe_v, cmp_w1_k, cmp_w2_k, cmp_w1_v, cmp_w2_v,
           diff_lq1, diff_lk1, diff_lq2, diff_lk2, diff_head_g, w_up_nsa, w_up_diff, w_out,
           norm_ff_g, w_ff_in, w_ff_out, rel_bias_table, norm_final_g):
    B, S, D = x.shape
    T = B * S
    depth = w_in.shape[0]
    n_sel = S // SEL_LEN
    n_cmp = S // CMP_STRIDE
    qscale = HEAD_DIM ** -0.5 * LOG2E

    dc, ds, dw = _nsa_tables(rel_bias_table[:, :NSA_HEADS])
    dd = _diff_table(rel_bias_table[:, NSA_HEADS:])
    perm, eblk, wfar = _selection_constants(S)

    gn0 = COL_QD
    gn1 = gn0 + NSA_HEADS * 3
    cs = jnp.ones((1, N_PROJ), F32)
    cs = cs.at[:, COL_QN:COL_KC].set(qscale).at[:, COL_QD:COL_KD].set(qscale)

    x2 = x.reshape(T, D)
    for l in range(depth):
        w_main = jnp.concatenate([w_in[l][:, :gn0], w_in[l][:, gn1:]], axis=1).astype(BF16)
        wg = w_in[l][:, gn0:gn1].reshape(D, NSA_GROUPS, NSA_HPG * 3)
        wg = jnp.pad(wg, ((0, 0), (0, 0), (0, LANES - NSA_HPG * 3))).reshape(D, NSA_GROUPS * LANES)
        proj, gates = _in_proj(x2, norm_mix_g[l][None], w_main, cs, wg.astype(BF16))
        proj3 = proj.reshape(B, S, N_PROJ)
        gates3 = gates.reshape(B, S, NSA_GROUPS * LANES)

        halves = proj3[:, :, COL_KC:COL_KS].reshape(B, n_cmp, CMP_STRIDE, 2, NSA_GROUPS, HEAD_DIM)
        halves = halves.transpose(3, 0, 4, 1, 2, 5).reshape(
            2, B, NSA_GROUPS, n_cmp, CMP_STRIDE * HEAD_DIM)
        pe = jnp.stack([cmp_pe_k[l], cmp_pe_v[l]]).reshape(2, 1, CMP_LEN * HEAD_DIM)
        w1 = jnp.stack([cmp_w1_k[l], cmp_w1_v[l]]).astype(BF16)
        w2 = jnp.stack([cmp_w2_k[l], cmp_w2_v[l]]).astype(BF16)
        cmp_pad = _compress(halves, pe, w1, w2)

        o_nsa = _nsa(proj3, gates3, cmp_pad, dc, ds, dw, perm, eblk, wfar, n_sel)

        lam_init = 0.8 - 0.6 * math.exp(-0.3 * l)
        o_d = _diff(proj3, dd, diff_lq1[l][None], diff_lk1[l][None], diff_lq2[l][None],
                    diff_lk2[l][None], diff_head_g[l][:, None, :], lam_init)

        x2 = _merge_out(o_nsa.reshape(T, -1), o_d.reshape(T, -1), w_up_nsa[l].astype(BF16),
                        w_up_diff[l].astype(BF16), proj, w_out[l].astype(BF16), x2)

        act = _ffn_in(x2, norm_ff_g[l][None], w_ff_in[l].astype(BF16))
        last = l == depth - 1
        x2 = _ffn_out(act, w_ff_out[l].astype(BF16), x2,
                      norm_final_g[None] if last else norm_ff_g[l][None], final_norm=last)
    return x2.reshape(B, S, D)
```
